```python
import jax, jax.numpy as jnp
from jax import lax
import numpy as np

D_MODEL = 1024
BATCH = 8
SEQ = 8192
DEPTH = 1

GRID_W = 64
CTX_LEN = 256
CHUNK = 128
RET_HEADS = 4
RET_DK = 128
RET_DV = 128
ML_HEADS = 4
ML_DH = 128
RET_W = RET_HEADS * RET_DK
RET_VW = RET_HEADS * RET_DV
ML_W = ML_HEADS * ML_DH
N_ML_GATES = 4 * ML_HEADS
FFN_HIDDEN = -((-8 * D_MODEL) // (3 * 256)) * 256
ROPE_BASE = 10000.0
EPS = 1e-6
IN_WIDTHS = (RET_W, RET_W, RET_VW, RET_VW, ML_W, ML_W, ML_W, ML_W, N_ML_GATES, D_MODEL, D_MODEL)
IN_COLS = sum(IN_WIDTHS)

kernel_name = "hybrid_retention_mlstm_prefix_dit_block"


def rms_norm(x, gain):
    xf = x.astype(jnp.float32)
    y = xf * lax.rsqrt(jnp.mean(xf * xf, axis=-1, keepdims=True) + EPS)
    return (y * gain.astype(jnp.float32)).astype(x.dtype)


def head_norm(t):
    tf = t.astype(jnp.float32)
    return tf * lax.rsqrt(jnp.mean(tf * tf, axis=-1, keepdims=True) + EPS)


def modulate(h, shift, scale):
    return h * (1 + scale) + shift


def to_heads(t, n_heads):
    b, n, _ = t.shape
    return t.reshape(b, n, n_heads, -1).transpose(0, 2, 1, 3)


def from_heads(t):
    b, h, n, d = t.shape
    return t.transpose(0, 2, 1, 3).reshape(b, n, h * d)


def rope_2d(n):
    n_rows = n // GRID_W
    rows = jnp.broadcast_to(jnp.arange(n_rows, dtype=jnp.float32)[:, None], (n_rows, GRID_W)).reshape(n)
    cols = jnp.broadcast_to(jnp.arange(GRID_W, dtype=jnp.float32)[None, :], (n_rows, GRID_W)).reshape(n)
    n_freq = RET_DK // 4
    inv = ROPE_BASE ** (-jnp.arange(n_freq, dtype=jnp.float32) / n_freq)
    ang = jnp.concatenate([rows[:, None] * inv, cols[:, None] * inv], axis=-1)
    return jnp.cos(ang), jnp.sin(ang)


def apply_rope(t, cos, sin):
    half = t.shape[-1] // 2
    t1, t2 = t[..., :half], t[..., half:]
    cos = cos.astype(t.dtype)
    sin = sin.astype(t.dtype)
    return jnp.concatenate([t1 * cos - t2 * sin, t2 * cos + t1 * sin], axis=-1)


def to_chunks(t):
    b, h, n = t.shape[:3]
    t = t.reshape(b, h, n // CHUNK, CHUNK, *t.shape[3:])
    return jnp.moveaxis(t, 2, 0)


def from_chunks(t):
    t = jnp.moveaxis(t, 0, 2)
    b, h, nc, l = t.shape[:4]
    return t.reshape(b, h, nc * l, *t.shape[4:])


def retention_scan(q, k, v, log_gamma, state):
    q, k, v = (t.astype(jnp.float32) for t in (q, k, v))
    lg = log_gamma.astype(jnp.float32)
    pos = jnp.arange(CHUNK, dtype=jnp.float32)
    rel = pos[:, None] - pos[None, :]
    intra = jnp.where(rel >= 0, jnp.exp(lg[:, None, None] * jnp.maximum(rel, 0.0)), 0.0)
    inter = jnp.exp(lg[:, None] * (pos + 1.0))
    to_end = jnp.exp(lg[:, None] * (CHUNK - 1.0 - pos))
    chunk_decay = jnp.exp(lg * CHUNK)

    def step(s, inp):
        qc, kc, vc = inp
        scores = jnp.einsum("bhld,bhsd->bhls", qc, kc) * intra
        out = jnp.einsum("bhls,bhsv->bhlv", scores, vc) + inter[..., None] * jnp.einsum("bhld,bhdv->bhlv", qc, s)
        s_new = chunk_decay[:, None, None] * s + jnp.einsum("bhsd,bhsv->bhdv", kc * to_end[..., None], vc)
        return s_new, out

    final, outs = lax.scan(step, state, (to_chunks(q), to_chunks(k), to_chunks(v)))
    return from_chunks(outs), final


def mlstm_scan(q, k, v, i_pre, log_f, state):
    q, k, v, i_pre, log_f = (t.astype(jnp.float32) for t in (q, k, v, i_pre, log_f))
    causal = jnp.tril(jnp.ones((CHUNK, CHUNK), dtype=bool))

    def step(carry, inp):
        c_mat, n_vec, m = carry
        qc, kc, vc, ic, fc = inp
        b_cum = jnp.cumsum(fc, axis=-1)
        log_inter = b_cum + m[..., None]
        log_intra = jnp.where(causal, b_cum[..., :, None] - b_cum[..., None, :] + ic[..., None, :], -jnp.inf)
        m_t = jnp.maximum(log_inter, jnp.max(log_intra, axis=-1))
        w_inter = jnp.exp(log_inter - m_t)
        w_intra = jnp.exp(log_intra - m_t[..., None])
        s = jnp.einsum("bhld,bhsd->bhls", qc, kc) * w_intra
        num = jnp.einsum("bhls,bhsv->bhlv", s, vc) + w_inter[..., None] * jnp.einsum("bhvd,bhld->bhlv", c_mat, qc)
        den = jnp.sum(s, axis=-1) + w_inter * jnp.einsum("bhd,bhld->bhl", n_vec, qc)
        h = num / jnp.maximum(jnp.abs(den), jnp.exp(-m_t))[..., None]
        b_last = b_cum[..., -1]
        log_src = b_last[..., None] - b_cum + ic
        m_new = jnp.maximum(b_last + m, jnp.max(log_src, axis=-1))
        w_old = jnp.exp(b_last + m - m_new)
        w_src = jnp.exp(log_src - m_new[..., None])
        c_new = w_old[..., None, None] * c_mat + jnp.einsum("bhs,bhsv,bhsd->bhvd", w_src, vc, kc)
        n_new = w_old[..., None] * n_vec + jnp.einsum("bhs,bhsd->bhd", w_src, kc)
        return (c_new, n_new, m_new), h

    final, outs = lax.scan(step, state, tuple(to_chunks(t) for t in (q, k, v, i_pre, log_f)))
    return from_chunks(outs), final


def zero_states(b):
    ret = jnp.zeros((b, RET_HEADS, RET_DK, RET_DV), jnp.float32)
    ml = (jnp.zeros((b, ML_HEADS, ML_DH, ML_DH), jnp.float32),
          jnp.zeros((b, ML_HEADS, ML_DH), jnp.float32),
          jnp.zeros((b, ML_HEADS), jnp.float32))
    return (ret, ml)


def mixer_inputs(h, w_in, gate_bias, rope):
    offsets = np.cumsum(IN_WIDTHS)[:-1].tolist()
    rq, rk, rv, rg, mq, mk, mv, mo, mg, bg_ret, bg_ml = (h @ w for w in jnp.split(w_in, offsets, axis=-1))
    rq = to_heads(rq, RET_HEADS)
    rk = to_heads(rk, RET_HEADS) * (RET_DK ** -0.5)
    if rope is not None:
        rq = apply_rope(rq, *rope)
        rk = apply_rope(rk, *rope)
    rv = to_heads(rv, RET_HEADS)
    mq = to_heads(mq, ML_HEADS)
    mk = to_heads(mk, ML_HEADS) * (ML_DH ** -0.5)
    mv = to_heads(mv, ML_HEADS)
    b, n, _ = mg.shape
    mg = (mg + gate_bias.reshape(-1)).reshape(b, n, 4, ML_HEADS).transpose(2, 0, 3, 1)
    i_f, lf_f, i_b, lf_b = mg[0], jax.nn.log_sigmoid(mg[1]), mg[2], jax.nn.log_sigmoid(mg[3])
    feats = (rq, rk, rv, mq, mk, mv, i_f, lf_f, i_b, lf_b)
    post = (rg, mo, bg_ret, bg_ml)
    return feats, post


def bidirectional_scans(feats, log_gamma, st_f, st_b):
    rq, rk, rv, mq, mk, mv, i_f, lf_f, i_b, lf_b = feats
    flip = lambda t: jnp.flip(t, axis=2)
    ret_f, r_state_f = retention_scan(rq, rk, rv, log_gamma[0], st_f[0])
    ml_f, m_state_f = mlstm_scan(mq, mk, mv, i_f, lf_f, st_f[1])
    ret_b, r_state_b = retention_scan(flip(rq), flip(rk), flip(rv), log_gamma[1], st_b[0])
    ml_b, m_state_b = mlstm_scan(flip(mq), flip(mk), flip(mv), flip(i_b), flip(lf_b), st_b[1])
    return ret_f + flip(ret_b), ml_f + flip(ml_b), (r_state_f, m_state_f), (r_state_b, m_state_b)


def mixer_output(ret, ml, rg, mo, bg_ret, bg_ml, w_ret_up, w_ml_up, w_out):
    dtype = rg.dtype
    y_ret = from_heads(head_norm(ret)).astype(dtype) * jax.nn.silu(rg)
    y_ml = from_heads(head_norm(jax.nn.sigmoid(to_heads(mo, ML_HEADS)) * ml)).astype(dtype)
    merged = jax.nn.sigmoid(bg_ret) * (y_ret @ w_ret_up) + jax.nn.sigmoid(bg_ml) * (y_ml @ w_ml_up)
    return merged @ w_out


def swiglu(h, w_ffn_in, w_ffn_out):
    gate, up = jnp.split(h @ w_ffn_in, 2, axis=-1)
    return (jax.nn.silu(gate) * up) @ w_ffn_out


def setup_inputs(seed: int = 0) -> dict:
    key = jax.random.key(seed)
    ks = jax.random.split(key, 20)
    f32 = jnp.float32

    def nrm(k, shape, scale):
        return jax.random.normal(k, shape, f32) * scale

    x = nrm(ks[0], (BATCH, SEQ, D_MODEL), 1.0)
    c = nrm(ks[1], (BATCH, D_MODEL), 1.0)
    ctx = nrm(ks[2], (BATCH, CTX_LEN, D_MODEL), 1.0)
    c_ctx = nrm(ks[3], (D_MODEL,), 1.0)
    w_ada = nrm(ks[4], (DEPTH, D_MODEL, 6 * D_MODEL), 0.5 * D_MODEL ** -0.5)
    b_ada = nrm(ks[5], (DEPTH, 6 * D_MODEL), 0.02)
    norm1_gain = 1.0 + nrm(ks[6], (DEPTH, D_MODEL), 0.02)
    norm2_gain = 1.0 + nrm(ks[7], (DEPTH, D_MODEL), 0.02)
    w_in = nrm(ks[8], (DEPTH, D_MODEL, IN_COLS), D_MODEL ** -0.5)
    i_bias = nrm(ks[9], (DEPTH, 2, ML_HEADS), 0.1)
    f_bias = jnp.linspace(3.0, 6.0, ML_HEADS, dtype=f32) + nrm(ks[10], (DEPTH, 2, ML_HEADS), 0.1)
    mlstm_gate_bias = jnp.stack([i_bias[:, 0], f_bias[:, 0], i_bias[:, 1], f_bias[:, 1]], axis=1)
    gamma = 1.0 - 2.0 ** (-5.0 - jnp.arange(RET_HEADS, dtype=f32))
    ret_decay_logit = jnp.log(gamma / (1.0 - gamma)) + nrm(ks[11], (DEPTH, 2, RET_HEADS), 0.1)
    w_ret_up = nrm(ks[12], (DEPTH, RET_VW, D_MODEL), RET_VW ** -0.5)
    w_ml_up = nrm(ks[13], (DEPTH, ML_W, D_MODEL), ML_W ** -0.5)
    w_out = nrm(ks[14], (DEPTH, D_MODEL, D_MODEL), D_MODEL ** -0.5)
    w_ffn_in = nrm(ks[15], (DEPTH, D_MODEL, 2 * FFN_HIDDEN), D_MODEL ** -0.5)
    w_ffn_out = nrm(ks[16], (DEPTH, FFN_HIDDEN, D_MODEL), FFN_HIDDEN ** -0.5)
    final_gain = 1.0 + nrm(ks[17], (D_MODEL,), 0.02)
    return {"x": x, "c": c, "ctx": ctx, "c_ctx": c_ctx, "w_ada": w_ada, "b_ada": b_ada,
            "norm1_gain": norm1_gain, "norm2_gain": norm2_gain, "w_in": w_in,
            "mlstm_gate_bias": mlstm_gate_bias, "ret_decay_logit": ret_decay_logit,
            "w_ret_up": w_ret_up, "w_ml_up": w_ml_up, "w_out": w_out,
            "w_ffn_in": w_ffn_in, "w_ffn_out": w_ffn_out, "final_gain": final_gain}


def reference(x, c, ctx, c_ctx, w_ada, b_ada, norm1_gain, norm2_gain, w_in, mlstm_gate_bias,
              ret_decay_logit, w_ret_up, w_ml_up, w_out, w_ffn_in, w_ffn_out, final_gain):
    b, n, _ = x.shape
    rope = rope_2d(n)
    log_gamma = jax.nn.log_sigmoid(ret_decay_logit.astype(jnp.float32))
    for layer in range(DEPTH):
        mod = jax.nn.silu(c) @ w_ada[layer] + b_ada[layer]
        sh1, sc1, g1, sh2, sc2, g2 = jnp.split(mod[:, None, :], 6, axis=-1)
        mod_c = jax.nn.silu(c_ctx) @ w_ada[layer] + b_ada[layer]
        csh1, csc1, cg1, csh2, csc2, cg2 = jnp.split(mod_c, 6)

        hc = modulate(rms_norm(ctx, norm1_gain[layer]), csh1, csc1)
        feats_c, post_c = mixer_inputs(hc, w_in[layer], mlstm_gate_bias[layer], None)
        zero = zero_states(b)
        ret_c, ml_c, st_f, st_b = bidirectional_scans(feats_c, log_gamma[layer], zero, zero)

        hx = modulate(rms_norm(x, norm1_gain[layer]), sh1, sc1)
        feats_x, post_x = mixer_inputs(hx, w_in[layer], mlstm_gate_bias[layer], rope)
        ret_x, ml_x, _, _ = bidirectional_scans(feats_x, log_gamma[layer], st_f, st_b)
        x = x + g1 * mixer_output(ret_x, ml_x, *post_x, w_ret_up[layer], w_ml_up[layer], w_out[layer])
        x = x + g2 * swiglu(modulate(rms_norm(x, norm2_gain[layer]), sh2, sc2), w_ffn_in[layer], w_ffn_out[layer])

        if layer + 1 < DEPTH:
            ctx = ctx + cg1 * mixer_output(ret_c, ml_c, *post_c, w_ret_up[layer], w_ml_up[layer], w_out[layer])
            ctx = ctx + cg2 * swiglu(modulate(rms_norm(ctx, norm2_gain[layer]), csh2, csc2),
                                     w_ffn_in[layer], w_ffn_out[layer])
    return rms_norm(x, final_gain)
```

```python
import functools

import jax
import jax.numpy as jnp
from jax import lax
from jax.experimental import pallas as pl
from jax.experimental.pallas import tpu as pltpu

HEADS = 4
HEAD_DIM = 128
MIX_W = HEADS * HEAD_DIM
GRID_W = 64
ROPE_BASE = 10000.0
EPS = 1e-6
SCAN_CHUNK = 256
PROJ_TOKENS = 512
FFN_TOKENS = 512
FFN_SPLITS = (1536, 1280)
VMEM_LIMIT = 56 * 1024 * 1024

F32 = jnp.float32
BF16 = jnp.bfloat16


def _dot(a, b):
    return jnp.dot(a, b, preferred_element_type=F32)


def _dot_nt(a, b):
    return lax.dot_general(a, b, (((1,), (1,)), ((), ())), preferred_element_type=F32)


def _sigmoid(t):
    return 1.0 / (1.0 + jnp.exp(-t))


def _silu(t):
    return t * _sigmoid(t)


def _log_sigmoid(t):
    return jnp.minimum(t, 0.0) - jnp.log1p(jnp.exp(-jnp.abs(t)))


def _rms(t):
    return t * lax.rsqrt(jnp.mean(t * t, axis=-1, keepdims=True) + EPS)


def _split3(t):
    hi = t.astype(BF16)
    r1 = t - hi.astype(F32)
    mid = r1.astype(BF16)
    lo = (r1 - mid.astype(F32)).astype(BF16)
    return hi, mid, lo


def _adaln_kernel(c_ref, w_ref, b_ref, o_ref):
    s = _silu(c_ref[...]).astype(BF16)
    o_ref[...] = _dot(s, w_ref[...].astype(BF16)) + b_ref[...]


def _adaln(cc, w, b):
    rows, d = cc.shape
    cols = w.shape[1]
    blk = 1536
    return pl.pallas_call(
        _adaln_kernel,
        grid=(cols // blk,),
        in_specs=[pl.BlockSpec((rows, d), lambda j: (0, 0)),
                  pl.BlockSpec((d, blk), lambda j: (0, j)),
                  pl.BlockSpec((1, blk), lambda j: (0, j))],
        out_specs=pl.BlockSpec((rows, blk), lambda j: (0, j)),
        out_shape=jax.ShapeDtypeStruct((rows, cols), F32),
        compiler_params=pltpu.CompilerParams(dimension_semantics=("parallel",),
                                             vmem_limit_bytes=VMEM_LIMIT),
        name="adaln",
    )(cc, w, b.reshape(1, cols))


def _proj_kernel(*refs, use_rope, tokens):
    if use_rope:
        (x_ref, gain_ref, sh_ref, sc_ref, wq_ref, wv_ref, wkt_ref, gb_ref,
         cos_ref, sin_ref, cost_ref, sint_ref, q_ref, v_ref, kt_ref, gr_ref, gc_ref) = refs
    else:
        (x_ref, gain_ref, sh_ref, sc_ref, wq_ref, wv_ref, wkt_ref, gb_ref,
         q_ref, v_ref, kt_ref, gr_ref, gc_ref) = refs
    L = SCAN_CHUNK
    h = _rms(x_ref[0]) * gain_ref[...]
    h = h * (1.0 + sc_ref[0]) + sh_ref[0]
    hb = h.astype(BF16)

    q = _dot(hb, wq_ref[...])
    for hd in range(HEADS):
        sl = slice(hd * HEAD_DIM, (hd + 1) * HEAD_DIM)
        t = q[:, sl]
        if use_rope:
            t = t * cos_ref[...] + pltpu.roll(t, HEAD_DIM // 2, 1) * sin_ref[...]
        q_ref[0, :, sl] = t.astype(BF16)
    q_ref[0, :, MIX_W:] = q[:, MIX_W:].astype(BF16)

    v_ref[0] = _dot(hb, wv_ref[...]).astype(BF16)

    kg = _dot_nt(wkt_ref[...], hb)
    kscale = HEAD_DIM ** -0.5
    for hd in range(HEADS):
        sl = slice(hd * HEAD_DIM, (hd + 1) * HEAD_DIM)
        t = kg[sl, :] * kscale
        if use_rope:
            half = HEAD_DIM // 2
            rot = jnp.concatenate([t[half:, :], t[:half, :]], axis=0)
            t = t * cost_ref[...] + rot * sint_ref[...]
        kt_ref[0, sl, :] = t.astype(BF16)
    kt_ref[0, MIX_W:, :] = (kg[MIX_W:2 * MIX_W, :] * kscale).astype(BF16)

    g = kg[2 * MIX_W:, :] + gb_ref[...]
    i_pre = g[0:8, :]
    log_f = _log_sigmoid(g[8:16, :])
    src = lax.broadcasted_iota(jnp.int32, (L, L), 0)
    dst = lax.broadcasted_iota(jnp.int32, (L, L), 1)
    prefix_m = (src <= dst).astype(BF16)
    suffix_m = (src >= dst).astype(BF16)
    is_fwd = lax.broadcasted_iota(jnp.int32, (8, 1), 0) < HEADS
    for ci in range(tokens // L):
        cs = slice(ci * L, (ci + 1) * L)
        lf = log_f[:, cs]
        parts = _split3(jnp.concatenate([lf, lf], axis=0))
        pre = sum(_dot(p, prefix_m) for p in parts)[0:8]
        suf = sum(_dot(p, suffix_m) for p in parts)[0:8]
        cum = jnp.where(is_fwd, pre, suf)
        stats = jnp.concatenate([i_pre[:, cs] - cum, cum], axis=0)
        gr_ref[0, :, cs] = stats
        gc_ref[0, cs, :] = stats.T


def _proj(xs, gain, shift, scale, wq, wv, wkt, gbias, rope):
    b, n, d = xs.shape
    t = min(PROJ_TOKENS, n)
    use_rope = rope is not None
    tok3 = lambda i, j: (i, j, 0)
    const2 = lambda i, j: (0, 0)
    in_specs = [pl.BlockSpec((1, t, d), tok3),
                pl.BlockSpec((1, d), const2),
                pl.BlockSpec((1, 1, d), lambda i, j: (i, 0, 0)),
                pl.BlockSpec((1, 1, d), lambda i, j: (i, 0, 0)),
                pl.BlockSpec(wq.shape, const2),
                pl.BlockSpec(wv.shape, const2),
                pl.BlockSpec(wkt.shape, const2),
                pl.BlockSpec(gbias.shape, const2)]
    args = [xs, gain, shift, scale, wq, wv, wkt, gbias]
    if use_rope:
        cos, sin, cos_t, sin_t = rope
        in_specs += [pl.BlockSpec((t, HEAD_DIM), lambda i, j: (j, 0)),
                     pl.BlockSpec((t, HEAD_DIM), lambda i, j: (j, 0)),
                     pl.BlockSpec((HEAD_DIM, t), lambda i, j: (0, j)),
                     pl.BlockSpec((HEAD_DIM, t), lambda i, j: (0, j))]
        args += [cos, sin, cos_t, sin_t]
    out_shape = (jax.ShapeDtypeStruct((b, n, 2 * MIX_W), BF16),
                 jax.ShapeDtypeStruct((b, n, 2 * MIX_W), BF16),
                 jax.ShapeDtypeStruct((b, 2 * MIX_W, n), BF16),
                 jax.ShapeDtypeStruct((b, 16, n), F32),
                 jax.ShapeDtypeStruct((b, n, 16), F32))
    out_specs = (pl.BlockSpec((1, t, 2 * MIX_W), tok3),
                 pl.BlockSpec((1, t, 2 * MIX_W), tok3),
                 pl.BlockSpec((1, 2 * MIX_W, t), lambda i, j: (i, 0, j)),
                 pl.BlockSpec((1, 16, t), lambda i, j: (i, 0, j)),
                 pl.BlockSpec((1, t, 16), tok3))
    return pl.pallas_call(
        functools.partial(_proj_kernel, use_rope=use_rope, tokens=t),
        grid=(b, n // t),
        in_specs=in_specs,
        out_specs=out_specs,
        out_shape=out_shape,
        compiler_params=pltpu.CompilerParams(dimension_semantics=("parallel", "parallel"),
                                             vmem_limit_bytes=VMEM_LIMIT),
        name="proj_rope" if use_rope else "proj_ctx",
    )(*args)


def _state_kernel(*refs, n_chunks, has_init, emit_chunks, emit_final):
    refs = list(refs)
    rc_ref = refs.pop(0)
    kt_refs = (refs.pop(0), refs.pop(0))
    v_refs = (refs.pop(0), refs.pop(0))
    gr_refs = (refs.pop(0), refs.pop(0))
    if has_init:
        s0_ref, c0_ref, m0_ref = refs.pop(0), refs.pop(0), refs.pop(0)
    if emit_chunks:
        s_out = (refs.pop(0), refs.pop(0))
        c_out = (refs.pop(0), refs.pop(0))
        m_out = (refs.pop(0), refs.pop(0))
    if emit_final:
        s_fin, c_fin, m_fin = refs.pop(0), refs.pop(0), refs.pop(0)
    s_scr, c_scr, m_scr = refs
    L = SCAN_CHUNK
    j = pl.program_id(1)

    @pl.when(j == 0)
    def _():
        if has_init:
            s_scr[...] = s0_ref[0]
            c_scr[...] = c0_ref[0]
            m_scr[...] = m0_ref[0]
        else:
            s_scr[...] = jnp.zeros_like(s_scr)
            c_scr[...] = jnp.zeros_like(c_scr)
            m_scr[...] = jnp.zeros_like(m_scr)

    pos = lax.broadcasted_iota(jnp.int32, (1, L), 1).astype(F32)
    ones = jnp.ones((L, HEAD_DIM), BF16)
    for d in range(2):
        kt_ref, v_ref, gr_ref = kt_refs[d], v_refs[d], gr_refs[d]
        a = gr_ref[0, 4 * d:4 * d + 4, :]
        cum = gr_ref[0, 8 + 4 * d:12 + 4 * d, :]
        b_last = cum[:, L - 1:L] if d == 0 else cum[:, 0:1]
        m_old = m_scr[4 * d:4 * d + 4, :]
        amax = jnp.broadcast_to(jnp.max(a, axis=1, keepdims=True), m_old.shape)
        m_mid = jnp.maximum(m_old, amax)
        w_src = jnp.exp(a - jnp.concatenate([m_mid] * (L // HEAD_DIM), axis=1))
        w_old = jnp.exp(m_old - m_mid)
        if emit_chunks:
            m_out[d][0, 0] = m_old
        m_scr[4 * d:4 * d + 4, :] = jnp.broadcast_to(b_last, m_old.shape) + m_mid
        for hd in range(HEADS):
            sl = slice(hd * HEAD_DIM, (hd + 1) * HEAD_DIM)
            ml = slice(MIX_W + hd * HEAD_DIM, MIX_W + (hd + 1) * HEAD_DIM)
            lg = rc_ref[4 * d + hd]
            to_end = jnp.exp(lg * ((L - 1.0) - pos)) if d == 0 else jnp.exp(lg * pos)
            s_prev = s_scr[d, hd]
            if emit_chunks:
                s_out[d][0, 0, hd] = s_prev.astype(BF16)
            kw = (kt_ref[0, sl, :].astype(F32) * to_end).astype(BF16)
            s_scr[d, hd] = rc_ref[8 + 4 * d + hd] * s_prev + _dot(kw, v_ref[0, :, sl])
            c_prev = c_scr[d, hd]
            if emit_chunks:
                c_out[d][0, 0, hd] = c_prev.astype(BF16)
            kw = (kt_ref[0, ml, :].astype(F32) * w_src[hd:hd + 1, :]).astype(BF16)
            v_ext = jnp.concatenate([v_ref[0, :, ml], ones], axis=1)
            wo = w_old[hd:hd + 1, :]
            c_scr[d, hd] = jnp.concatenate([wo, wo], axis=1) * c_prev + _dot(kw, v_ext)

    if emit_final:
        @pl.when(j == n_chunks - 1)
        def _():
            s_fin[0] = s_scr[...]
            c_fin[0] = c_scr[...]
            m_fin[0] = m_scr[...]


def _state_scan(ret_consts, kt, v, gr, init, emit_chunks, emit_final):
    b, n, _ = v.shape
    L = SCAN_CHUNK
    nc = n // L
    fwd_tok = lambda i, j: (i, j, 0)
    bwd_tok = lambda i, j: (i, nc - 1 - j, 0)
    fwd_feat = lambda i, j: (i, 0, j)
    bwd_feat = lambda i, j: (i, 0, nc - 1 - j)
    in_specs = [pl.BlockSpec(memory_space=pltpu.SMEM),
                pl.BlockSpec((1, 2 * MIX_W, L), fwd_feat), pl.BlockSpec((1, 2 * MIX_W, L), bwd_feat),
                pl.BlockSpec((1, L, 2 * MIX_W), fwd_tok), pl.BlockSpec((1, L, 2 * MIX_W), bwd_tok),
                pl.BlockSpec((1, 16, L), fwd_feat), pl.BlockSpec((1, 16, L), bwd_feat)]
    args = [ret_consts, kt, kt, v, v, gr, gr]
    s_shape = (2, HEADS, HEAD_DIM, HEAD_DIM)
    c_shape = (2, HEADS, HEAD_DIM, 2 * HEAD_DIM)
    m_shape = (8, HEAD_DIM)
    per_b = lambda shape: pl.BlockSpec((1,) + shape, lambda i, j: (i,) + (0,) * len(shape))
    if init is not None:
        in_specs += [per_b(s_shape), per_b(c_shape), per_b(m_shape)]
        args += list(init)
    out_shape, out_specs = [], []
    if emit_chunks:
        fwd_c = lambda i, j: (i, j, 0, 0, 0)
        bwd_c = lambda i, j: (i, nc - 1 - j, 0, 0, 0)
        for shape in (s_shape[1:], c_shape[1:]):
            for imap in (fwd_c, bwd_c):
                out_shape.append(jax.ShapeDtypeStruct((b, nc) + shape, BF16))
                out_specs.append(pl.BlockSpec((1, 1) + shape, imap))
        for imap in (lambda i, j: (i, j, 0, 0), lambda i, j: (i, nc - 1 - j, 0, 0)):
            out_shape.append(jax.ShapeDtypeStruct((b, nc, HEADS, HEAD_DIM), F32))
            out_specs.append(pl.BlockSpec((1, 1, HEADS, HEAD_DIM), imap))
    if emit_final:
        for shape in (s_shape, c_shape, m_shape):
            out_shape.append(jax.ShapeDtypeStruct((b,) + shape, F32))
            out_specs.append(per_b(shape))
    return pl.pallas_call(
        functools.partial(_state_kernel, n_chunks=nc, has_init=init is not None,
                          emit_chunks=emit_chunks, emit_final=emit_final),
        grid=(b, nc),
        in_specs=in_specs,
        out_specs=tuple(out_specs),
        out_shape=tuple(out_shape),
        scratch_shapes=[pltpu.VMEM(s_shape, F32), pltpu.VMEM(c_shape, F32), pltpu.VMEM(m_shape, F32)],
        compiler_params=pltpu.CompilerParams(dimension_semantics=("arbitrary", "arbitrary"),
                                             vmem_limit_bytes=VMEM_LIMIT),
        name="state_chunks" if emit_chunks else "state_ctx",
    )(*args)


def _mixer_kernel(rc_ref, x_ref, gain_ref, sh_ref, sc_ref, g1_ref, q_ref, kt_ref, v_ref, gr_ref, gc_ref,
                  sf_ref, sb_ref, cf_ref, cb_ref, mf_ref, mb_ref,
                  wpost_ref, wru_ref, wmu_ref, wout_ref, o_ref, dec_scr, qdec_scr):
    L = SCAN_CHUNK
    row = lax.broadcasted_iota(jnp.int32, (L, L), 0)
    col = lax.broadcasted_iota(jnp.int32, (L, L), 1)

    @pl.when((pl.program_id(0) == 0) & (pl.program_id(1) == 0))
    def _():
        rel = (row - col).astype(F32)
        lpos = lax.broadcasted_iota(jnp.int32, (L, HEAD_DIM), 0).astype(F32)
        for hd in range(HEADS):
            lg_f = rc_ref[hd]
            lg_b = rc_ref[4 + hd]
            dec_scr[hd] = (jnp.where(rel >= 0, jnp.exp(lg_f * jnp.maximum(rel, 0.0)), 0.0)
                           + jnp.where(rel <= 0, jnp.exp(lg_b * jnp.maximum(-rel, 0.0)), 0.0))
            qdec_scr[hd] = jnp.concatenate([jnp.exp(lg_f * (lpos + 1.0)), jnp.exp(lg_b * (L - lpos))], axis=1)

    xf = x_ref[0]
    h = _rms(xf) * gain_ref[...]
    hb = (h * (1.0 + sc_ref[0]) + sh_ref[0]).astype(BF16)
    post = _dot(hb, wpost_ref[...])

    masks = (col <= row, col >= row)
    s_refs, c_refs, m_refs = (sf_ref, sb_ref), (cf_ref, cb_ref), (mf_ref, mb_ref)
    ones = jnp.ones((L, HEAD_DIM), BF16)
    neg_inf = jnp.float32(-jnp.inf)
    y_ret, y_ml = [], []
    for hd in range(HEADS):
        sl = slice(hd * HEAD_DIM, (hd + 1) * HEAD_DIM)
        ml = slice(MIX_W + hd * HEAD_DIM, MIX_W + (hd + 1) * HEAD_DIM)
        qh = q_ref[0, :, sl]
        p = (_dot(qh, kt_ref[0, sl, :]) * dec_scr[hd]).astype(BF16)
        qf = qh.astype(F32)
        qw = (jnp.concatenate([qf, qf], axis=1) * qdec_scr[hd]).astype(BF16)
        st = jnp.concatenate([sf_ref[0, 0, hd], sb_ref[0, 0, hd]], axis=0)
        ret = _dot(p, v_ref[0, :, sl]) + _dot(qw, st)
        y_ret.append(_rms(ret) * _silu(post[:, sl]))
        qh = q_ref[0, :, ml]
        scores = _dot(qh, kt_ref[0, ml, :])
        v_ext = jnp.concatenate([v_ref[0, :, ml], ones], axis=1)
        hsum = None
        for d in range(2):
            a_row = gr_ref[0, 4 * d + hd:4 * d + hd + 1, :]
            cum_col = gc_ref[0, :, 8 + 4 * d + hd:9 + 4 * d + hd]
            m_in = m_refs[d][0, 0, hd:hd + 1, :]
            a_masked = jnp.where(masks[d], a_row, neg_inf)
            cmax = jnp.broadcast_to(jnp.max(a_masked, axis=1, keepdims=True), (L, HEAD_DIM))
            m_loc = jnp.maximum(cmax, m_in)
            w = jnp.exp(a_masked - jnp.concatenate([m_loc] * (L // HEAD_DIM), axis=1))
            tot = _dot((scores * w).astype(BF16), v_ext)
            w_inter = jnp.exp(m_in - m_loc)
            tot = tot + jnp.concatenate([w_inter, w_inter], axis=1) * _dot(qh, c_refs[d][0, 0, hd])
            floor = jnp.exp(-(jnp.broadcast_to(cum_col, (L, HEAD_DIM)) + m_loc))
            hd_out = tot[:, :HEAD_DIM] / jnp.maximum(jnp.abs(tot[:, HEAD_DIM:]), floor)
            hsum = hd_out if hsum is None else hsum + hd_out
        y_ml.append(_rms(_sigmoid(post[:, ml]) * hsum))

    y_ret = jnp.concatenate(y_ret, axis=1).astype(BF16)
    y_ml = jnp.concatenate(y_ml, axis=1).astype(BF16)
    d_model = xf.shape[-1]
    bg_ret = post[:, 2 * MIX_W:2 * MIX_W + d_model]
    bg_ml = post[:, 2 * MIX_W + d_model:]
    merged = _sigmoid(bg_ret) * _dot(y_ret, wru_ref[...]) + _sigmoid(bg_ml) * _dot(y_ml, wmu_ref[...])
    o_ref[0] = xf + g1_ref[0] * _dot(merged.astype(BF16), wout_ref[...])


def _mixer(ret_consts, x, gain, shift, scale, gate1, q, kt, v, gr, gc, chunk_states, wpost, wru, wmu, wout):
    b, n, d = x.shape
    L = SCAN_CHUNK
    nc = n // L
    sf, sb, cf, cb, mf, mb = chunk_states
    tok3 = lambda i, j: (i, j, 0)
    feat3 = lambda i, j: (i, 0, j)
    mod3 = lambda i, j: (i, 0, 0)
    const2 = lambda i, j: (0, 0)
    st5 = lambda i, j: (i, j, 0, 0, 0)
    in_specs = [pl.BlockSpec(memory_space=pltpu.SMEM),
                pl.BlockSpec((1, L, d), tok3),
                pl.BlockSpec((1, d), const2),
                pl.BlockSpec((1, 1, d), mod3), pl.BlockSpec((1, 1, d), mod3), pl.BlockSpec((1, 1, d), mod3),
                pl.BlockSpec((1, L, 2 * MIX_W), tok3),
                pl.BlockSpec((1, 2 * MIX_W, L), feat3),
                pl.BlockSpec((1, L, 2 * MIX_W), tok3),
                pl.BlockSpec((1, 16, L), feat3),
                pl.BlockSpec((1, L, 16), tok3),
                pl.BlockSpec((1, 1, HEADS, HEAD_DIM, HEAD_DIM), st5),
                pl.BlockSpec((1, 1, HEADS, HEAD_DIM, HEAD_DIM), st5),
                pl.BlockSpec((1, 1, HEADS, HEAD_DIM, 2 * HEAD_DIM), st5),
                pl.BlockSpec((1, 1, HEADS, HEAD_DIM, 2 * HEAD_DIM), st5),
                pl.BlockSpec((1, 1, HEADS, HEAD_DIM), lambda i, j: (i, j, 0, 0)),
                pl.BlockSpec((1, 1, HEADS, HEAD_DIM), lambda i, j: (i, j, 0, 0)),
                pl.BlockSpec(wpost.shape, const2),
                pl.BlockSpec(wru.shape, const2),
                pl.BlockSpec(wmu.shape, const2),
                pl.BlockSpec(wout.shape, const2)]
    return pl.pallas_call(
        _mixer_kernel,
        grid=(b, nc),
        in_specs=in_specs,
        out_specs=pl.BlockSpec((1, L, d), tok3),
        out_shape=jax.ShapeDtypeStruct((b, n, d), F32),
        scratch_shapes=[pltpu.VMEM((HEADS, L, L), F32), pltpu.VMEM((HEADS, L, 2 * HEAD_DIM), F32)],
        compiler_params=pltpu.CompilerParams(dimension_semantics=("arbitrary", "arbitrary"),
                                             vmem_limit_bytes=VMEM_LIMIT),
        name="mixer",
    )(ret_consts, x, gain, shift, scale, gate1, q, kt, v, gr, gc, sf, sb, cf, cb, mf, mb,
      wpost, wru, wmu, wout)


def _ffn_kernel(x_ref, gain_ref, sh_ref, sc_ref, g2_ref, fgain_ref, w1_ref, w2_ref, o_ref, *, hidden):
    xf = x_ref[0]
    h = _rms(xf) * gain_ref[...]
    hb = (h * (1.0 + sc_ref[0]) + sh_ref[0]).astype(BF16)
    acc = None
    start = 0
    for width in FFN_SPLITS:
        gate = _dot(hb, w1_ref[:, start:start + width])
        up = _dot(hb, w1_ref[:, hidden + start:hidden + start + width])
        act = (_silu(gate) * up).astype(BF16)
        part = _dot(act, w2_ref[start:start + width, :])
        acc = part if acc is None else acc + part
        start += width
    o_ref[0] = _rms(xf + g2_ref[0] * acc) * fgain_ref[...]


def _ffn(x, gain, shift, scale, gate2, final_gain, w1, w2):
    b, n, d = x.shape
    t = FFN_TOKENS
    hidden = w2.shape[0]
    assert sum(FFN_SPLITS) == hidden
    tok3 = lambda i, j: (i, j, 0)
    mod3 = lambda i, j: (i, 0, 0)
    const2 = lambda i, j: (0, 0)
    return pl.pallas_call(
        functools.partial(_ffn_kernel, hidden=hidden),
        grid=(b, n // t),
        in_specs=[pl.BlockSpec((1, t, d), tok3),
                  pl.BlockSpec((1, d), const2),
                  pl.BlockSpec((1, 1, d), mod3), pl.BlockSpec((1, 1, d), mod3), pl.BlockSpec((1, 1, d), mod3),
                  pl.BlockSpec((1, d), const2),
                  pl.BlockSpec(w1.shape, const2),
                  pl.BlockSpec(w2.shape, const2)],
        out_specs=pl.BlockSpec((1, t, d), tok3),
        out_shape=jax.ShapeDtypeStruct((b, n, d), F32),
        compiler_params=pltpu.CompilerParams(dimension_semantics=("parallel", "parallel"),
                                             vmem_limit_bytes=VMEM_LIMIT),
        name="ffn",
    )(x, gain, shift, scale, gate2, final_gain, w1, w2)


def _rope_tables(n):
    n_rows = n // GRID_W
    rows = jnp.broadcast_to(jnp.arange(n_rows, dtype=F32)[:, None], (n_rows, GRID_W)).reshape(n)
    cols = jnp.broadcast_to(jnp.arange(GRID_W, dtype=F32)[None, :], (n_rows, GRID_W)).reshape(n)
    n_freq = HEAD_DIM // 4
    inv = ROPE_BASE ** (-jnp.arange(n_freq, dtype=F32) / n_freq)
    ang = jnp.concatenate([rows[:, None] * inv, cols[:, None] * inv], axis=-1)
    cos, sin = jnp.cos(ang), jnp.sin(ang)
    cos2 = jnp.concatenate([cos, cos], axis=-1)
    sin2 = jnp.concatenate([-sin, sin], axis=-1)
    return cos2, sin2, cos2.T, sin2.T


def kernel(x, c, ctx, c_ctx, w_ada, b_ada, norm1_gain, norm2_gain, w_in, mlstm_gate_bias, ret_decay_logit,
           w_ret_up, w_ml_up, w_out, w_ffn_in, w_ffn_out, final_gain):
    assert w_ada.shape[0] == 1, "single-layer block"
    b, n, d = x.shape
    assert n % PROJ_TOKENS == 0 and n % FFN_TOKENS == 0 and ctx.shape[1] % SCAN_CHUNK == 0

    rows = -(-(b + 1) // 16) * 16
    cc = jnp.concatenate([c, c_ctx[None, :], jnp.zeros((rows - b - 1, d), F32)], axis=0)
    mod = _adaln(cc, w_ada[0], b_ada[0])
    sh1, sc1, g1, sh2, sc2, g2 = (mod[:b, i * d:(i + 1) * d].reshape(b, 1, d) for i in range(6))
    csh1 = jnp.broadcast_to(mod[b, 0:d].reshape(1, 1, d), (b, 1, d))
    csc1 = jnp.broadcast_to(mod[b, d:2 * d].reshape(1, 1, d), (b, 1, d))

    w = w_in[0]
    o = [0]
    for width in (MIX_W, MIX_W, MIX_W, MIX_W, MIX_W, MIX_W, MIX_W, MIX_W, 4 * HEADS, d, d):
        o.append(o[-1] + width)
    col = lambda i: w[:, o[i]:o[i + 1]]
    gates = col(8).reshape(d, 4, HEADS)
    gates = jnp.concatenate([gates[:, 0], gates[:, 2], gates[:, 1], gates[:, 3]], axis=1)
    gb = mlstm_gate_bias[0]
    gbias = jnp.concatenate([gb[0], gb[2], gb[1], gb[3]]).reshape(4 * HEADS, 1).astype(F32)
    wq = jnp.concatenate([col(0), col(4)], axis=1).astype(BF16)
    wv = jnp.concatenate([col(2), col(6)], axis=1).astype(BF16)
    wkt = jnp.concatenate([col(1), col(5), gates], axis=1).T.astype(BF16)
    wpost = jnp.concatenate([col(3), col(7), col(9), col(10)], axis=1).astype(BF16)

    log_gamma = jax.nn.log_sigmoid(ret_decay_logit[0].astype(F32)).reshape(2 * HEADS)
    ret_consts = jnp.concatenate([log_gamma, jnp.exp(log_gamma * SCAN_CHUNK)])

    gain1 = norm1_gain[0].reshape(1, d)
    _, v_c, kt_c, gr_c, _ = _proj(ctx, gain1, csh1, csc1, wq, wv, wkt, gbias, None)
    ctx_final = _state_scan(ret_consts, kt_c, v_c, gr_c, None, emit_chunks=False, emit_final=True)

    q_x, v_x, kt_x, gr_x, gc_x = _proj(x, gain1, sh1, sc1, wq, wv, wkt, gbias, _rope_tables(n))
    chunk_states = _state_scan(ret_consts, kt_x, v_x, gr_x, ctx_final, emit_chunks=True, emit_final=False)
    x1 = _mixer(ret_consts, x, gain1, sh1, sc1, g1, q_x, kt_x, v_x, gr_x, gc_x, chunk_states,
                wpost, w_ret_up[0].astype(BF16), w_ml_up[0].astype(BF16), w_out[0].astype(BF16))
    return _ffn(x1, norm2_gain[0].reshape(1, d), sh2, sc2, g2, final_gain.reshape(1, d),
                w_ffn_in[0].astype(BF16), w_ffn_out[0].astype(BF16))
```

```python
import functools

import jax
import jax.numpy as jnp
from jax import lax
from jax.experimental import pallas as pl
from jax.experimental.pallas import tpu as pltpu

HEADS = 4
HEAD_DIM = 128
MIX_W = HEADS * HEAD_DIM
GRID_W = 64
ROPE_BASE = 10000.0
EPS = 1e-6
SCAN_CHUNK = 256
PROJ_TOKENS = 512
MIX_TOKENS = 512
FFN_TOKENS = 512
FFN_SPLITS = (1536, 1280)
VMEM_LIMIT = 56 * 1024 * 1024

F32 = jnp.float32
BF16 = jnp.bfloat16


def _dot(a, b):
    return jnp.dot(a, b, preferred_element_type=F32)


def _dot_nt(a, b):
    return lax.dot_general(a, b, (((1,), (1,)), ((), ())), preferred_element_type=F32)


def _sigmoid(t):
    return 0.5 * jnp.tanh(0.5 * t) + 0.5


def _silu(t):
    return t * _sigmoid(t)


def _log_sigmoid(t):
    return jnp.minimum(t, 0.0) - jnp.log1p(jnp.exp(-jnp.abs(t)))


def _rms(t):
    return t * lax.rsqrt(jnp.mean(t * t, axis=-1, keepdims=True) + EPS)


def _split3(t):
    hi = t.astype(BF16)
    r1 = t - hi.astype(F32)
    mid = r1.astype(BF16)
    lo = (r1 - mid.astype(F32)).astype(BF16)
    return hi, mid, lo


def _adaln_kernel(c_ref, w_ref, b_ref, o_ref):
    s = _silu(c_ref[...]).astype(BF16)
    o_ref[...] = _dot(s, w_ref[...].astype(BF16)) + b_ref[...]


def _adaln(cc, w, b):
    rows, d = cc.shape
    cols = w.shape[1]
    blk = 1536
    return pl.pallas_call(
        _adaln_kernel,
        grid=(cols // blk,),
        in_specs=[pl.BlockSpec((rows, d), lambda j: (0, 0)),
                  pl.BlockSpec((d, blk), lambda j: (0, j)),
                  pl.BlockSpec((1, blk), lambda j: (0, j))],
        out_specs=pl.BlockSpec((rows, blk), lambda j: (0, j)),
        out_shape=jax.ShapeDtypeStruct((rows, cols), F32),
        compiler_params=pltpu.CompilerParams(dimension_semantics=("parallel",),
                                             vmem_limit_bytes=VMEM_LIMIT),
        name="adaln",
    )(cc, w, b.reshape(1, cols))


def _proj_kernel(*refs, use_rope, tokens):
    if use_rope:
        (x_ref, gain_ref, sh_ref, sc_ref, wq_ref, wv_ref, wkt_ref, gb_ref,
         cos_ref, sin_ref, cost_ref, sint_ref, q_ref, v_ref, kt_ref, gr_ref, gc_ref) = refs
    else:
        (x_ref, gain_ref, sh_ref, sc_ref, wq_ref, wv_ref, wkt_ref, gb_ref,
         q_ref, v_ref, kt_ref, gr_ref, gc_ref) = refs
    L = SCAN_CHUNK
    h = _rms(x_ref[0]) * gain_ref[...]
    h = h * (1.0 + sc_ref[0]) + sh_ref[0]
    hb = h.astype(BF16)

    q = _dot(hb, wq_ref[...])
    for hd in range(HEADS):
        sl = slice(hd * HEAD_DIM, (hd + 1) * HEAD_DIM)
        t = q[:, sl]
        if use_rope:
            t = t * cos_ref[...] + pltpu.roll(t, HEAD_DIM // 2, 1) * sin_ref[...]
        q_ref[0, :, sl] = t.astype(BF16)
    q_ref[0, :, MIX_W:] = q[:, MIX_W:].astype(BF16)

    v_ref[0] = _dot(hb, wv_ref[...]).astype(BF16)

    kg = _dot_nt(wkt_ref[...], hb)
    kscale = HEAD_DIM ** -0.5
    for hd in range(HEADS):
        sl = slice(hd * HEAD_DIM, (hd + 1) * HEAD_DIM)
        t = kg[sl, :] * kscale
        if use_rope:
            half = HEAD_DIM // 2
            rot = jnp.concatenate([t[half:, :], t[:half, :]], axis=0)
            t = t * cost_ref[...] + rot * sint_ref[...]
        kt_ref[0, sl, :] = t.astype(BF16)
    kt_ref[0, MIX_W:, :] = (kg[MIX_W:2 * MIX_W, :] * kscale).astype(BF16)

    g = kg[2 * MIX_W:, :] + gb_ref[...]
    i_pre = g[0:8, :]
    log_f = _log_sigmoid(g[8:16, :])
    src = lax.broadcasted_iota(jnp.int32, (L, L), 0)
    dst = lax.broadcasted_iota(jnp.int32, (L, L), 1)
    prefix_m = (src <= dst).astype(BF16)
    suffix_m = (src >= dst).astype(BF16)
    is_fwd = lax.broadcasted_iota(jnp.int32, (8, 1), 0) < HEADS
    for ci in range(tokens // L):
        cs = slice(ci * L, (ci + 1) * L)
        lf = log_f[:, cs]
        parts = _split3(jnp.concatenate([lf, lf], axis=0))
        pre = sum(_dot(p, prefix_m) for p in parts)[0:8]
        suf = sum(_dot(p, suffix_m) for p in parts)[0:8]
        cum = jnp.where(is_fwd, pre, suf)
        stats = jnp.concatenate([i_pre[:, cs] - cum, cum], axis=0)
        gr_ref[0, :, cs] = stats
        gc_ref[0, cs, :] = stats.T


def _proj(xs, gain, shift, scale, wq, wv, wkt, gbias, rope):
    b, n, d = xs.shape
    t = min(PROJ_TOKENS, n)
    use_rope = rope is not None
    tok3 = lambda i, j: (i, j, 0)
    const2 = lambda i, j: (0, 0)
    in_specs = [pl.BlockSpec((1, t, d), tok3),
                pl.BlockSpec((1, d), const2),
                pl.BlockSpec((1, 1, d), lambda i, j: (i, 0, 0)),
                pl.BlockSpec((1, 1, d), lambda i, j: (i, 0, 0)),
                pl.BlockSpec(wq.shape, const2),
                pl.BlockSpec(wv.shape, const2),
                pl.BlockSpec(wkt.shape, const2),
                pl.BlockSpec(gbias.shape, const2)]
    args = [xs, gain, shift, scale, wq, wv, wkt, gbias]
    if use_rope:
        cos, sin, cos_t, sin_t = rope
        in_specs += [pl.BlockSpec((t, HEAD_DIM), lambda i, j: (j, 0)),
                     pl.BlockSpec((t, HEAD_DIM), lambda i, j: (j, 0)),
                     pl.BlockSpec((HEAD_DIM, t), lambda i, j: (0, j)),
                     pl.BlockSpec((HEAD_DIM, t), lambda i, j: (0, j))]
        args += [cos, sin, cos_t, sin_t]
    out_shape = (jax.ShapeDtypeStruct((b, n, 2 * MIX_W), BF16),
                 jax.ShapeDtypeStruct((b, n, 2 * MIX_W), BF16),
                 jax.ShapeDtypeStruct((b, 2 * MIX_W, n), BF16),
                 jax.ShapeDtypeStruct((b, 16, n), F32),
                 jax.ShapeDtypeStruct((b, n, 16), F32))
    out_specs = (pl.BlockSpec((1, t, 2 * MIX_W), tok3),
                 pl.BlockSpec((1, t, 2 * MIX_W), tok3),
                 pl.BlockSpec((1, 2 * MIX_W, t), lambda i, j: (i, 0, j)),
                 pl.BlockSpec((1, 16, t), lambda i, j: (i, 0, j)),
                 pl.BlockSpec((1, t, 16), tok3))
    return pl.pallas_call(
        functools.partial(_proj_kernel, use_rope=use_rope, tokens=t),
        grid=(b, n // t),
        in_specs=in_specs,
        out_specs=out_specs,
        out_shape=out_shape,
        compiler_params=pltpu.CompilerParams(dimension_semantics=("parallel", "parallel"),
                                             vmem_limit_bytes=VMEM_LIMIT),
        name="proj_rope" if use_rope else "proj_ctx",
    )(*args)


def _state_kernel(*refs, n_chunks, has_init, emit_chunks, emit_final):
    refs = list(refs)
    rc_ref = refs.pop(0)
    kt_refs = (refs.pop(0), refs.pop(0))
    v_refs = (refs.pop(0), refs.pop(0))
    gr_refs = (refs.pop(0), refs.pop(0))
    if has_init:
        s0_ref, c0_ref, m0_ref = refs.pop(0), refs.pop(0), refs.pop(0)
    if emit_chunks:
        s_out = (refs.pop(0), refs.pop(0))
        c_out = (refs.pop(0), refs.pop(0))
        m_out = (refs.pop(0), refs.pop(0))
    if emit_final:
        s_fin, c_fin, m_fin = refs.pop(0), refs.pop(0), refs.pop(0)
    s_scr, c_scr, m_scr = refs
    L = SCAN_CHUNK
    j = pl.program_id(1)

    @pl.when(j == 0)
    def _():
        if has_init:
            s_scr[...] = s0_ref[0]
            c_scr[...] = c0_ref[0]
            m_scr[...] = m0_ref[0]
        else:
            s_scr[...] = jnp.zeros_like(s_scr)
            c_scr[...] = jnp.zeros_like(c_scr)
            m_scr[...] = jnp.zeros_like(m_scr)

    pos = lax.broadcasted_iota(jnp.int32, (1, L), 1).astype(F32)
    ones = jnp.ones((L, HEAD_DIM), BF16)
    for d in range(2):
        kt_ref, v_ref, gr_ref = kt_refs[d], v_refs[d], gr_refs[d]
        a = gr_ref[0, 4 * d:4 * d + 4, :]
        cum = gr_ref[0, 8 + 4 * d:12 + 4 * d, :]
        b_last = cum[:, L - 1:L] if d == 0 else cum[:, 0:1]
        m_old = m_scr[4 * d:4 * d + 4, :]
        amax = jnp.broadcast_to(jnp.max(a, axis=1, keepdims=True), m_old.shape)
        m_mid = jnp.maximum(m_old, amax)
        w_src = jnp.exp(a - jnp.concatenate([m_mid] * (L // HEAD_DIM), axis=1))
        w_old = jnp.exp(m_old - m_mid)
        if emit_chunks:
            m_out[d][0, 0] = m_old
        m_scr[4 * d:4 * d + 4, :] = jnp.broadcast_to(b_last, m_old.shape) + m_mid
        for hd in range(HEADS):
            sl = slice(hd * HEAD_DIM, (hd + 1) * HEAD_DIM)
            ml = slice(MIX_W + hd * HEAD_DIM, MIX_W + (hd + 1) * HEAD_DIM)
            lg = rc_ref[4 * d + hd]
            to_end = jnp.exp(lg * ((L - 1.0) - pos)) if d == 0 else jnp.exp(lg * pos)
            s_prev = s_scr[d, hd]
            if emit_chunks:
                s_out[d][0, 0, hd] = s_prev.astype(BF16)
            kw = (kt_ref[0, sl, :].astype(F32) * to_end).astype(BF16)
            s_scr[d, hd] = rc_ref[8 + 4 * d + hd] * s_prev + _dot(kw, v_ref[0, :, sl])
            c_prev = c_scr[d, hd]
            if emit_chunks:
                c_out[d][0, 0, hd] = c_prev.astype(BF16)
            kw = (kt_ref[0, ml, :].astype(F32) * w_src[hd:hd + 1, :]).astype(BF16)
            v_ext = jnp.concatenate([v_ref[0, :, ml], ones], axis=1)
            wo = w_old[hd:hd + 1, :]
            c_scr[d, hd] = jnp.concatenate([wo, wo], axis=1) * c_prev + _dot(kw, v_ext)

    if emit_final:
        @pl.when(j == n_chunks - 1)
        def _():
            s_fin[0] = s_scr[...]
            c_fin[0] = c_scr[...]
            m_fin[0] = m_scr[...]


def _state_scan(ret_consts, kt, v, gr, init, emit_chunks, emit_final):
    b, n, _ = v.shape
    L = SCAN_CHUNK
    nc = n // L
    fwd_tok = lambda i, j: (i, j, 0)
    bwd_tok = lambda i, j: (i, nc - 1 - j, 0)
    fwd_feat = lambda i, j: (i, 0, j)
    bwd_feat = lambda i, j: (i, 0, nc - 1 - j)
    in_specs = [pl.BlockSpec(memory_space=pltpu.SMEM),
                pl.BlockSpec((1, 2 * MIX_W, L), fwd_feat), pl.BlockSpec((1, 2 * MIX_W, L), bwd_feat),
                pl.BlockSpec((1, L, 2 * MIX_W), fwd_tok), pl.BlockSpec((1, L, 2 * MIX_W), bwd_tok),
                pl.BlockSpec((1, 16, L), fwd_feat), pl.BlockSpec((1, 16, L), bwd_feat)]
    args = [ret_consts, kt, kt, v, v, gr, gr]
    s_shape = (2, HEADS, HEAD_DIM, HEAD_DIM)
    c_shape = (2, HEADS, HEAD_DIM, 2 * HEAD_DIM)
    m_shape = (8, HEAD_DIM)
    per_b = lambda shape: pl.BlockSpec((1,) + shape, lambda i, j: (i,) + (0,) * len(shape))
    if init is not None:
        in_specs += [per_b(s_shape), per_b(c_shape), per_b(m_shape)]
        args += list(init)
    out_shape, out_specs = [], []
    if emit_chunks:
        fwd_c = lambda i, j: (i, j, 0, 0, 0)
        bwd_c = lambda i, j: (i, nc - 1 - j, 0, 0, 0)
        for shape in (s_shape[1:], c_shape[1:]):
            for imap in (fwd_c, bwd_c):
                out_shape.append(jax.ShapeDtypeStruct((b, nc) + shape, BF16))
                out_specs.append(pl.BlockSpec((1, 1) + shape, imap))
        for imap in (lambda i, j: (i, j, 0, 0), lambda i, j: (i, nc - 1 - j, 0, 0)):
            out_shape.append(jax.ShapeDtypeStruct((b, nc, HEADS, HEAD_DIM), F32))
            out_specs.append(pl.BlockSpec((1, 1, HEADS, HEAD_DIM), imap))
    if emit_final:
        for shape in (s_shape, c_shape, m_shape):
            out_shape.append(jax.ShapeDtypeStruct((b,) + shape, F32))
            out_specs.append(per_b(shape))
    return pl.pallas_call(
        functools.partial(_state_kernel, n_chunks=nc, has_init=init is not None,
                          emit_chunks=emit_chunks, emit_final=emit_final),
        grid=(b, nc),
        in_specs=in_specs,
        out_specs=tuple(out_specs),
        out_shape=tuple(out_shape),
        scratch_shapes=[pltpu.VMEM(s_shape, F32), pltpu.VMEM(c_shape, F32), pltpu.VMEM(m_shape, F32)],
        compiler_params=pltpu.CompilerParams(dimension_semantics=("arbitrary", "arbitrary"),
                                             vmem_limit_bytes=VMEM_LIMIT),
        name="state_chunks" if emit_chunks else "state_ctx",
    )(*args)


def _mixer_kernel(rc_ref, x_ref, gain_ref, sh_ref, sc_ref, g1_ref, q_ref, kt_ref, v_ref, gr_ref, gc_ref,
                  sf_ref, sb_ref, cf_ref, cb_ref, mf_ref, mb_ref,
                  wpost_ref, wru_ref, wmu_ref, wout_ref, o_ref, dec_scr, qdec_scr):
    L = SCAN_CHUNK
    row = lax.broadcasted_iota(jnp.int32, (L, L), 0)
    col = lax.broadcasted_iota(jnp.int32, (L, L), 1)

    @pl.when((pl.program_id(0) == 0) & (pl.program_id(1) == 0))
    def _():
        rel = (row - col).astype(F32)
        lpos = lax.broadcasted_iota(jnp.int32, (L, HEAD_DIM), 0).astype(F32)
        for hd in range(HEADS):
            lg_f = rc_ref[hd]
            lg_b = rc_ref[4 + hd]
            dec_scr[hd] = (jnp.where(rel >= 0, jnp.exp(lg_f * jnp.maximum(rel, 0.0)), 0.0)
                           + jnp.where(rel <= 0, jnp.exp(lg_b * jnp.maximum(-rel, 0.0)), 0.0))
            qdec_scr[hd] = jnp.concatenate([jnp.exp(lg_f * (lpos + 1.0)), jnp.exp(lg_b * (L - lpos))], axis=1)

    xf = x_ref[0]
    h = _rms(xf) * gain_ref[...]
    hb = (h * (1.0 + sc_ref[0]) + sh_ref[0]).astype(BF16)
    d_model = xf.shape[-1]
    bg_w = d_model // HEADS
    post_w = 2 * HEAD_DIM + 2 * bg_w

    masks = (col <= row, col >= row)
    s_refs, c_refs, m_refs = (sf_ref, sb_ref), (cf_ref, cb_ref), (mf_ref, mb_ref)
    ones = jnp.ones((L, HEAD_DIM), BF16)
    neg_inf = jnp.float32(-jnp.inf)
    y_ret, y_ml, bg_ret, bg_ml = [], [], [], []
    for hd in range(HEADS):
        sl = slice(hd * HEAD_DIM, (hd + 1) * HEAD_DIM)
        ml = slice(MIX_W + hd * HEAD_DIM, MIX_W + (hd + 1) * HEAD_DIM)
        post = _dot(hb, wpost_ref[:, hd * post_w:(hd + 1) * post_w])
        bg_ret.append(post[:, 2 * HEAD_DIM:2 * HEAD_DIM + bg_w])
        bg_ml.append(post[:, 2 * HEAD_DIM + bg_w:])
        ret_rows, ml_rows = [], []
        for ci in range(xf.shape[0] // L):
            tok = slice(ci * L, (ci + 1) * L)
            qh = q_ref[0, tok, sl]
            p = (_dot(qh, kt_ref[0, sl, tok]) * dec_scr[hd]).astype(BF16)
            qf = qh.astype(F32)
            qw = (jnp.concatenate([qf, qf], axis=1) * qdec_scr[hd]).astype(BF16)
            st = jnp.concatenate([sf_ref[0, ci, hd], sb_ref[0, ci, hd]], axis=0)
            ret_rows.append(_rms(_dot(p, v_ref[0, tok, sl]) + _dot(qw, st)))
            qh = q_ref[0, tok, ml]
            scores = _dot(qh, kt_ref[0, ml, tok])
            v_ext = jnp.concatenate([v_ref[0, tok, ml], ones], axis=1)
            hsum = None
            for d in range(2):
                a_row = gr_ref[0, 4 * d + hd:4 * d + hd + 1, tok]
                cum_col = gc_ref[0, tok, 8 + 4 * d + hd:9 + 4 * d + hd]
                m_in = m_refs[d][0, ci, hd:hd + 1, :]
                a_masked = jnp.where(masks[d], a_row, neg_inf)
                cmax = jnp.broadcast_to(jnp.max(a_masked, axis=1, keepdims=True), (L, HEAD_DIM))
                m_loc = jnp.maximum(cmax, m_in)
                w = jnp.exp(a_masked - jnp.concatenate([m_loc] * (L // HEAD_DIM), axis=1))
                tot = _dot((scores * w).astype(BF16), v_ext)
                w_inter = jnp.exp(m_in - m_loc)
                tot = tot + jnp.concatenate([w_inter, w_inter], axis=1) * _dot(qh, c_refs[d][0, ci, hd])
                floor = jnp.exp(-(jnp.broadcast_to(cum_col, (L, HEAD_DIM)) + m_loc))
                hd_out = tot[:, :HEAD_DIM] / jnp.maximum(jnp.abs(tot[:, HEAD_DIM:]), floor)
                hsum = hd_out if hsum is None else hsum + hd_out
            ml_rows.append(hsum)
        y_ret.append(jnp.concatenate(ret_rows, axis=0) * _silu(post[:, :HEAD_DIM]))
        y_ml.append(_rms(_sigmoid(post[:, HEAD_DIM:2 * HEAD_DIM]) * jnp.concatenate(ml_rows, axis=0)))

    y_ret = jnp.concatenate(y_ret, axis=1).astype(BF16)
    y_ml = jnp.concatenate(y_ml, axis=1).astype(BF16)
    bg_ret = jnp.concatenate(bg_ret, axis=1)
    bg_ml = jnp.concatenate(bg_ml, axis=1)
    merged = _sigmoid(bg_ret) * _dot(y_ret, wru_ref[...]) + _sigmoid(bg_ml) * _dot(y_ml, wmu_ref[...])
    o_ref[0] = xf + g1_ref[0] * _dot(merged.astype(BF16), wout_ref[...])


def _mixer(ret_consts, x, gain, shift, scale, gate1, q, kt, v, gr, gc, chunk_states, wpost, wru, wmu, wout):
    b, n, d = x.shape
    L = SCAN_CHUNK
    t = MIX_TOKENS
    ch = t // L
    sf, sb, cf, cb, mf, mb = chunk_states
    tok3 = lambda i, j: (i, j, 0)
    feat3 = lambda i, j: (i, 0, j)
    mod3 = lambda i, j: (i, 0, 0)
    const2 = lambda i, j: (0, 0)
    st5 = lambda i, j: (i, j, 0, 0, 0)
    st4 = lambda i, j: (i, j, 0, 0)
    in_specs = [pl.BlockSpec(memory_space=pltpu.SMEM),
                pl.BlockSpec((1, t, d), tok3),
                pl.BlockSpec((1, d), const2),
                pl.BlockSpec((1, 1, d), mod3), pl.BlockSpec((1, 1, d), mod3), pl.BlockSpec((1, 1, d), mod3),
                pl.BlockSpec((1, t, 2 * MIX_W), tok3),
                pl.BlockSpec((1, 2 * MIX_W, t), feat3),
                pl.BlockSpec((1, t, 2 * MIX_W), tok3),
                pl.BlockSpec((1, 16, t), feat3),
                pl.BlockSpec((1, t, 16), tok3),
                pl.BlockSpec((1, ch, HEADS, HEAD_DIM, HEAD_DIM), st5),
                pl.BlockSpec((1, ch, HEADS, HEAD_DIM, HEAD_DIM), st5),
                pl.BlockSpec((1, ch, HEADS, HEAD_DIM, 2 * HEAD_DIM), st5),
                pl.BlockSpec((1, ch, HEADS, HEAD_DIM, 2 * HEAD_DIM), st5),
                pl.BlockSpec((1, ch, HEADS, HEAD_DIM), st4),
                pl.BlockSpec((1, ch, HEADS, HEAD_DIM), st4),
                pl.BlockSpec(wpost.shape, const2),
                pl.BlockSpec(wru.shape, const2),
                pl.BlockSpec(wmu.shape, const2),
                pl.BlockSpec(wout.shape, const2)]
    return pl.pallas_call(
        _mixer_kernel,
        grid=(b, n // t),
        in_specs=in_specs,
        out_specs=pl.BlockSpec((1, t, d), tok3),
        out_shape=jax.ShapeDtypeStruct((b, n, d), F32),
        scratch_shapes=[pltpu.VMEM((HEADS, L, L), F32), pltpu.VMEM((HEADS, L, 2 * HEAD_DIM), F32)],
        compiler_params=pltpu.CompilerParams(dimension_semantics=("arbitrary", "arbitrary"),
                                             vmem_limit_bytes=VMEM_LIMIT),
        name="mixer",
    )(ret_consts, x, gain, shift, scale, gate1, q, kt, v, gr, gc, sf, sb, cf, cb, mf, mb,
      wpost, wru, wmu, wout)


def _ffn_kernel(x_ref, gain_ref, sh_ref, sc_ref, g2_ref, fgain_ref, w1_ref, w2_ref, o_ref, *, hidden):
    xf = x_ref[0]
    h = _rms(xf) * gain_ref[...]
    hb = (h * (1.0 + sc_ref[0]) + sh_ref[0]).astype(BF16)
    acc = None
    start = 0
    for width in FFN_SPLITS:
        gate = _dot(hb, w1_ref[:, start:start + width])
        up = _dot(hb, w1_ref[:, hidden + start:hidden + start + width])
        act = (_silu(gate) * up).astype(BF16)
        part = _dot(act, w2_ref[start:start + width, :])
        acc = part if acc is None else acc + part
        start += width
    o_ref[0] = _rms(xf + g2_ref[0] * acc) * fgain_ref[...]


def _ffn(x, gain, shift, scale, gate2, final_gain, w1, w2):
    b, n, d = x.shape
    t = FFN_TOKENS
    hidden = w2.shape[0]
    assert sum(FFN_SPLITS) == hidden
    tok3 = lambda i, j: (i, j, 0)
    mod3 = lambda i, j: (i, 0, 0)
    const2 = lambda i, j: (0, 0)
    return pl.pallas_call(
        functools.partial(_ffn_kernel, hidden=hidden),
        grid=(b, n // t),
        in_specs=[pl.BlockSpec((1, t, d), tok3),
                  pl.BlockSpec((1, d), const2),
                  pl.BlockSpec((1, 1, d), mod3), pl.BlockSpec((1, 1, d), mod3), pl.BlockSpec((1, 1, d), mod3),
                  pl.BlockSpec((1, d), const2),
                  pl.BlockSpec(w1.shape, const2),
                  pl.BlockSpec(w2.shape, const2)],
        out_specs=pl.BlockSpec((1, t, d), tok3),
        out_shape=jax.ShapeDtypeStruct((b, n, d), F32),
        compiler_params=pltpu.CompilerParams(dimension_semantics=("parallel", "parallel"),
                                             vmem_limit_bytes=VMEM_LIMIT),
        name="ffn",
    )(x, gain, shift, scale, gate2, final_gain, w1, w2)


def _rope_tables(n):
    n_rows = n // GRID_W
    rows = jnp.broadcast_to(jnp.arange(n_rows, dtype=F32)[:, None], (n_rows, GRID_W)).reshape(n)
    cols = jnp.broadcast_to(jnp.arange(GRID_W, dtype=F32)[None, :], (n_rows, GRID_W)).reshape(n)
    n_freq = HEAD_DIM // 4
    inv = ROPE_BASE ** (-jnp.arange(n_freq, dtype=F32) / n_freq)
    ang = jnp.concatenate([rows[:, None] * inv, cols[:, None] * inv], axis=-1)
    cos, sin = jnp.cos(ang), jnp.sin(ang)
    cos2 = jnp.concatenate([cos, cos], axis=-1)
    sin2 = jnp.concatenate([-sin, sin], axis=-1)
    return cos2, sin2, cos2.T, sin2.T


def kernel(x, c, ctx, c_ctx, w_ada, b_ada, norm1_gain, norm2_gain, w_in, mlstm_gate_bias, ret_decay_logit,
           w_ret_up, w_ml_up, w_out, w_ffn_in, w_ffn_out, final_gain):
    assert w_ada.shape[0] == 1, "single-layer block"
    b, n, d = x.shape
    assert n % PROJ_TOKENS == 0 and n % FFN_TOKENS == 0 and n % MIX_TOKENS == 0
    assert MIX_TOKENS % SCAN_CHUNK == 0 and ctx.shape[1] % SCAN_CHUNK == 0

    rows = -(-(b + 1) // 16) * 16
    cc = jnp.concatenate([c, c_ctx[None, :], jnp.zeros((rows - b - 1, d), F32)], axis=0)
    mod = _adaln(cc, w_ada[0], b_ada[0])
    sh1, sc1, g1, sh2, sc2, g2 = (mod[:b, i * d:(i + 1) * d].reshape(b, 1, d) for i in range(6))
    csh1 = jnp.broadcast_to(mod[b, 0:d].reshape(1, 1, d), (b, 1, d))
    csc1 = jnp.broadcast_to(mod[b, d:2 * d].reshape(1, 1, d), (b, 1, d))

    w = w_in[0]
    o = [0]
    for width in (MIX_W, MIX_W, MIX_W, MIX_W, MIX_W, MIX_W, MIX_W, MIX_W, 4 * HEADS, d, d):
        o.append(o[-1] + width)
    col = lambda i: w[:, o[i]:o[i + 1]]
    gates = col(8).reshape(d, 4, HEADS)
    gates = jnp.concatenate([gates[:, 0], gates[:, 2], gates[:, 1], gates[:, 3]], axis=1)
    gb = mlstm_gate_bias[0]
    gbias = jnp.concatenate([gb[0], gb[2], gb[1], gb[3]]).reshape(4 * HEADS, 1).astype(F32)
    wq = jnp.concatenate([col(0), col(4)], axis=1).astype(BF16)
    wv = jnp.concatenate([col(2), col(6)], axis=1).astype(BF16)
    wkt = jnp.concatenate([col(1), col(5), gates], axis=1).T.astype(BF16)
    bg_w = d // HEADS
    wpost = jnp.concatenate(
        [piece for hd in range(HEADS)
         for piece in (col(3)[:, hd * HEAD_DIM:(hd + 1) * HEAD_DIM], col(7)[:, hd * HEAD_DIM:(hd + 1) * HEAD_DIM],
                       col(9)[:, hd * bg_w:(hd + 1) * bg_w], col(10)[:, hd * bg_w:(hd + 1) * bg_w])],
        axis=1).astype(BF16)

    log_gamma = jax.nn.log_sigmoid(ret_decay_logit[0].astype(F32)).reshape(2 * HEADS)
    ret_consts = jnp.concatenate([log_gamma, jnp.exp(log_gamma * SCAN_CHUNK)])

    gain1 = norm1_gain[0].reshape(1, d)
    _, v_c, kt_c, gr_c, _ = _proj(ctx, gain1, csh1, csc1, wq, wv, wkt, gbias, None)
    ctx_final = _state_scan(ret_consts, kt_c, v_c, gr_c, None, emit_chunks=False, emit_final=True)

    q_x, v_x, kt_x, gr_x, gc_x = _proj(x, gain1, sh1, sc1, wq, wv, wkt, gbias, _rope_tables(n))
    chunk_states = _state_scan(ret_consts, kt_x, v_x, gr_x, ctx_final, emit_chunks=True, emit_final=False)
    x1 = _mixer(ret_consts, x, gain1, sh1, sc1, g1, q_x, kt_x, v_x, gr_x, gc_x, chunk_states,
                wpost, w_ret_up[0].astype(BF16), w_ml_up[0].astype(BF16), w_out[0].astype(BF16))
    return _ffn(x1, norm2_gain[0].reshape(1, d), sh2, sc2, g2, final_gain.reshape(1, d),
                w_ffn_in[0].astype(BF16), w_ffn_out[0].astype(BF16))
```

```python
import functools

import jax
import jax.numpy as jnp
from jax import lax
from jax.experimental import pallas as pl
from jax.experimental.pallas import tpu as pltpu

HEADS = 4
HEAD_DIM = 128
MIX_W = HEADS * HEAD_DIM
GRID_W = 64
ROPE_BASE = 10000.0
EPS = 1e-6
SCAN_CHUNK = 256
PROJ_TOKENS = 512
BWD_CHUNKS = 4
MIX_TOKENS = 512
FFN_TOKENS = 1024
FFN_SPLITS = (1536, 1280)
VMEM_LIMIT = 56 * 1024 * 1024

F32 = jnp.float32
BF16 = jnp.bfloat16


def _resident(shape):
    return pl.BlockSpec(shape, lambda *_: (0,) * len(shape), pipeline_mode=pl.Buffered(1))


def _dot(a, b):
    return jnp.dot(a, b, preferred_element_type=F32)


def _dot_nt(a, b):
    return lax.dot_general(a, b, (((1,), (1,)), ((), ())), preferred_element_type=F32)


def _sigmoid(t):
    return 0.5 * jnp.tanh(0.5 * t) + 0.5


def _silu(t):
    return t * _sigmoid(t)


def _log_sigmoid(t):
    return jnp.minimum(t, 0.0) - jnp.log1p(jnp.exp(-jnp.abs(t)))


def _rms(t):
    return t * lax.rsqrt(jnp.mean(t * t, axis=-1, keepdims=True) + EPS)


def _split3(t):
    hi = t.astype(BF16)
    r1 = t - hi.astype(F32)
    mid = r1.astype(BF16)
    lo = (r1 - mid.astype(F32)).astype(BF16)
    return hi, mid, lo


def _adaln_kernel(c_ref, w_ref, b_ref, o_ref):
    s = _silu(c_ref[...]).astype(BF16)
    o_ref[...] = _dot(s, w_ref[...].astype(BF16)) + b_ref[...]


def _adaln(cc, w, b):
    rows, d = cc.shape
    cols = w.shape[1]
    blk = 1536
    return pl.pallas_call(
        _adaln_kernel,
        grid=(cols // blk,),
        in_specs=[pl.BlockSpec((rows, d), lambda j: (0, 0)),
                  pl.BlockSpec((d, blk), lambda j: (0, j)),
                  pl.BlockSpec((1, blk), lambda j: (0, j))],
        out_specs=pl.BlockSpec((rows, blk), lambda j: (0, j)),
        out_shape=jax.ShapeDtypeStruct((rows, cols), F32),
        compiler_params=pltpu.CompilerParams(dimension_semantics=("parallel",),
                                             vmem_limit_bytes=VMEM_LIMIT),
        name="adaln",
    )(cc, w, b.reshape(1, cols))


def _project_tile(x_ref, gain_ref, sh_ref, sc_ref, wq_ref, wv_ref, wkt_ref, gb_ref, rope_refs,
                  q_ref, v_ref, kt_ref, gr_ref, gc_ref):
    L = SCAN_CHUNK
    tokens = x_ref.shape[1]
    h = _rms(x_ref[0]) * gain_ref[...]
    h = h * (1.0 + sc_ref[0]) + sh_ref[0]
    hb = h.astype(BF16)

    q = _dot(hb, wq_ref[...])
    for hd in range(HEADS):
        sl = slice(hd * HEAD_DIM, (hd + 1) * HEAD_DIM)
        t = q[:, sl]
        if rope_refs is not None:
            cos_ref, sin_ref = rope_refs[0], rope_refs[1]
            t = t * cos_ref[...] + pltpu.roll(t, HEAD_DIM // 2, 1) * sin_ref[...]
        q_ref[0, :, sl] = t.astype(BF16)
    q_ref[0, :, MIX_W:] = q[:, MIX_W:].astype(BF16)

    v_ref[0] = _dot(hb, wv_ref[...]).astype(BF16)

    kg = _dot_nt(wkt_ref[...], hb)
    kscale = HEAD_DIM ** -0.5
    for hd in range(HEADS):
        sl = slice(hd * HEAD_DIM, (hd + 1) * HEAD_DIM)
        t = kg[sl, :] * kscale
        if rope_refs is not None:
            cost_ref, sint_ref = rope_refs[2], rope_refs[3]
            half = HEAD_DIM // 2
            rot = jnp.concatenate([t[half:, :], t[:half, :]], axis=0)
            t = t * cost_ref[...] + rot * sint_ref[...]
        kt_ref[0, sl, :] = t.astype(BF16)
    kt_ref[0, MIX_W:, :] = (kg[MIX_W:2 * MIX_W, :] * kscale).astype(BF16)

    g = kg[2 * MIX_W:, :] + gb_ref[...]
    i_pre = g[0:8, :]
    log_f = _log_sigmoid(g[8:16, :])
    src = lax.broadcasted_iota(jnp.int32, (L, L), 0)
    dst = lax.broadcasted_iota(jnp.int32, (L, L), 1)
    prefix_m = (src <= dst).astype(BF16)
    suffix_m = (src >= dst).astype(BF16)
    is_fwd = lax.broadcasted_iota(jnp.int32, (8, 1), 0) < HEADS
    for ci in range(tokens // L):
        cs = slice(ci * L, (ci + 1) * L)
        lf = log_f[:, cs]
        parts = _split3(jnp.concatenate([lf, lf], axis=0))
        pre = sum(_dot(p, prefix_m) for p in parts)[0:8]
        suf = sum(_dot(p, suffix_m) for p in parts)[0:8]
        cum = jnp.where(is_fwd, pre, suf)
        stats = jnp.concatenate([i_pre[:, cs] - cum, cum], axis=0)
        gr_ref[0, :, cs] = stats
        gc_ref[0, cs, :] = stats.T


S_SHAPE = (2, HEADS, HEAD_DIM, HEAD_DIM)
C_SHAPE = (2, HEADS, HEAD_DIM, 2 * HEAD_DIM)
M_SHAPE = (2 * HEADS, HEAD_DIM)


def _chunk_sources(rc_ref, kt_ref, v_ref, gr_ref, cs, d):
    L = SCAN_CHUNK
    pos = lax.broadcasted_iota(jnp.int32, (1, L), 1).astype(F32)
    ones = jnp.ones((L, HEAD_DIM), BF16)
    a = gr_ref[0, 4 * d:4 * d + 4, cs]
    cum = gr_ref[0, 8 + 4 * d:12 + 4 * d, cs]
    edge = cum[:, L - 1:L] if d == 0 else cum[:, 0:1]
    b_last = jnp.broadcast_to(edge, (HEADS, HEAD_DIM))
    amax = jnp.broadcast_to(jnp.max(a, axis=1, keepdims=True), (HEADS, HEAD_DIM))
    w_loc = jnp.exp(a - jnp.concatenate([amax] * (L // HEAD_DIM), axis=1))
    u_ret, u_ml = [], []
    for hd in range(HEADS):
        sl = slice(hd * HEAD_DIM, (hd + 1) * HEAD_DIM)
        ml = slice(MIX_W + hd * HEAD_DIM, MIX_W + (hd + 1) * HEAD_DIM)
        lg = rc_ref[4 * d + hd]
        to_end = jnp.exp(lg * ((L - 1.0) - pos)) if d == 0 else jnp.exp(lg * pos)
        kw = (kt_ref[0, sl, cs].astype(F32) * to_end).astype(BF16)
        u_ret.append(_dot(kw, v_ref[0, cs, sl]))
        kw = (kt_ref[0, ml, cs].astype(F32) * w_loc[hd:hd + 1, :]).astype(BF16)
        u_ml.append(_dot(kw, jnp.concatenate([v_ref[0, cs, ml], ones], axis=1)))
    return amax, b_last, u_ret, u_ml


def _advance_state(rc_ref, s_scr, c_scr, m_scr, d, amax, b_last, u_ret, u_ml, emit):
    rows = slice(HEADS * d, HEADS * (d + 1))
    m_old = m_scr[rows, :]
    m_mid = jnp.maximum(m_old, amax)
    w_old = jnp.exp(m_old - m_mid)
    w_new = jnp.exp(amax - m_mid)
    if emit is not None:
        s_out, c_out, m_out, slot = emit
        m_out[0, slot] = m_old
    m_scr[rows, :] = b_last + m_mid
    for hd in range(HEADS):
        s_prev = s_scr[d, hd]
        c_prev = c_scr[d, hd]
        if emit is not None:
            s_out[0, slot, hd] = s_prev.astype(BF16)
            c_out[0, slot, hd] = c_prev.astype(BF16)
        s_scr[d, hd] = rc_ref[2 * HEADS + HEADS * d + hd] * s_prev + u_ret[hd]
        wo, wn = w_old[hd:hd + 1, :], w_new[hd:hd + 1, :]
        c_scr[d, hd] = (jnp.concatenate([wo, wo], axis=1) * c_prev
                        + jnp.concatenate([wn, wn], axis=1) * u_ml[hd])


def _ctx_kernel(rc_ref, x_ref, gain_ref, sh_ref, sc_ref, wq_ref, wv_ref, wkt_ref, gb_ref,
                s_fin, c_fin, m_fin, q_scr, v_scr, kt_scr, gr_scr, gc_scr, s_scr, c_scr, m_scr):
    L = SCAN_CHUNK
    _project_tile(x_ref, gain_ref, sh_ref, sc_ref, wq_ref, wv_ref, wkt_ref, gb_ref, None,
                  q_scr, v_scr, kt_scr, gr_scr, gc_scr)
    s_scr[...] = jnp.zeros_like(s_scr)
    c_scr[...] = jnp.zeros_like(c_scr)
    m_scr[...] = jnp.zeros_like(m_scr)
    n_chunks = x_ref.shape[1] // L
    for d in range(2):
        for ci in (range(n_chunks) if d == 0 else reversed(range(n_chunks))):
            cs = slice(ci * L, (ci + 1) * L)
            _advance_state(rc_ref, s_scr, c_scr, m_scr, d, *_chunk_sources(rc_ref, kt_scr, v_scr, gr_scr, cs, d),
                           None)
    s_fin[0] = s_scr[...]
    c_fin[0] = c_scr[...]
    m_fin[0] = m_scr[...]


def _ctx_states(ret_consts, ctx, gain, shift, scale, wq, wv, wkt, gbias):
    b, n, d = ctx.shape
    per_b = lambda shape: pl.BlockSpec((1,) + shape, lambda i: (i,) + (0,) * len(shape))
    return pl.pallas_call(
        _ctx_kernel,
        grid=(b,),
        in_specs=[pl.BlockSpec(memory_space=pltpu.SMEM),
                  per_b((n, d)), _resident((1, d)), per_b((1, d)), per_b((1, d)),
                  _resident(wq.shape), _resident(wv.shape), _resident(wkt.shape), _resident(gbias.shape)],
        out_specs=(per_b(S_SHAPE), per_b(C_SHAPE), per_b(M_SHAPE)),
        out_shape=tuple(jax.ShapeDtypeStruct((b,) + shape, F32) for shape in (S_SHAPE, C_SHAPE, M_SHAPE)),
        scratch_shapes=[pltpu.VMEM((1, n, 2 * MIX_W), BF16), pltpu.VMEM((1, n, 2 * MIX_W), BF16),
                        pltpu.VMEM((1, 2 * MIX_W, n), BF16), pltpu.VMEM((1, 16, n), F32),
                        pltpu.VMEM((1, n, 16), F32),
                        pltpu.VMEM(S_SHAPE, F32), pltpu.VMEM(C_SHAPE, F32), pltpu.VMEM(M_SHAPE, F32)],
        compiler_params=pltpu.CompilerParams(dimension_semantics=("arbitrary",),
                                             vmem_limit_bytes=VMEM_LIMIT),
        name="ctx_states",
    )(ret_consts, ctx, gain, shift, scale, wq, wv, wkt, gbias)


def _proj_scan_kernel(rc_ref, x_ref, gain_ref, sh_ref, sc_ref, wq_ref, wv_ref, wkt_ref, gb_ref,
                      cos_ref, sin_ref, cost_ref, sint_ref, s0_ref, c0_ref, m0_ref,
                      q_ref, v_ref, kt_ref, gr_ref, gc_ref, sf_ref, cf_ref, mf_ref, sb_ref, cb_ref, mb_ref,
                      s_scr, c_scr, m_scr, ub_ret, ub_ml, ub_stat, *, n_tiles, bwd_chunks):
    L = SCAN_CHUNK
    j = pl.program_id(1)
    per_tile = x_ref.shape[1] // L
    n_chunks = n_tiles * per_tile

    @pl.when(j == 0)
    def _():
        s_scr[...] = s0_ref[0]
        c_scr[...] = c0_ref[0]
        m_scr[...] = m0_ref[0]

    @pl.when(j < n_tiles)
    def _():
        _project_tile(x_ref, gain_ref, sh_ref, sc_ref, wq_ref, wv_ref, wkt_ref, gb_ref,
                      (cos_ref, sin_ref, cost_ref, sint_ref), q_ref, v_ref, kt_ref, gr_ref, gc_ref)
        for ci in range(per_tile):
            cs = slice(ci * L, (ci + 1) * L)
            _advance_state(rc_ref, s_scr, c_scr, m_scr, 0, *_chunk_sources(rc_ref, kt_ref, v_ref, gr_ref, cs, 0),
                           (sf_ref, cf_ref, mf_ref, ci))
            amax, b_last, u_ret, u_ml = _chunk_sources(rc_ref, kt_ref, v_ref, gr_ref, cs, 1)
            chunk = j * per_tile + ci
            ub_stat[chunk] = jnp.concatenate([amax, b_last], axis=0)
            for hd in range(HEADS):
                ub_ret[chunk, hd] = u_ret[hd].astype(BF16)
                ub_ml[chunk, hd] = u_ml[hd].astype(BF16)

    @pl.when(j >= n_tiles)
    def _():
        first = n_chunks - 1 - (j - n_tiles) * bwd_chunks
        for i in range(bwd_chunks):
            chunk = first - i
            stat = ub_stat[chunk]
            _advance_state(rc_ref, s_scr, c_scr, m_scr, 1, stat[0:HEADS], stat[HEADS:2 * HEADS],
                           [ub_ret[chunk, hd].astype(F32) for hd in range(HEADS)],
                           [ub_ml[chunk, hd].astype(F32) for hd in range(HEADS)],
                           (sb_ref, cb_ref, mb_ref, bwd_chunks - 1 - i))


def _proj_scan(ret_consts, x, gain, shift, scale, wq, wv, wkt, gbias, rope, init):
    b, n, d = x.shape
    L = SCAN_CHUNK
    t = PROJ_TOKENS
    nt = n // t
    nc = n // L
    per_tile = t // L
    nbb = nc // BWD_CHUNKS
    tile = lambda j: jnp.minimum(j, nt - 1)
    bblk = lambda j: nbb - 1 - jnp.maximum(j - nt, 0)
    tok3 = lambda i, j: (i, tile(j), 0)
    feat3 = lambda i, j: (i, 0, tile(j))
    mod3 = lambda i, j: (i, 0, 0)
    per_b = lambda shape: pl.BlockSpec((1,) + shape, lambda i, j: (i,) + (0,) * len(shape))
    cos, sin, cos_t, sin_t = rope
    in_specs = [pl.BlockSpec(memory_space=pltpu.SMEM),
                pl.BlockSpec((1, t, d), tok3),
                _resident((1, d)),
                pl.BlockSpec((1, 1, d), mod3), pl.BlockSpec((1, 1, d), mod3),
                _resident(wq.shape), _resident(wv.shape), _resident(wkt.shape), _resident(gbias.shape),
                pl.BlockSpec((t, HEAD_DIM), lambda i, j: (tile(j), 0)),
                pl.BlockSpec((t, HEAD_DIM), lambda i, j: (tile(j), 0)),
                pl.BlockSpec((HEAD_DIM, t), lambda i, j: (0, tile(j))),
                pl.BlockSpec((HEAD_DIM, t), lambda i, j: (0, tile(j))),
                per_b(S_SHAPE), per_b(C_SHAPE), per_b(M_SHAPE)]
    out_shape = [jax.ShapeDtypeStruct((b, n, 2 * MIX_W), BF16),
                 jax.ShapeDtypeStruct((b, n, 2 * MIX_W), BF16),
                 jax.ShapeDtypeStruct((b, 2 * MIX_W, n), BF16),
                 jax.ShapeDtypeStruct((b, 16, n), F32),
                 jax.ShapeDtypeStruct((b, n, 16), F32)]
    out_specs = [pl.BlockSpec((1, t, 2 * MIX_W), tok3),
                 pl.BlockSpec((1, t, 2 * MIX_W), tok3),
                 pl.BlockSpec((1, 2 * MIX_W, t), feat3),
                 pl.BlockSpec((1, 16, t), feat3),
                 pl.BlockSpec((1, t, 16), tok3)]
    for blk, step in ((per_tile, tile), (BWD_CHUNKS, bblk)):
        out_shape += [jax.ShapeDtypeStruct((b, nc) + S_SHAPE[1:], BF16),
                      jax.ShapeDtypeStruct((b, nc) + C_SHAPE[1:], BF16),
                      jax.ShapeDtypeStruct((b, nc, HEADS, HEAD_DIM), F32)]
        out_specs += [pl.BlockSpec((1, blk) + S_SHAPE[1:], lambda i, j, step=step: (i, step(j), 0, 0, 0)),
                      pl.BlockSpec((1, blk) + C_SHAPE[1:], lambda i, j, step=step: (i, step(j), 0, 0, 0)),
                      pl.BlockSpec((1, blk, HEADS, HEAD_DIM), lambda i, j, step=step: (i, step(j), 0, 0))]
    return pl.pallas_call(
        functools.partial(_proj_scan_kernel, n_tiles=nt, bwd_chunks=BWD_CHUNKS),
        grid=(b, nt + nbb),
        in_specs=in_specs,
        out_specs=tuple(out_specs),
        out_shape=tuple(out_shape),
        scratch_shapes=[pltpu.VMEM(S_SHAPE, F32), pltpu.VMEM(C_SHAPE, F32), pltpu.VMEM(M_SHAPE, F32),
                        pltpu.VMEM((nc,) + S_SHAPE[1:], BF16), pltpu.VMEM((nc,) + C_SHAPE[1:], BF16),
                        pltpu.VMEM((nc,) + M_SHAPE, F32)],
        compiler_params=pltpu.CompilerParams(dimension_semantics=("arbitrary", "arbitrary"),
                                             vmem_limit_bytes=VMEM_LIMIT),
        name="proj_scan",
    )(ret_consts, x, gain, shift, scale, wq, wv, wkt, gbias, cos, sin, cos_t, sin_t, *init)


def _mixer_kernel(rc_ref, x_ref, gain_ref, sh_ref, sc_ref, g1_ref, q_ref, kt_ref, v_ref, gr_ref, gc_ref,
                  sf_ref, sb_ref, cf_ref, cb_ref, mf_ref, mb_ref,
                  wpost_ref, wru_ref, wmu_ref, wout_ref, o_ref, dec_scr, qdec_scr):
    L = SCAN_CHUNK
    row = lax.broadcasted_iota(jnp.int32, (L, L), 0)
    col = lax.broadcasted_iota(jnp.int32, (L, L), 1)

    @pl.when((pl.program_id(0) == 0) & (pl.program_id(1) == 0))
    def _():
        rel = (row - col).astype(F32)
        lpos = lax.broadcasted_iota(jnp.int32, (L, HEAD_DIM), 0).astype(F32)
        for hd in range(HEADS):
            lg_f = rc_ref[hd]
            lg_b = rc_ref[4 + hd]
            dec_scr[hd] = (jnp.where(rel >= 0, jnp.exp(lg_f * jnp.maximum(rel, 0.0)), 0.0)
                           + jnp.where(rel <= 0, jnp.exp(lg_b * jnp.maximum(-rel, 0.0)), 0.0))
            qdec_scr[hd] = jnp.concatenate([jnp.exp(lg_f * (lpos + 1.0)), jnp.exp(lg_b * (L - lpos))], axis=1)

    xf = x_ref[0]
    h = _rms(xf) * gain_ref[...]
    hb = (h * (1.0 + sc_ref[0]) + sh_ref[0]).astype(BF16)
    d_model = xf.shape[-1]
    bg_w = d_model // HEADS
    post_w = 2 * HEAD_DIM + 2 * bg_w

    masks = (col <= row, col >= row)
    c_refs, m_refs = (cf_ref, cb_ref), (mf_ref, mb_ref)
    ones = jnp.ones((L, HEAD_DIM), BF16)
    neg_inf = jnp.float32(-jnp.inf)
    y_ret, y_ml, bg_ret, bg_ml = [], [], [], []
    for hd in range(HEADS):
        sl = slice(hd * HEAD_DIM, (hd + 1) * HEAD_DIM)
        ml = slice(MIX_W + hd * HEAD_DIM, MIX_W + (hd + 1) * HEAD_DIM)
        post = _dot(hb, wpost_ref[:, hd * post_w:(hd + 1) * post_w])
        bg_ret.append(post[:, 2 * HEAD_DIM:2 * HEAD_DIM + bg_w])
        bg_ml.append(post[:, 2 * HEAD_DIM + bg_w:])
        ret_rows, ml_rows = [], []
        for ci in range(xf.shape[0] // L):
            tok = slice(ci * L, (ci + 1) * L)
            qh = q_ref[0, tok, sl]
            p = (_dot(qh, kt_ref[0, sl, tok]) * dec_scr[hd]).astype(BF16)
            qf = qh.astype(F32)
            qw = (jnp.concatenate([qf, qf], axis=1) * qdec_scr[hd]).astype(BF16)
            st = jnp.concatenate([sf_ref[0, ci, hd], sb_ref[0, ci, hd]], axis=0)
            ret_rows.append(_rms(_dot(p, v_ref[0, tok, sl]) + _dot(qw, st)))
            qh = q_ref[0, tok, ml]
            scores = _dot(qh, kt_ref[0, ml, tok])
            v_ext = jnp.concatenate([v_ref[0, tok, ml], ones], axis=1)
            hsum = None
            for d in range(2):
                a_row = gr_ref[0, 4 * d + hd:4 * d + hd + 1, tok]
                cum_col = gc_ref[0, tok, 8 + 4 * d + hd:9 + 4 * d + hd]
                m_in = m_refs[d][0, ci, hd:hd + 1, :]
                a_masked = jnp.where(masks[d], a_row, neg_inf)
                cmax = jnp.broadcast_to(jnp.max(a_masked, axis=1, keepdims=True), (L, HEAD_DIM))
                m_loc = jnp.maximum(cmax, m_in)
                w = jnp.exp(a_masked - jnp.concatenate([m_loc] * (L // HEAD_DIM), axis=1))
                tot = _dot((scores * w).astype(BF16), v_ext)
                w_inter = jnp.exp(m_in - m_loc)
                tot = tot + jnp.concatenate([w_inter, w_inter], axis=1) * _dot(qh, c_refs[d][0, ci, hd])
                floor = jnp.exp(-(jnp.broadcast_to(cum_col, (L, HEAD_DIM)) + m_loc))
                hd_out = tot[:, :HEAD_DIM] / jnp.maximum(jnp.abs(tot[:, HEAD_DIM:]), floor)
                hsum = hd_out if hsum is None else hsum + hd_out
            ml_rows.append(hsum)
        y_ret.append(jnp.concatenate(ret_rows, axis=0) * _silu(post[:, :HEAD_DIM]))
        y_ml.append(_rms(_sigmoid(post[:, HEAD_DIM:2 * HEAD_DIM]) * jnp.concatenate(ml_rows, axis=0)))

    y_ret = jnp.concatenate(y_ret, axis=1).astype(BF16)
    y_ml = jnp.concatenate(y_ml, axis=1).astype(BF16)
    bg_ret = jnp.concatenate(bg_ret, axis=1)
    bg_ml = jnp.concatenate(bg_ml, axis=1)
    merged = _sigmoid(bg_ret) * _dot(y_ret, wru_ref[...]) + _sigmoid(bg_ml) * _dot(y_ml, wmu_ref[...])
    o_ref[0] = xf + g1_ref[0] * _dot(merged.astype(BF16), wout_ref[...])


def _mixer(ret_consts, x, gain, shift, scale, gate1, q, kt, v, gr, gc, chunk_states, wpost, wru, wmu, wout):
    b, n, d = x.shape
    L = SCAN_CHUNK
    t = MIX_TOKENS
    ch = t // L
    sf, sb, cf, cb, mf, mb = chunk_states
    tok3 = lambda i, j: (i, j, 0)
    feat3 = lambda i, j: (i, 0, j)
    mod3 = lambda i, j: (i, 0, 0)
    st5 = lambda i, j: (i, j, 0, 0, 0)
    st4 = lambda i, j: (i, j, 0, 0)
    in_specs = [pl.BlockSpec(memory_space=pltpu.SMEM),
                pl.BlockSpec((1, t, d), tok3),
                _resident((1, d)),
                pl.BlockSpec((1, 1, d), mod3), pl.BlockSpec((1, 1, d), mod3), pl.BlockSpec((1, 1, d), mod3),
                pl.BlockSpec((1, t, 2 * MIX_W), tok3),
                pl.BlockSpec((1, 2 * MIX_W, t), feat3),
                pl.BlockSpec((1, t, 2 * MIX_W), tok3),
                pl.BlockSpec((1, 16, t), feat3),
                pl.BlockSpec((1, t, 16), tok3),
                pl.BlockSpec((1, ch) + S_SHAPE[1:], st5),
                pl.BlockSpec((1, ch) + S_SHAPE[1:], st5),
                pl.BlockSpec((1, ch) + C_SHAPE[1:], st5),
                pl.BlockSpec((1, ch) + C_SHAPE[1:], st5),
                pl.BlockSpec((1, ch, HEADS, HEAD_DIM), st4),
                pl.BlockSpec((1, ch, HEADS, HEAD_DIM), st4),
                _resident(wpost.shape),
                _resident(wru.shape),
                _resident(wmu.shape),
                _resident(wout.shape)]
    return pl.pallas_call(
        _mixer_kernel,
        grid=(b, n // t),
        in_specs=in_specs,
        out_specs=pl.BlockSpec((1, t, d), tok3),
        out_shape=jax.ShapeDtypeStruct((b, n, d), F32),
        scratch_shapes=[pltpu.VMEM((HEADS, L, L), F32), pltpu.VMEM((HEADS, L, 2 * HEAD_DIM), F32)],
        compiler_params=pltpu.CompilerParams(dimension_semantics=("arbitrary", "arbitrary"),
                                             vmem_limit_bytes=VMEM_LIMIT),
        name="mixer",
    )(ret_consts, x, gain, shift, scale, gate1, q, kt, v, gr, gc, sf, sb, cf, cb, mf, mb,
      wpost, wru, wmu, wout)


def _ffn_kernel(x_ref, gain_ref, sh_ref, sc_ref, g2_ref, fgain_ref, w1_ref, w2_ref, o_ref, *, hidden):
    xf = x_ref[0]
    h = _rms(xf) * gain_ref[...]
    hb = (h * (1.0 + sc_ref[0]) + sh_ref[0]).astype(BF16)
    acc = None
    start = 0
    for width in FFN_SPLITS:
        gate = _dot(hb, w1_ref[:, start:start + width])
        up = _dot(hb, w1_ref[:, hidden + start:hidden + start + width])
        act = (_silu(gate) * up).astype(BF16)
        part = _dot(act, w2_ref[start:start + width, :])
        acc = part if acc is None else acc + part
        start += width
    o_ref[0] = _rms(xf + g2_ref[0] * acc) * fgain_ref[...]


def _ffn(x, gain, shift, scale, gate2, final_gain, w1, w2):
    b, n, d = x.shape
    t = FFN_TOKENS
    hidden = w2.shape[0]
    assert sum(FFN_SPLITS) == hidden
    tok3 = lambda i, j: (i, j, 0)
    mod3 = lambda i, j: (i, 0, 0)
    return pl.pallas_call(
        functools.partial(_ffn_kernel, hidden=hidden),
        grid=(b, n // t),
        in_specs=[pl.BlockSpec((1, t, d), tok3),
                  _resident((1, d)),
                  pl.BlockSpec((1, 1, d), mod3), pl.BlockSpec((1, 1, d), mod3), pl.BlockSpec((1, 1, d), mod3),
                  _resident((1, d)),
                  _resident(w1.shape),
                  _resident(w2.shape)],
        out_specs=pl.BlockSpec((1, t, d), tok3),
        out_shape=jax.ShapeDtypeStruct((b, n, d), F32),
        compiler_params=pltpu.CompilerParams(dimension_semantics=("parallel", "parallel"),
                                             vmem_limit_bytes=VMEM_LIMIT),
        name="ffn",
    )(x, gain, shift, scale, gate2, final_gain, w1, w2)


def _rope_tables(n):
    n_rows = n // GRID_W
    rows = jnp.broadcast_to(jnp.arange(n_rows, dtype=F32)[:, None], (n_rows, GRID_W)).reshape(n)
    cols = jnp.broadcast_to(jnp.arange(GRID_W, dtype=F32)[None, :], (n_rows, GRID_W)).reshape(n)
    n_freq = HEAD_DIM // 4
    inv = ROPE_BASE ** (-jnp.arange(n_freq, dtype=F32) / n_freq)
    ang = jnp.concatenate([rows[:, None] * inv, cols[:, None] * inv], axis=-1)
    cos, sin = jnp.cos(ang), jnp.sin(ang)
    cos2 = jnp.concatenate([cos, cos], axis=-1)
    sin2 = jnp.concatenate([-sin, sin], axis=-1)
    return cos2, sin2, cos2.T, sin2.T


def kernel(x, c, ctx, c_ctx, w_ada, b_ada, norm1_gain, norm2_gain, w_in, mlstm_gate_bias, ret_decay_logit,
           w_ret_up, w_ml_up, w_out, w_ffn_in, w_ffn_out, final_gain):
    assert w_ada.shape[0] == 1, "single-layer block"
    b, n, d = x.shape
    assert n % PROJ_TOKENS == 0 and n % FFN_TOKENS == 0 and n % MIX_TOKENS == 0
    assert PROJ_TOKENS % SCAN_CHUNK == 0 and MIX_TOKENS % SCAN_CHUNK == 0 and ctx.shape[1] % SCAN_CHUNK == 0
    assert (n // SCAN_CHUNK) % BWD_CHUNKS == 0

    rows = -(-(b + 1) // 16) * 16
    cc = jnp.concatenate([c, c_ctx[None, :], jnp.zeros((rows - b - 1, d), F32)], axis=0)
    mod = _adaln(cc, w_ada[0], b_ada[0])
    sh1, sc1, g1, sh2, sc2, g2 = (mod[:b, i * d:(i + 1) * d].reshape(b, 1, d) for i in range(6))
    csh1 = jnp.broadcast_to(mod[b, 0:d].reshape(1, 1, d), (b, 1, d))
    csc1 = jnp.broadcast_to(mod[b, d:2 * d].reshape(1, 1, d), (b, 1, d))

    w = w_in[0]
    o = [0]
    for width in (MIX_W, MIX_W, MIX_W, MIX_W, MIX_W, MIX_W, MIX_W, MIX_W, 4 * HEADS, d, d):
        o.append(o[-1] + width)
    col = lambda i: w[:, o[i]:o[i + 1]]
    gates = col(8).reshape(d, 4, HEADS)
    gates = jnp.concatenate([gates[:, 0], gates[:, 2], gates[:, 1], gates[:, 3]], axis=1)
    gb = mlstm_gate_bias[0]
    gbias = jnp.concatenate([gb[0], gb[2], gb[1], gb[3]]).reshape(4 * HEADS, 1).astype(F32)
    wq = jnp.concatenate([col(0), col(4)], axis=1).astype(BF16)
    wv = jnp.concatenate([col(2), col(6)], axis=1).astype(BF16)
    wkt = jnp.concatenate([col(1), col(5), gates], axis=1).T.astype(BF16)
    bg_w = d // HEADS
    wpost = jnp.concatenate(
        [piece for hd in range(HEADS)
         for piece in (col(3)[:, hd * HEAD_DIM:(hd + 1) * HEAD_DIM], col(7)[:, hd * HEAD_DIM:(hd + 1) * HEAD_DIM],
                       col(9)[:, hd * bg_w:(hd + 1) * bg_w], col(10)[:, hd * bg_w:(hd + 1) * bg_w])],
        axis=1).astype(BF16)

    log_gamma = jax.nn.log_sigmoid(ret_decay_logit[0].astype(F32)).reshape(2 * HEADS)
    ret_consts = jnp.concatenate([log_gamma, jnp.exp(log_gamma * SCAN_CHUNK)])

    gain1 = norm1_gain[0].reshape(1, d)
    ctx_final = _ctx_states(ret_consts, ctx, gain1, csh1, csc1, wq, wv, wkt, gbias)
    q_x, v_x, kt_x, gr_x, gc_x, sf, cf, mf, sb, cb, mb = _proj_scan(
        ret_consts, x, gain1, sh1, sc1, wq, wv, wkt, gbias, _rope_tables(n), ctx_final)
    x1 = _mixer(ret_consts, x, gain1, sh1, sc1, g1, q_x, kt_x, v_x, gr_x, gc_x, (sf, sb, cf, cb, mf, mb),
                wpost, w_ret_up[0].astype(BF16), w_ml_up[0].astype(BF16), w_out[0].astype(BF16))
    return _ffn(x1, norm2_gain[0].reshape(1, d), sh2, sc2, g2, final_gain.reshape(1, d),
                w_ffn_in[0].astype(BF16), w_ffn_out[0].astype(BF16))
```

```python
import functools

import jax
import jax.numpy as jnp
from jax import lax
from jax.experimental import pallas as pl
from jax.experimental.pallas import tpu as pltpu

HEADS = 4
HEAD_DIM = 128
MIX_W = HEADS * HEAD_DIM
GRID_W = 64
ROPE_BASE = 10000.0
EPS = 1e-6
SCAN_CHUNK = 256
PROJ_TOKENS = 512
BWD_CHUNKS = 4
MIX_TOKENS = 512
FFN_TOKENS = 1024
FFN_SPLITS = (1536, 1280)
VMEM_LIMIT = 56 * 1024 * 1024

F32 = jnp.float32
BF16 = jnp.bfloat16


def _resident(shape):
    return pl.BlockSpec(shape, lambda *_: (0,) * len(shape), pipeline_mode=pl.Buffered(1))


def _dot(a, b):
    return jnp.dot(a, b, preferred_element_type=F32)


def _dot_nt(a, b):
    return lax.dot_general(a, b, (((1,), (1,)), ((), ())), preferred_element_type=F32)


def _sigmoid(t):
    return 0.5 * jnp.tanh(0.5 * t) + 0.5


def _silu(t):
    return t * _sigmoid(t)


def _log_sigmoid(t):
    return jnp.minimum(t, 0.0) - jnp.log1p(jnp.exp(-jnp.abs(t)))


def _rms(t):
    return t * lax.rsqrt(jnp.mean(t * t, axis=-1, keepdims=True) + EPS)


def _split3(t):
    hi = t.astype(BF16)
    r1 = t - hi.astype(F32)
    mid = r1.astype(BF16)
    lo = (r1 - mid.astype(F32)).astype(BF16)
    return hi, mid, lo


def _adaln_kernel(c_ref, w_ref, b_ref, o_ref):
    s = _silu(c_ref[...]).astype(BF16)
    o_ref[...] = _dot(s, w_ref[...].astype(BF16)) + b_ref[...]


def _adaln(cc, w, b):
    rows, d = cc.shape
    cols = w.shape[1]
    blk = 1536
    return pl.pallas_call(
        _adaln_kernel,
        grid=(cols // blk,),
        in_specs=[pl.BlockSpec((rows, d), lambda j: (0, 0)),
                  pl.BlockSpec((d, blk), lambda j: (0, j)),
                  pl.BlockSpec((1, blk), lambda j: (0, j))],
        out_specs=pl.BlockSpec((rows, blk), lambda j: (0, j)),
        out_shape=jax.ShapeDtypeStruct((rows, cols), F32),
        compiler_params=pltpu.CompilerParams(dimension_semantics=("parallel",),
                                             vmem_limit_bytes=VMEM_LIMIT),
        name="adaln",
    )(cc, w, b.reshape(1, cols))


def _project_q(hb, wq_ref, rope_refs, q_ref):
    q = _dot(hb, wq_ref[...])
    for hd in range(HEADS):
        sl = slice(hd * HEAD_DIM, (hd + 1) * HEAD_DIM)
        t = q[:, sl]
        if rope_refs is not None:
            cos_ref, sin_ref = rope_refs[0], rope_refs[1]
            t = t * cos_ref[...] + pltpu.roll(t, HEAD_DIM // 2, 1) * sin_ref[...]
        q_ref[0, :, sl] = t.astype(BF16)
    q_ref[0, :, MIX_W:] = q[:, MIX_W:].astype(BF16)


def _project_kv(x_ref, gain_ref, sh_ref, sc_ref, wv_ref, wkt_ref, gb_ref, rope_refs, v_ref, kt_ref, gr_ref):
    L = SCAN_CHUNK
    tokens = x_ref.shape[1]
    h = _rms(x_ref[0]) * gain_ref[...]
    h = h * (1.0 + sc_ref[0]) + sh_ref[0]
    hb = h.astype(BF16)

    kg = _dot_nt(wkt_ref[...], hb)
    kscale = HEAD_DIM ** -0.5
    for hd in range(HEADS):
        sl = slice(hd * HEAD_DIM, (hd + 1) * HEAD_DIM)
        t = kg[sl, :] * kscale
        if rope_refs is not None:
            cost_ref, sint_ref = rope_refs[2], rope_refs[3]
            half = HEAD_DIM // 2
            rot = jnp.concatenate([t[half:, :], t[:half, :]], axis=0)
            t = t * cost_ref[...] + rot * sint_ref[...]
        kt_ref[0, sl, :] = t.astype(BF16)
    kt_ref[0, MIX_W:, :] = (kg[MIX_W:2 * MIX_W, :] * kscale).astype(BF16)

    g = kg[2 * MIX_W:, :] + gb_ref[...]
    i_pre = g[0:8, :]
    log_f = _log_sigmoid(g[8:16, :])
    src = lax.broadcasted_iota(jnp.int32, (L, L), 0)
    dst = lax.broadcasted_iota(jnp.int32, (L, L), 1)
    prefix_m = (src <= dst).astype(BF16)
    suffix_m = (src >= dst).astype(BF16)
    is_fwd = lax.broadcasted_iota(jnp.int32, (8, 1), 0) < HEADS
    for ci in range(tokens // L):
        cs = slice(ci * L, (ci + 1) * L)
        lf = log_f[:, cs]
        parts = _split3(jnp.concatenate([lf, lf], axis=0))
        pre = sum(_dot(p, prefix_m) for p in parts)[0:8]
        suf = sum(_dot(p, suffix_m) for p in parts)[0:8]
        cum = jnp.where(is_fwd, pre, suf)
        gr_ref[0, :, cs] = jnp.concatenate([i_pre[:, cs] - cum, cum], axis=0)

    v_ref[0] = _dot(hb, wv_ref[...]).astype(BF16)
    return hb


S_SHAPE = (2, HEADS, HEAD_DIM, HEAD_DIM)
C_SHAPE = (2, HEADS, HEAD_DIM, 2 * HEAD_DIM)
M_SHAPE = (2 * HEADS, HEAD_DIM)


def _chunk_sources(rc_ref, kt_ref, v_ref, gr_ref, cs, d):
    L = SCAN_CHUNK
    pos = lax.broadcasted_iota(jnp.int32, (1, L), 1).astype(F32)
    ones = jnp.ones((L, HEAD_DIM), BF16)
    a = gr_ref[0, 4 * d:4 * d + 4, cs]
    cum = gr_ref[0, 8 + 4 * d:12 + 4 * d, cs]
    edge = cum[:, L - 1:L] if d == 0 else cum[:, 0:1]
    b_last = jnp.broadcast_to(edge, (HEADS, HEAD_DIM))
    amax = jnp.broadcast_to(jnp.max(a, axis=1, keepdims=True), (HEADS, HEAD_DIM))
    w_loc = jnp.exp(a - jnp.concatenate([amax] * (L // HEAD_DIM), axis=1))
    u_ret, u_ml = [], []
    for hd in range(HEADS):
        sl = slice(hd * HEAD_DIM, (hd + 1) * HEAD_DIM)
        ml = slice(MIX_W + hd * HEAD_DIM, MIX_W + (hd + 1) * HEAD_DIM)
        lg = rc_ref[4 * d + hd]
        to_end = jnp.exp(lg * ((L - 1.0) - pos)) if d == 0 else jnp.exp(lg * pos)
        kw = (kt_ref[0, sl, cs].astype(F32) * to_end).astype(BF16)
        u_ret.append(_dot(kw, v_ref[0, cs, sl]))
        kw = (kt_ref[0, ml, cs].astype(F32) * w_loc[hd:hd + 1, :]).astype(BF16)
        u_ml.append(_dot(kw, jnp.concatenate([v_ref[0, cs, ml], ones], axis=1)))
    return amax, b_last, u_ret, u_ml


def _advance_state(rc_ref, s_scr, c_scr, m_scr, d, amax, b_last, u_ret, u_ml, emit):
    rows = slice(HEADS * d, HEADS * (d + 1))
    m_old = m_scr[rows, :]
    m_mid = jnp.maximum(m_old, amax)
    w_old = jnp.exp(m_old - m_mid)
    w_new = jnp.exp(amax - m_mid)
    if emit is not None:
        s_out, c_out, m_out, slot = emit
        m_out[0, slot] = m_old
    m_scr[rows, :] = b_last + m_mid
    for hd in range(HEADS):
        s_prev = s_scr[d, hd]
        c_prev = c_scr[d, hd]
        if emit is not None:
            s_out[0, slot, hd] = s_prev.astype(BF16)
            c_out[0, slot, hd] = c_prev.astype(BF16)
        s_scr[d, hd] = rc_ref[2 * HEADS + HEADS * d + hd] * s_prev + u_ret[hd]
        wo, wn = w_old[hd:hd + 1, :], w_new[hd:hd + 1, :]
        c_scr[d, hd] = (jnp.concatenate([wo, wo], axis=1) * c_prev
                        + jnp.concatenate([wn, wn], axis=1) * u_ml[hd])


def _ctx_kernel(rc_ref, x_ref, gain_ref, sh_ref, sc_ref, wv_ref, wkt_ref, gb_ref,
                s_fin, c_fin, m_fin, v_scr, kt_scr, gr_scr, s_scr, c_scr, m_scr):
    L = SCAN_CHUNK
    _project_kv(x_ref, gain_ref, sh_ref, sc_ref, wv_ref, wkt_ref, gb_ref, None, v_scr, kt_scr, gr_scr)
    s_scr[...] = jnp.zeros_like(s_scr)
    c_scr[...] = jnp.zeros_like(c_scr)
    m_scr[...] = jnp.zeros_like(m_scr)
    n_chunks = x_ref.shape[1] // L
    for d in range(2):
        for ci in (range(n_chunks) if d == 0 else reversed(range(n_chunks))):
            cs = slice(ci * L, (ci + 1) * L)
            _advance_state(rc_ref, s_scr, c_scr, m_scr, d, *_chunk_sources(rc_ref, kt_scr, v_scr, gr_scr, cs, d),
                           None)
    s_fin[0] = s_scr[...]
    c_fin[0] = c_scr[...]
    m_fin[0] = m_scr[...]


def _ctx_states(ret_consts, ctx, gain, shift, scale, wv, wkt, gbias):
    b, n, d = ctx.shape
    per_b = lambda shape: pl.BlockSpec((1,) + shape, lambda i: (i,) + (0,) * len(shape))
    return pl.pallas_call(
        _ctx_kernel,
        grid=(b,),
        in_specs=[pl.BlockSpec(memory_space=pltpu.SMEM),
                  per_b((n, d)), _resident((1, d)), per_b((1, d)), per_b((1, d)),
                  _resident(wv.shape), _resident(wkt.shape), _resident(gbias.shape)],
        out_specs=(per_b(S_SHAPE), per_b(C_SHAPE), per_b(M_SHAPE)),
        out_shape=tuple(jax.ShapeDtypeStruct((b,) + shape, F32) for shape in (S_SHAPE, C_SHAPE, M_SHAPE)),
        scratch_shapes=[pltpu.VMEM((1, n, 2 * MIX_W), BF16),
                        pltpu.VMEM((1, 2 * MIX_W, n), BF16), pltpu.VMEM((1, 16, n), F32),
                        pltpu.VMEM(S_SHAPE, F32), pltpu.VMEM(C_SHAPE, F32), pltpu.VMEM(M_SHAPE, F32)],
        compiler_params=pltpu.CompilerParams(dimension_semantics=("arbitrary",),
                                             vmem_limit_bytes=VMEM_LIMIT),
        name="ctx_states",
    )(ret_consts, ctx, gain, shift, scale, wv, wkt, gbias)


def _proj_scan_kernel(rc_ref, x_ref, gain_ref, sh_ref, sc_ref, wq_ref, wv_ref, wkt_ref, gb_ref,
                      cos_ref, sin_ref, cost_ref, sint_ref, s0_ref, c0_ref, m0_ref,
                      q_ref, v_ref, kt_ref, gr_ref, sf_ref, cf_ref, mf_ref, sb_ref, cb_ref, mb_ref,
                      s_scr, c_scr, m_scr, ub_ret, ub_ml, ub_stat, *, n_tiles, bwd_chunks):
    L = SCAN_CHUNK
    j = pl.program_id(1)
    per_tile = x_ref.shape[1] // L
    n_chunks = n_tiles * per_tile

    @pl.when(j == 0)
    def _():
        s_scr[...] = s0_ref[0]
        c_scr[...] = c0_ref[0]
        m_scr[...] = m0_ref[0]

    @pl.when(j < n_tiles)
    def _():
        rope_refs = (cos_ref, sin_ref, cost_ref, sint_ref)
        hb = _project_kv(x_ref, gain_ref, sh_ref, sc_ref, wv_ref, wkt_ref, gb_ref, rope_refs,
                         v_ref, kt_ref, gr_ref)
        _project_q(hb, wq_ref, rope_refs, q_ref)
        for ci in range(per_tile):
            cs = slice(ci * L, (ci + 1) * L)
            _advance_state(rc_ref, s_scr, c_scr, m_scr, 0, *_chunk_sources(rc_ref, kt_ref, v_ref, gr_ref, cs, 0),
                           (sf_ref, cf_ref, mf_ref, ci))
            amax, b_last, u_ret, u_ml = _chunk_sources(rc_ref, kt_ref, v_ref, gr_ref, cs, 1)
            chunk = j * per_tile + ci
            ub_stat[chunk] = jnp.concatenate([amax, b_last], axis=0)
            for hd in range(HEADS):
                ub_ret[chunk, hd] = u_ret[hd].astype(BF16)
                ub_ml[chunk, hd] = u_ml[hd].astype(BF16)

    @pl.when(j >= n_tiles)
    def _():
        first = n_chunks - 1 - (j - n_tiles) * bwd_chunks
        for i in range(bwd_chunks):
            chunk = first - i
            stat = ub_stat[chunk]
            _advance_state(rc_ref, s_scr, c_scr, m_scr, 1, stat[0:HEADS], stat[HEADS:2 * HEADS],
                           [ub_ret[chunk, hd].astype(F32) for hd in range(HEADS)],
                           [ub_ml[chunk, hd].astype(F32) for hd in range(HEADS)],
                           (sb_ref, cb_ref, mb_ref, bwd_chunks - 1 - i))


def _proj_scan(ret_consts, x, gain, shift, scale, wq, wv, wkt, gbias, rope, init):
    b, n, d = x.shape
    L = SCAN_CHUNK
    t = PROJ_TOKENS
    nt = n // t
    nc = n // L
    per_tile = t // L
    nbb = nc // BWD_CHUNKS
    tile = lambda j: jnp.minimum(j, nt - 1)
    bblk = lambda j: nbb - 1 - jnp.maximum(j - nt, 0)
    tok3 = lambda i, j: (i, tile(j), 0)
    feat3 = lambda i, j: (i, 0, tile(j))
    mod3 = lambda i, j: (i, 0, 0)
    per_b = lambda shape: pl.BlockSpec((1,) + shape, lambda i, j: (i,) + (0,) * len(shape))
    cos, sin, cos_t, sin_t = rope
    in_specs = [pl.BlockSpec(memory_space=pltpu.SMEM),
                pl.BlockSpec((1, t, d), tok3),
                _resident((1, d)),
                pl.BlockSpec((1, 1, d), mod3), pl.BlockSpec((1, 1, d), mod3),
                _resident(wq.shape), _resident(wv.shape), _resident(wkt.shape), _resident(gbias.shape),
                pl.BlockSpec((t, HEAD_DIM), lambda i, j: (tile(j), 0)),
                pl.BlockSpec((t, HEAD_DIM), lambda i, j: (tile(j), 0)),
                pl.BlockSpec((HEAD_DIM, t), lambda i, j: (0, tile(j))),
                pl.BlockSpec((HEAD_DIM, t), lambda i, j: (0, tile(j))),
                per_b(S_SHAPE), per_b(C_SHAPE), per_b(M_SHAPE)]
    out_shape = [jax.ShapeDtypeStruct((b, n, 2 * MIX_W), BF16),
                 jax.ShapeDtypeStruct((b, n, 2 * MIX_W), BF16),
                 jax.ShapeDtypeStruct((b, 2 * MIX_W, n), BF16),
                 jax.ShapeDtypeStruct((b, 16, n), F32)]
    out_specs = [pl.BlockSpec((1, t, 2 * MIX_W), tok3),
                 pl.BlockSpec((1, t, 2 * MIX_W), tok3),
                 pl.BlockSpec((1, 2 * MIX_W, t), feat3),
                 pl.BlockSpec((1, 16, t), feat3)]
    for blk, step in ((per_tile, tile), (BWD_CHUNKS, bblk)):
        out_shape += [jax.ShapeDtypeStruct((b, nc) + S_SHAPE[1:], BF16),
                      jax.ShapeDtypeStruct((b, nc) + C_SHAPE[1:], BF16),
                      jax.ShapeDtypeStruct((b, nc, HEADS, HEAD_DIM), F32)]
        out_specs += [pl.BlockSpec((1, blk) + S_SHAPE[1:], lambda i, j, step=step: (i, step(j), 0, 0, 0)),
                      pl.BlockSpec((1, blk) + C_SHAPE[1:], lambda i, j, step=step: (i, step(j), 0, 0, 0)),
                      pl.BlockSpec((1, blk, HEADS, HEAD_DIM), lambda i, j, step=step: (i, step(j), 0, 0))]
    return pl.pallas_call(
        functools.partial(_proj_scan_kernel, n_tiles=nt, bwd_chunks=BWD_CHUNKS),
        grid=(b, nt + nbb),
        in_specs=in_specs,
        out_specs=tuple(out_specs),
        out_shape=tuple(out_shape),
        scratch_shapes=[pltpu.VMEM(S_SHAPE, F32), pltpu.VMEM(C_SHAPE, F32), pltpu.VMEM(M_SHAPE, F32),
                        pltpu.VMEM((nc,) + S_SHAPE[1:], BF16), pltpu.VMEM((nc,) + C_SHAPE[1:], BF16),
                        pltpu.VMEM((nc,) + M_SHAPE, F32)],
        compiler_params=pltpu.CompilerParams(dimension_semantics=("arbitrary", "arbitrary"),
                                             vmem_limit_bytes=VMEM_LIMIT),
        name="proj_scan",
    )(ret_consts, x, gain, shift, scale, wq, wv, wkt, gbias, cos, sin, cos_t, sin_t, *init)


def _mixer_kernel(rc_ref, x_ref, gain_ref, sh_ref, sc_ref, g1_ref, q_ref, kt_ref, v_ref, gr_ref,
                  sf_ref, sb_ref, cf_ref, cb_ref, mf_ref, mb_ref,
                  wpost_ref, wru_ref, wmu_ref, wout_ref, o_ref, dec_scr, qdec_scr):
    L = SCAN_CHUNK
    row = lax.broadcasted_iota(jnp.int32, (L, L), 0)
    col = lax.broadcasted_iota(jnp.int32, (L, L), 1)

    @pl.when((pl.program_id(0) == 0) & (pl.program_id(1) == 0))
    def _():
        rel = (row - col).astype(F32)
        lpos = lax.broadcasted_iota(jnp.int32, (L, HEAD_DIM), 0).astype(F32)
        for hd in range(HEADS):
            lg_f = rc_ref[hd]
            lg_b = rc_ref[4 + hd]
            dec_scr[hd] = (jnp.where(rel >= 0, jnp.exp(lg_f * jnp.maximum(rel, 0.0)), 0.0)
                           + jnp.where(rel <= 0, jnp.exp(lg_b * jnp.maximum(-rel, 0.0)), 0.0))
            qdec_scr[hd] = jnp.concatenate([jnp.exp(lg_f * (lpos + 1.0)), jnp.exp(lg_b * (L - lpos))], axis=1)

    d_model = x_ref.shape[-1]
    bg_w = d_model // HEADS
    post_w = 2 * HEAD_DIM + 2 * bg_w

    masks = (col <= row, col >= row)
    c_refs, m_refs = (cf_ref, cb_ref), (mf_ref, mb_ref)
    ones = jnp.ones((L, HEAD_DIM), BF16)
    neg_inf = jnp.float32(-jnp.inf)
    n_chunks = x_ref.shape[1] // L
    cum_cols = [gr_ref[0, 8:16, ci * L:(ci + 1) * L].T for ci in range(n_chunks)]
    hb = None
    y_ret, y_ml, bg_ret, bg_ml = [], [], [], []
    for hd in range(HEADS):
        sl = slice(hd * HEAD_DIM, (hd + 1) * HEAD_DIM)
        ml = slice(MIX_W + hd * HEAD_DIM, MIX_W + (hd + 1) * HEAD_DIM)
        ret_rows, ml_rows = [], []
        for ci in range(n_chunks):
            tok = slice(ci * L, (ci + 1) * L)
            qh = q_ref[0, tok, sl]
            p = (_dot(qh, kt_ref[0, sl, tok]) * dec_scr[hd]).astype(BF16)
            qf = qh.astype(F32)
            qw = (jnp.concatenate([qf, qf], axis=1) * qdec_scr[hd]).astype(BF16)
            st = jnp.concatenate([sf_ref[0, ci, hd], sb_ref[0, ci, hd]], axis=0)
            ret_rows.append(_rms(_dot(p, v_ref[0, tok, sl]) + _dot(qw, st)))
            qh = q_ref[0, tok, ml]
            scores = _dot(qh, kt_ref[0, ml, tok])
            v_ext = jnp.concatenate([v_ref[0, tok, ml], ones], axis=1)
            hsum = None
            for d in range(2):
                a_row = gr_ref[0, 4 * d + hd:4 * d + hd + 1, tok]
                cum_col = cum_cols[ci][:, 4 * d + hd:4 * d + hd + 1]
                m_in = m_refs[d][0, ci, hd:hd + 1, :]
                a_masked = jnp.where(masks[d], a_row, neg_inf)
                cmax = jnp.broadcast_to(jnp.max(a_masked, axis=1, keepdims=True), (L, HEAD_DIM))
                m_loc = jnp.maximum(cmax, m_in)
                w = jnp.exp(a_masked - jnp.concatenate([m_loc] * (L // HEAD_DIM), axis=1))
                tot = _dot((scores * w).astype(BF16), v_ext)
                w_inter = jnp.exp(m_in - m_loc)
                tot = tot + jnp.concatenate([w_inter, w_inter], axis=1) * _dot(qh, c_refs[d][0, ci, hd])
                floor = jnp.exp(-(jnp.broadcast_to(cum_col, (L, HEAD_DIM)) + m_loc))
                hd_out = tot[:, :HEAD_DIM] / jnp.maximum(jnp.abs(tot[:, HEAD_DIM:]), floor)
                hsum = hd_out if hsum is None else hsum + hd_out
            ml_rows.append(hsum)
        if hb is None:
            xf = x_ref[0]
            hb = (_rms(xf) * gain_ref[...] * (1.0 + sc_ref[0]) + sh_ref[0]).astype(BF16)
        post = _dot(hb, wpost_ref[:, hd * post_w:(hd + 1) * post_w])
        bg_ret.append(post[:, 2 * HEAD_DIM:2 * HEAD_DIM + bg_w])
        bg_ml.append(post[:, 2 * HEAD_DIM + bg_w:])
        y_ret.append(jnp.concatenate(ret_rows, axis=0) * _silu(post[:, :HEAD_DIM]))
        y_ml.append(_rms(_sigmoid(post[:, HEAD_DIM:2 * HEAD_DIM]) * jnp.concatenate(ml_rows, axis=0)))

    y_ret = jnp.concatenate(y_ret, axis=1).astype(BF16)
    y_ml = jnp.concatenate(y_ml, axis=1).astype(BF16)
    bg_ret = jnp.concatenate(bg_ret, axis=1)
    bg_ml = jnp.concatenate(bg_ml, axis=1)
    merged = _sigmoid(bg_ret) * _dot(y_ret, wru_ref[...]) + _sigmoid(bg_ml) * _dot(y_ml, wmu_ref[...])
    o_ref[0] = xf + g1_ref[0] * _dot(merged.astype(BF16), wout_ref[...])


def _mixer(ret_consts, x, gain, shift, scale, gate1, q, kt, v, gr, chunk_states, wpost, wru, wmu, wout):
    b, n, d = x.shape
    L = SCAN_CHUNK
    t = MIX_TOKENS
    ch = t // L
    sf, sb, cf, cb, mf, mb = chunk_states
    tok3 = lambda i, j: (i, j, 0)
    feat3 = lambda i, j: (i, 0, j)
    mod3 = lambda i, j: (i, 0, 0)
    st5 = lambda i, j: (i, j, 0, 0, 0)
    st4 = lambda i, j: (i, j, 0, 0)
    in_specs = [pl.BlockSpec(memory_space=pltpu.SMEM),
                pl.BlockSpec((1, t, d), tok3),
                _resident((1, d)),
                pl.BlockSpec((1, 1, d), mod3), pl.BlockSpec((1, 1, d), mod3), pl.BlockSpec((1, 1, d), mod3),
                pl.BlockSpec((1, t, 2 * MIX_W), tok3),
                pl.BlockSpec((1, 2 * MIX_W, t), feat3),
                pl.BlockSpec((1, t, 2 * MIX_W), tok3),
                pl.BlockSpec((1, 16, t), feat3),
                pl.BlockSpec((1, ch) + S_SHAPE[1:], st5),
                pl.BlockSpec((1, ch) + S_SHAPE[1:], st5),
                pl.BlockSpec((1, ch) + C_SHAPE[1:], st5),
                pl.BlockSpec((1, ch) + C_SHAPE[1:], st5),
                pl.BlockSpec((1, ch, HEADS, HEAD_DIM), st4),
                pl.BlockSpec((1, ch, HEADS, HEAD_DIM), st4),
                _resident(wpost.shape),
                _resident(wru.shape),
                _resident(wmu.shape),
                _resident(wout.shape)]
    return pl.pallas_call(
        _mixer_kernel,
        grid=(b, n // t),
        in_specs=in_specs,
        out_specs=pl.BlockSpec((1, t, d), tok3),
        out_shape=jax.ShapeDtypeStruct((b, n, d), F32),
        scratch_shapes=[pltpu.VMEM((HEADS, L, L), F32), pltpu.VMEM((HEADS, L, 2 * HEAD_DIM), F32)],
        compiler_params=pltpu.CompilerParams(dimension_semantics=("arbitrary", "arbitrary"),
                                             vmem_limit_bytes=VMEM_LIMIT),
        name="mixer",
    )(ret_consts, x, gain, shift, scale, gate1, q, kt, v, gr, sf, sb, cf, cb, mf, mb,
      wpost, wru, wmu, wout)


def _ffn_kernel(x_ref, gain_ref, sh_ref, sc_ref, g2_ref, fgain_ref, w1_ref, w2_ref, o_ref, *, hidden):
    xf = x_ref[0]
    h = _rms(xf) * gain_ref[...]
    hb = (h * (1.0 + sc_ref[0]) + sh_ref[0]).astype(BF16)
    acc = None
    start = 0
    for width in FFN_SPLITS:
        gate = _dot(hb, w1_ref[:, start:start + width])
        up = _dot(hb, w1_ref[:, hidden + start:hidden + start + width])
        act = (_silu(gate) * up).astype(BF16)
        part = _dot(act, w2_ref[start:start + width, :])
        acc = part if acc is None else acc + part
        start += width
    o_ref[0] = _rms(xf + g2_ref[0] * acc) * fgain_ref[...]


def _ffn(x, gain, shift, scale, gate2, final_gain, w1, w2):
    b, n, d = x.shape
    t = FFN_TOKENS
    hidden = w2.shape[0]
    assert sum(FFN_SPLITS) == hidden
    tok3 = lambda i, j: (i, j, 0)
    mod3 = lambda i, j: (i, 0, 0)
    return pl.pallas_call(
        functools.partial(_ffn_kernel, hidden=hidden),
        grid=(b, n // t),
        in_specs=[pl.BlockSpec((1, t, d), tok3),
                  _resident((1, d)),
                  pl.BlockSpec((1, 1, d), mod3), pl.BlockSpec((1, 1, d), mod3), pl.BlockSpec((1, 1, d), mod3),
                  _resident((1, d)),
                  _resident(w1.shape),
                  _resident(w2.shape)],
        out_specs=pl.BlockSpec((1, t, d), tok3),
        out_shape=jax.ShapeDtypeStruct((b, n, d), F32),
        compiler_params=pltpu.CompilerParams(dimension_semantics=("parallel", "parallel"),
                                             vmem_limit_bytes=VMEM_LIMIT),
        name="ffn",
    )(x, gain, shift, scale, gate2, final_gain, w1, w2)


def _rope_tables(n):
    n_rows = n // GRID_W
    rows = jnp.broadcast_to(jnp.arange(n_rows, dtype=F32)[:, None], (n_rows, GRID_W)).reshape(n)
    cols = jnp.broadcast_to(jnp.arange(GRID_W, dtype=F32)[None, :], (n_rows, GRID_W)).reshape(n)
    n_freq = HEAD_DIM // 4
    inv = ROPE_BASE ** (-jnp.arange(n_freq, dtype=F32) / n_freq)
    ang = jnp.concatenate([rows[:, None] * inv, cols[:, None] * inv], axis=-1)
    cos, sin = jnp.cos(ang), jnp.sin(ang)
    cos2 = jnp.concatenate([cos, cos], axis=-1)
    sin2 = jnp.concatenate([-sin, sin], axis=-1)
    return cos2, sin2, cos2.T, sin2.T


def kernel(x, c, ctx, c_ctx, w_ada, b_ada, norm1_gain, norm2_gain, w_in, mlstm_gate_bias, ret_decay_logit,
           w_ret_up, w_ml_up, w_out, w_ffn_in, w_ffn_out, final_gain):
    assert w_ada.shape[0] == 1, "single-layer block"
    b, n, d = x.shape
    assert n % PROJ_TOKENS == 0 and n % FFN_TOKENS == 0 and n % MIX_TOKENS == 0
    assert PROJ_TOKENS % SCAN_CHUNK == 0 and MIX_TOKENS % SCAN_CHUNK == 0 and ctx.shape[1] % SCAN_CHUNK == 0
    assert (n // SCAN_CHUNK) % BWD_CHUNKS == 0

    rows = -(-(b + 1) // 16) * 16
    cc = jnp.concatenate([c, c_ctx[None, :], jnp.zeros((rows - b - 1, d), F32)], axis=0)
    mod = _adaln(cc, w_ada[0], b_ada[0])
    sh1, sc1, g1, sh2, sc2, g2 = (mod[:b, i * d:(i + 1) * d].reshape(b, 1, d) for i in range(6))
    csh1 = jnp.broadcast_to(mod[b, 0:d].reshape(1, 1, d), (b, 1, d))
    csc1 = jnp.broadcast_to(mod[b, d:2 * d].reshape(1, 1, d), (b, 1, d))

    w = w_in[0]
    o = [0]
    for width in (MIX_W, MIX_W, MIX_W, MIX_W, MIX_W, MIX_W, MIX_W, MIX_W, 4 * HEADS, d, d):
        o.append(o[-1] + width)
    col = lambda i: w[:, o[i]:o[i + 1]]
    gates = col(8).reshape(d, 4, HEADS)
    gates = jnp.concatenate([gates[:, 0], gates[:, 2], gates[:, 1], gates[:, 3]], axis=1)
    gb = mlstm_gate_bias[0]
    gbias = jnp.concatenate([gb[0], gb[2], gb[1], gb[3]]).reshape(4 * HEADS, 1).astype(F32)
    wq = jnp.concatenate([col(0), col(4)], axis=1).astype(BF16)
    wv = jnp.concatenate([col(2), col(6)], axis=1).astype(BF16)
    wkt = jnp.concatenate([col(1), col(5), gates], axis=1).T.astype(BF16)
    bg_w = d // HEADS
    wpost = jnp.concatenate(
        [piece for hd in range(HEADS)
         for piece in (col(3)[:, hd * HEAD_DIM:(hd + 1) * HEAD_DIM], col(7)[:, hd * HEAD_DIM:(hd + 1) * HEAD_DIM],
                       col(9)[:, hd * bg_w:(hd + 1) * bg_w], col(10)[:, hd * bg_w:(hd + 1) * bg_w])],
        axis=1).astype(BF16)

    log_gamma = jax.nn.log_sigmoid(ret_decay_logit[0].astype(F32)).reshape(2 * HEADS)
    ret_consts = jnp.concatenate([log_gamma, jnp.exp(log_gamma * SCAN_CHUNK)])

    gain1 = norm1_gain[0].reshape(1, d)
    ctx_final = _ctx_states(ret_consts, ctx, gain1, csh1, csc1, wv, wkt, gbias)
    q_x, v_x, kt_x, gr_x, sf, cf, mf, sb, cb, mb = _proj_scan(
        ret_consts, x, gain1, sh1, sc1, wq, wv, wkt, gbias, _rope_tables(n), ctx_final)
    x1 = _mixer(ret_consts, x, gain1, sh1, sc1, g1, q_x, kt_x, v_x, gr_x, (sf, sb, cf, cb, mf, mb),
                wpost, w_ret_up[0].astype(BF16), w_ml_up[0].astype(BF16), w_out[0].astype(BF16))
    return _ffn(x1, norm2_gain[0].reshape(1, d), sh2, sc2, g2, final_gain.reshape(1, d),
                w_ffn_in[0].astype(BF16), w_ffn_out[0].astype(BF16))
```

```python
import functools

import jax
import jax.numpy as jnp
from jax import lax
from jax.experimental import pallas as pl
from jax.experimental.pallas import tpu as pltpu

HEADS = 4
HEAD_DIM = 128
MIX_W = HEADS * HEAD_DIM
GRID_W = 64
ROPE_BASE = 10000.0
EPS = 1e-6
LOG2E = 1.4426950408889634
SCAN_CHUNK = 256
PROJ_TOKENS = 512
BWD_CHUNKS = 16
MIX_TOKENS = 512
FFN_TOKENS = 1024
FFN_SPLITS = (1536, 1280)
VMEM_LIMIT = 56 * 1024 * 1024

F32 = jnp.float32
BF16 = jnp.bfloat16


def _resident(shape):
    return pl.BlockSpec(shape, lambda *_: (0,) * len(shape), pipeline_mode=pl.Buffered(1))


def _dot(a, b):
    return jnp.dot(a, b, preferred_element_type=F32)


def _dot_nt(a, b):
    return lax.dot_general(a, b, (((1,), (1,)), ((), ())), preferred_element_type=F32)


def _sigmoid(t):
    return 0.5 * jnp.tanh(0.5 * t) + 0.5


def _silu(t):
    return t * _sigmoid(t)


def _log_sigmoid(t):
    return jnp.minimum(t, 0.0) - jnp.log1p(jnp.exp(-jnp.abs(t)))


def _rms(t):
    return t * lax.rsqrt(jnp.mean(t * t, axis=-1, keepdims=True) + EPS)


def _split3(t):
    hi = t.astype(BF16)
    r1 = t - hi.astype(F32)
    mid = r1.astype(BF16)
    lo = (r1 - mid.astype(F32)).astype(BF16)
    return hi, mid, lo


def _adaln_kernel(c_ref, w_ref, b_ref, o_ref):
    s = _silu(c_ref[...]).astype(BF16)
    o_ref[...] = _dot(s, w_ref[...].astype(BF16)) + b_ref[...]


def _adaln(cc, w, b):
    rows, d = cc.shape
    cols = w.shape[1]
    blk = 1536
    return pl.pallas_call(
        _adaln_kernel,
        grid=(cols // blk,),
        in_specs=[pl.BlockSpec((rows, d), lambda j: (0, 0)),
                  pl.BlockSpec((d, blk), lambda j: (0, j)),
                  pl.BlockSpec((1, blk), lambda j: (0, j))],
        out_specs=pl.BlockSpec((rows, blk), lambda j: (0, j)),
        out_shape=jax.ShapeDtypeStruct((rows, cols), F32),
        compiler_params=pltpu.CompilerParams(dimension_semantics=("parallel",),
                                             vmem_limit_bytes=VMEM_LIMIT),
        name="adaln",
    )(cc, w, b.reshape(1, cols))


def _project_q(hb, wq_ref, rope_refs, q_ref):
    q = _dot(hb, wq_ref[...])
    for hd in range(HEADS):
        sl = slice(hd * HEAD_DIM, (hd + 1) * HEAD_DIM)
        t = q[:, sl]
        if rope_refs is not None:
            cos_ref, sin_ref = rope_refs[0], rope_refs[1]
            t = t * cos_ref[...] + pltpu.roll(t, HEAD_DIM // 2, 1) * sin_ref[...]
        q_ref[0, :, sl] = t.astype(BF16)
    q_ref[0, :, MIX_W:] = q[:, MIX_W:].astype(BF16)


def _project_kv(x_ref, gain_ref, sh_ref, sc_ref, wv_ref, wkt_ref, gb_ref, rope_refs, v_ref, kt_ref, gr_ref):
    L = SCAN_CHUNK
    tokens = x_ref.shape[1]
    h = _rms(x_ref[0]) * gain_ref[...]
    h = h * (1.0 + sc_ref[0]) + sh_ref[0]
    hb = h.astype(BF16)

    kg = _dot_nt(wkt_ref[...], hb)
    kscale = HEAD_DIM ** -0.5
    for hd in range(HEADS):
        sl = slice(hd * HEAD_DIM, (hd + 1) * HEAD_DIM)
        t = kg[sl, :] * kscale
        if rope_refs is not None:
            cost_ref, sint_ref = rope_refs[2], rope_refs[3]
            half = HEAD_DIM // 2
            rot = jnp.concatenate([t[half:, :], t[:half, :]], axis=0)
            t = t * cost_ref[...] + rot * sint_ref[...]
        for ci in range(tokens // L):
            kt_ref[0, ci, sl, :] = t[:, ci * L:(ci + 1) * L].astype(BF16)
    for ci in range(tokens // L):
        kt_ref[0, ci, MIX_W:, :] = (kg[MIX_W:2 * MIX_W, ci * L:(ci + 1) * L] * kscale).astype(BF16)

    g = kg[2 * MIX_W:, :] + gb_ref[...]
    i_pre = g[0:8, :]
    log_f = _log_sigmoid(g[8:16, :])
    src = lax.broadcasted_iota(jnp.int32, (L, L), 0)
    dst = lax.broadcasted_iota(jnp.int32, (L, L), 1)
    prefix_m = (src <= dst).astype(BF16)
    suffix_m = (src >= dst).astype(BF16)
    is_fwd = lax.broadcasted_iota(jnp.int32, (8, 1), 0) < HEADS
    for ci in range(tokens // L):
        cs = slice(ci * L, (ci + 1) * L)
        lf = log_f[:, cs]
        parts = _split3(jnp.concatenate([lf, lf], axis=0))
        pre = sum(_dot(p, prefix_m) for p in parts)[0:8]
        suf = sum(_dot(p, suffix_m) for p in parts)[0:8]
        cum = jnp.where(is_fwd, pre, suf)
        gr_ref[0, ci] = jnp.concatenate([i_pre[:, cs] - cum, cum], axis=0)

    v_ref[0] = _dot(hb, wv_ref[...]).astype(BF16)
    return hb


S_SHAPE = (2, HEADS, HEAD_DIM, HEAD_DIM)
C_SHAPE = (2, HEADS, HEAD_DIM, 2 * HEAD_DIM)
M_SHAPE = (2 * HEADS, HEAD_DIM)


def _chunk_sources(rc_ref, kt_ref, v_ref, gr_ref, ci, d):
    L = SCAN_CHUNK
    cs = slice(ci * L, (ci + 1) * L)
    pos = lax.broadcasted_iota(jnp.int32, (1, L), 1).astype(F32)
    ones = jnp.ones((L, HEAD_DIM), BF16)
    a = gr_ref[0, ci, 4 * d:4 * d + 4, :]
    cum = gr_ref[0, ci, 8 + 4 * d:12 + 4 * d, :]
    edge = cum[:, L - 1:L] if d == 0 else cum[:, 0:1]
    b_last = jnp.broadcast_to(edge, (HEADS, HEAD_DIM))
    amax = jnp.broadcast_to(jnp.max(a, axis=1, keepdims=True), (HEADS, HEAD_DIM))
    w_loc = jnp.exp(a - jnp.concatenate([amax] * (L // HEAD_DIM), axis=1))
    u_ret, u_ml = [], []
    for hd in range(HEADS):
        sl = slice(hd * HEAD_DIM, (hd + 1) * HEAD_DIM)
        ml = slice(MIX_W + hd * HEAD_DIM, MIX_W + (hd + 1) * HEAD_DIM)
        lg = rc_ref[4 * d + hd]
        to_end = jnp.exp(lg * ((L - 1.0) - pos)) if d == 0 else jnp.exp(lg * pos)
        kw = (kt_ref[0, ci, sl, :].astype(F32) * to_end).astype(BF16)
        u_ret.append(_dot(kw, v_ref[0, cs, sl]))
        kw = (kt_ref[0, ci, ml, :].astype(F32) * w_loc[hd:hd + 1, :]).astype(BF16)
        u_ml.append(_dot(kw, jnp.concatenate([v_ref[0, cs, ml], ones], axis=1)))
    return amax, b_last, u_ret, u_ml


def _advance_state(rc_ref, s_scr, c_scr, m_scr, d, amax, b_last, u_ret, u_ml, emit):
    rows = slice(HEADS * d, HEADS * (d + 1))
    m_old = m_scr[rows, :]
    m_mid = jnp.maximum(m_old, amax)
    w_old = jnp.exp(m_old - m_mid)
    w_new = jnp.exp(amax - m_mid)
    if emit is not None:
        s_out, c_out, m_out, slot = emit
        m_out[0, slot] = m_old
    m_scr[rows, :] = b_last + m_mid
    for hd in range(HEADS):
        s_prev = s_scr[d, hd]
        c_prev = c_scr[d, hd]
        if emit is not None:
            s_out[0, slot, hd] = s_prev.astype(BF16)
            c_out[0, slot, hd] = c_prev.astype(BF16)
        s_scr[d, hd] = rc_ref[2 * HEADS + HEADS * d + hd] * s_prev + u_ret[hd]
        wo, wn = w_old[hd:hd + 1, :], w_new[hd:hd + 1, :]
        c_scr[d, hd] = (jnp.concatenate([wo, wo], axis=1) * c_prev
                        + jnp.concatenate([wn, wn], axis=1) * u_ml[hd])


def _ctx_kernel(rc_ref, x_ref, gain_ref, sh_ref, sc_ref, wv_ref, wkt_ref, gb_ref,
                s_fin, c_fin, m_fin, v_scr, kt_scr, gr_scr, s_scr, c_scr, m_scr):
    L = SCAN_CHUNK
    _project_kv(x_ref, gain_ref, sh_ref, sc_ref, wv_ref, wkt_ref, gb_ref, None, v_scr, kt_scr, gr_scr)
    s_scr[...] = jnp.zeros_like(s_scr)
    c_scr[...] = jnp.zeros_like(c_scr)
    m_scr[...] = jnp.zeros_like(m_scr)
    n_chunks = x_ref.shape[1] // L
    for d in range(2):
        for ci in (range(n_chunks) if d == 0 else reversed(range(n_chunks))):
            _advance_state(rc_ref, s_scr, c_scr, m_scr, d, *_chunk_sources(rc_ref, kt_scr, v_scr, gr_scr, ci, d),
                           None)
    s_fin[0] = s_scr[...]
    c_fin[0] = c_scr[...]
    m_fin[0] = m_scr[...]


def _ctx_states(ret_consts, ctx, gain, shift, scale, wv, wkt, gbias):
    b, n, d = ctx.shape
    per_b = lambda shape: pl.BlockSpec((1,) + shape, lambda i: (i,) + (0,) * len(shape))
    return pl.pallas_call(
        _ctx_kernel,
        grid=(b,),
        in_specs=[pl.BlockSpec(memory_space=pltpu.SMEM),
                  per_b((n, d)), _resident((1, d)), per_b((1, d)), per_b((1, d)),
                  _resident(wv.shape), _resident(wkt.shape), _resident(gbias.shape)],
        out_specs=(per_b(S_SHAPE), per_b(C_SHAPE), per_b(M_SHAPE)),
        out_shape=tuple(jax.ShapeDtypeStruct((b,) + shape, F32) for shape in (S_SHAPE, C_SHAPE, M_SHAPE)),
        scratch_shapes=[pltpu.VMEM((1, n, 2 * MIX_W), BF16),
                        pltpu.VMEM((1, n // SCAN_CHUNK, 2 * MIX_W, SCAN_CHUNK), BF16),
                        pltpu.VMEM((1, n // SCAN_CHUNK, 16, SCAN_CHUNK), F32),
                        pltpu.VMEM(S_SHAPE, F32), pltpu.VMEM(C_SHAPE, F32), pltpu.VMEM(M_SHAPE, F32)],
        compiler_params=pltpu.CompilerParams(dimension_semantics=("arbitrary",),
                                             vmem_limit_bytes=VMEM_LIMIT),
        name="ctx_states",
    )(ret_consts, ctx, gain, shift, scale, wv, wkt, gbias)


def _proj_scan_kernel(rc_ref, x_ref, gain_ref, sh_ref, sc_ref, wq_ref, wv_ref, wkt_ref, gb_ref,
                      cos_ref, sin_ref, cost_ref, sint_ref, s0_ref, c0_ref, m0_ref,
                      q_ref, v_ref, kt_ref, gr_ref, sf_ref, cf_ref, mf_ref, sb_ref, cb_ref, mb_ref,
                      s_scr, c_scr, m_scr, ub_ret, ub_ml, ub_stat, *, n_tiles, bwd_chunks):
    L = SCAN_CHUNK
    j = pl.program_id(1)
    per_tile = x_ref.shape[1] // L
    n_chunks = n_tiles * per_tile

    @pl.when(j == 0)
    def _():
        s_scr[...] = s0_ref[0]
        c_scr[...] = c0_ref[0]
        m_scr[...] = m0_ref[0]

    @pl.when(j < n_tiles)
    def _():
        rope_refs = (cos_ref, sin_ref, cost_ref, sint_ref)
        hb = _project_kv(x_ref, gain_ref, sh_ref, sc_ref, wv_ref, wkt_ref, gb_ref, rope_refs,
                         v_ref, kt_ref, gr_ref)
        _project_q(hb, wq_ref, rope_refs, q_ref)
        for ci in range(per_tile):
            _advance_state(rc_ref, s_scr, c_scr, m_scr, 0, *_chunk_sources(rc_ref, kt_ref, v_ref, gr_ref, ci, 0),
                           (sf_ref, cf_ref, mf_ref, ci))
            amax, b_last, u_ret, u_ml = _chunk_sources(rc_ref, kt_ref, v_ref, gr_ref, ci, 1)
            chunk = j * per_tile + ci
            ub_stat[chunk] = jnp.concatenate([amax, b_last], axis=0)
            for hd in range(HEADS):
                ub_ret[chunk, hd] = u_ret[hd].astype(BF16)
                ub_ml[chunk, hd] = u_ml[hd].astype(BF16)

    @pl.when(j >= n_tiles)
    def _():
        first = n_chunks - 1 - (j - n_tiles) * bwd_chunks
        for i in range(bwd_chunks):
            chunk = first - i
            stat = ub_stat[chunk]
            _advance_state(rc_ref, s_scr, c_scr, m_scr, 1, stat[0:HEADS], stat[HEADS:2 * HEADS],
                           [ub_ret[chunk, hd].astype(F32) for hd in range(HEADS)],
                           [ub_ml[chunk, hd].astype(F32) for hd in range(HEADS)],
                           (sb_ref, cb_ref, mb_ref, bwd_chunks - 1 - i))


def _proj_scan(ret_consts, x, gain, shift, scale, wq, wv, wkt, gbias, rope, init):
    b, n, d = x.shape
    L = SCAN_CHUNK
    t = PROJ_TOKENS
    nt = n // t
    nc = n // L
    per_tile = t // L
    nbb = nc // BWD_CHUNKS
    tile = lambda j: jnp.minimum(j, nt - 1)
    bblk = lambda j: nbb - 1 - jnp.maximum(j - nt, 0)
    tok3 = lambda i, j: (i, tile(j), 0)
    chunk4 = lambda i, j: (i, tile(j), 0, 0)
    mod3 = lambda i, j: (i, 0, 0)
    per_b = lambda shape: pl.BlockSpec((1,) + shape, lambda i, j: (i,) + (0,) * len(shape))
    cos, sin, cos_t, sin_t = rope
    in_specs = [pl.BlockSpec(memory_space=pltpu.SMEM),
                pl.BlockSpec((1, t, d), tok3),
                _resident((1, d)),
                pl.BlockSpec((1, 1, d), mod3), pl.BlockSpec((1, 1, d), mod3),
                _resident(wq.shape), _resident(wv.shape), _resident(wkt.shape), _resident(gbias.shape),
                pl.BlockSpec((t, HEAD_DIM), lambda i, j: (tile(j), 0)),
                pl.BlockSpec((t, HEAD_DIM), lambda i, j: (tile(j), 0)),
                pl.BlockSpec((HEAD_DIM, t), lambda i, j: (0, tile(j))),
                pl.BlockSpec((HEAD_DIM, t), lambda i, j: (0, tile(j))),
                per_b(S_SHAPE), per_b(C_SHAPE), per_b(M_SHAPE)]
    out_shape = [jax.ShapeDtypeStruct((b, n, 2 * MIX_W), BF16),
                 jax.ShapeDtypeStruct((b, n, 2 * MIX_W), BF16),
                 jax.ShapeDtypeStruct((b, nc, 2 * MIX_W, L), BF16),
                 jax.ShapeDtypeStruct((b, nc, 16, L), F32)]
    out_specs = [pl.BlockSpec((1, t, 2 * MIX_W), tok3),
                 pl.BlockSpec((1, t, 2 * MIX_W), tok3),
                 pl.BlockSpec((1, per_tile, 2 * MIX_W, L), chunk4),
                 pl.BlockSpec((1, per_tile, 16, L), chunk4)]
    for blk, step in ((per_tile, tile), (BWD_CHUNKS, bblk)):
        out_shape += [jax.ShapeDtypeStruct((b, nc) + S_SHAPE[1:], BF16),
                      jax.ShapeDtypeStruct((b, nc) + C_SHAPE[1:], BF16),
                      jax.ShapeDtypeStruct((b, nc, HEADS, HEAD_DIM), F32)]
        out_specs += [pl.BlockSpec((1, blk) + S_SHAPE[1:], lambda i, j, step=step: (i, step(j), 0, 0, 0)),
                      pl.BlockSpec((1, blk) + C_SHAPE[1:], lambda i, j, step=step: (i, step(j), 0, 0, 0)),
                      pl.BlockSpec((1, blk, HEADS, HEAD_DIM), lambda i, j, step=step: (i, step(j), 0, 0))]
    return pl.pallas_call(
        functools.partial(_proj_scan_kernel, n_tiles=nt, bwd_chunks=BWD_CHUNKS),
        grid=(b, nt + nbb),
        in_specs=in_specs,
        out_specs=tuple(out_specs),
        out_shape=tuple(out_shape),
        scratch_shapes=[pltpu.VMEM(S_SHAPE, F32), pltpu.VMEM(C_SHAPE, F32), pltpu.VMEM(M_SHAPE, F32),
                        pltpu.VMEM((nc,) + S_SHAPE[1:], BF16), pltpu.VMEM((nc,) + C_SHAPE[1:], BF16),
                        pltpu.VMEM((nc,) + M_SHAPE, F32)],
        compiler_params=pltpu.CompilerParams(dimension_semantics=("arbitrary", "arbitrary"),
                                             vmem_limit_bytes=VMEM_LIMIT),
        name="proj_scan",
    )(ret_consts, x, gain, shift, scale, wq, wv, wkt, gbias, cos, sin, cos_t, sin_t, *init)


def _mixer_kernel(rc_ref, x_ref, gain_ref, sh_ref, sc_ref, g1_ref, q_ref, kt_ref, v_ref, gr_ref,
                  sf_ref, sb_ref, cf_ref, cb_ref, mf_ref, mb_ref,
                  wpost_ref, wru_ref, wmu_ref, wout_ref, o_ref, dec_scr, qdec_scr):
    L = SCAN_CHUNK
    row = lax.broadcasted_iota(jnp.int32, (L, L), 0)
    col = lax.broadcasted_iota(jnp.int32, (L, L), 1)

    @pl.when((pl.program_id(0) == 0) & (pl.program_id(1) == 0))
    def _():
        rel = (row - col).astype(F32)
        lpos = lax.broadcasted_iota(jnp.int32, (L, HEAD_DIM), 0).astype(F32)
        for hd in range(HEADS):
            lg_f = rc_ref[hd]
            lg_b = rc_ref[4 + hd]
            dec_scr[hd] = (jnp.where(rel >= 0, jnp.exp(lg_f * jnp.maximum(rel, 0.0)), 0.0)
                           + jnp.where(rel <= 0, jnp.exp(lg_b * jnp.maximum(-rel, 0.0)), 0.0))
            qdec_scr[hd] = jnp.concatenate([jnp.exp(lg_f * (lpos + 1.0)), jnp.exp(lg_b * (L - lpos))], axis=1)

    d_model = x_ref.shape[-1]
    bg_w = d_model // HEADS
    post_w = 2 * HEAD_DIM + 2 * bg_w

    masks = (col <= row, col >= row)
    c_refs, m_refs = (cf_ref, cb_ref), (mf_ref, mb_ref)
    ones = jnp.ones((L, HEAD_DIM), BF16)
    neg_inf = jnp.float32(-jnp.inf)
    n_chunks = x_ref.shape[1] // L
    cum_cols = [(gr_ref[0, ci, 8:16, :] * LOG2E).T for ci in range(n_chunks)]
    hb = None
    y_ret, y_ml, bg_ret, bg_ml = [], [], [], []
    for hd in range(HEADS):
        sl = slice(hd * HEAD_DIM, (hd + 1) * HEAD_DIM)
        ml = slice(MIX_W + hd * HEAD_DIM, MIX_W + (hd + 1) * HEAD_DIM)
        ret_rows, ml_rows = [], []
        for ci in range(n_chunks):
            tok = slice(ci * L, (ci + 1) * L)
            qh = q_ref[0, tok, sl]
            p = (_dot(qh, kt_ref[0, ci, sl, :]) * dec_scr[hd]).astype(BF16)
            qf = qh.astype(F32)
            qw = (jnp.concatenate([qf, qf], axis=1) * qdec_scr[hd]).astype(BF16)
            st = jnp.concatenate([sf_ref[0, ci, hd], sb_ref[0, ci, hd]], axis=0)
            ret_rows.append(_rms(_dot(p, v_ref[0, tok, sl]) + _dot(qw, st)))
            qh = q_ref[0, tok, ml]
            scores = _dot(qh, kt_ref[0, ci, ml, :])
            v_ext = jnp.concatenate([v_ref[0, tok, ml], ones], axis=1)
            hsum = None
            for d in range(2):
                a_row = gr_ref[0, ci, 4 * d + hd:4 * d + hd + 1, :] * LOG2E
                cum_col = cum_cols[ci][:, 4 * d + hd:4 * d + hd + 1]
                m_in = m_refs[d][0, ci, hd:hd + 1, :] * LOG2E
                a_masked = jnp.where(masks[d], a_row, neg_inf)
                cmax = jnp.broadcast_to(jnp.max(a_masked, axis=1, keepdims=True), (L, HEAD_DIM))
                m_loc = jnp.maximum(cmax, m_in)
                w = jnp.exp2(a_masked - jnp.concatenate([m_loc] * (L // HEAD_DIM), axis=1))
                tot = _dot((scores * w).astype(BF16), v_ext)
                w_inter = jnp.exp2(m_in - m_loc)
                tot = tot + jnp.concatenate([w_inter, w_inter], axis=1) * _dot(qh, c_refs[d][0, ci, hd])
                floor = jnp.exp2(-(jnp.broadcast_to(cum_col, (L, HEAD_DIM)) + m_loc))
                hd_out = tot[:, :HEAD_DIM] / jnp.maximum(jnp.abs(tot[:, HEAD_DIM:]), floor)
                hsum = hd_out if hsum is None else hsum + hd_out
            ml_rows.append(hsum)
        if hb is None:
            xf = x_ref[0]
            hb = (_rms(xf) * gain_ref[...] * (1.0 + sc_ref[0]) + sh_ref[0]).astype(BF16)
        post = _dot(hb, wpost_ref[:, hd * post_w:(hd + 1) * post_w])
        bg_ret.append(post[:, 2 * HEAD_DIM:2 * HEAD_DIM + bg_w])
        bg_ml.append(post[:, 2 * HEAD_DIM + bg_w:])
        y_ret.append(jnp.concatenate(ret_rows, axis=0) * _silu(post[:, :HEAD_DIM]))
        y_ml.append(_rms(_sigmoid(post[:, HEAD_DIM:2 * HEAD_DIM]) * jnp.concatenate(ml_rows, axis=0)))

    y_ret = jnp.concatenate(y_ret, axis=1).astype(BF16)
    y_ml = jnp.concatenate(y_ml, axis=1).astype(BF16)
    bg_ret = jnp.concatenate(bg_ret, axis=1)
    bg_ml = jnp.concatenate(bg_ml, axis=1)
    merged = _sigmoid(bg_ret) * _dot(y_ret, wru_ref[...]) + _sigmoid(bg_ml) * _dot(y_ml, wmu_ref[...])
    o_ref[0] = xf + g1_ref[0] * _dot(merged.astype(BF16), wout_ref[...])


def _mixer(ret_consts, x, gain, shift, scale, gate1, q, kt, v, gr, chunk_states, wpost, wru, wmu, wout):
    b, n, d = x.shape
    L = SCAN_CHUNK
    t = MIX_TOKENS
    ch = t // L
    sf, sb, cf, cb, mf, mb = chunk_states
    tok3 = lambda i, j: (i, j, 0)
    mod3 = lambda i, j: (i, 0, 0)
    st5 = lambda i, j: (i, j, 0, 0, 0)
    st4 = lambda i, j: (i, j, 0, 0)
    in_specs = [pl.BlockSpec(memory_space=pltpu.SMEM),
                pl.BlockSpec((1, t, d), tok3),
                _resident((1, d)),
                pl.BlockSpec((1, 1, d), mod3), pl.BlockSpec((1, 1, d), mod3), pl.BlockSpec((1, 1, d), mod3),
                pl.BlockSpec((1, t, 2 * MIX_W), tok3),
                pl.BlockSpec((1, ch, 2 * MIX_W, L), st4),
                pl.BlockSpec((1, t, 2 * MIX_W), tok3),
                pl.BlockSpec((1, ch, 16, L), st4),
                pl.BlockSpec((1, ch) + S_SHAPE[1:], st5),
                pl.BlockSpec((1, ch) + S_SHAPE[1:], st5),
                pl.BlockSpec((1, ch) + C_SHAPE[1:], st5),
                pl.BlockSpec((1, ch) + C_SHAPE[1:], st5),
                pl.BlockSpec((1, ch, HEADS, HEAD_DIM), st4),
                pl.BlockSpec((1, ch, HEADS, HEAD_DIM), st4),
                _resident(wpost.shape),
                _resident(wru.shape),
                _resident(wmu.shape),
                _resident(wout.shape)]
    return pl.pallas_call(
        _mixer_kernel,
        grid=(b, n // t),
        in_specs=in_specs,
        out_specs=pl.BlockSpec((1, t, d), tok3),
        out_shape=jax.ShapeDtypeStruct((b, n, d), F32),
        scratch_shapes=[pltpu.VMEM((HEADS, L, L), F32), pltpu.VMEM((HEADS, L, 2 * HEAD_DIM), F32)],
        compiler_params=pltpu.CompilerParams(dimension_semantics=("arbitrary", "arbitrary"),
                                             vmem_limit_bytes=VMEM_LIMIT),
        name="mixer",
    )(ret_consts, x, gain, shift, scale, gate1, q, kt, v, gr, sf, sb, cf, cb, mf, mb,
      wpost, wru, wmu, wout)


def _ffn_kernel(x_ref, gain_ref, sh_ref, sc_ref, g2_ref, fgain_ref, w1_ref, w2_ref, o_ref, *, hidden):
    xf = x_ref[0]
    h = _rms(xf) * gain_ref[...]
    hb = (h * (1.0 + sc_ref[0]) + sh_ref[0]).astype(BF16)
    acc = None
    start = 0
    for width in FFN_SPLITS:
        gate = _dot(hb, w1_ref[:, start:start + width])
        up = _dot(hb, w1_ref[:, hidden + start:hidden + start + width])
        act = (_silu(gate) * up).astype(BF16)
        part = _dot(act, w2_ref[start:start + width, :])
        acc = part if acc is None else acc + part
        start += width
    o_ref[0] = _rms(xf + g2_ref[0] * acc) * fgain_ref[...]


def _ffn(x, gain, shift, scale, gate2, final_gain, w1, w2):
    b, n, d = x.shape
    t = FFN_TOKENS
    hidden = w2.shape[0]
    assert sum(FFN_SPLITS) == hidden
    tok3 = lambda i, j: (i, j, 0)
    mod3 = lambda i, j: (i, 0, 0)
    return pl.pallas_call(
        functools.partial(_ffn_kernel, hidden=hidden),
        grid=(b, n // t),
        in_specs=[pl.BlockSpec((1, t, d), tok3),
                  _resident((1, d)),
                  pl.BlockSpec((1, 1, d), mod3), pl.BlockSpec((1, 1, d), mod3), pl.BlockSpec((1, 1, d), mod3),
                  _resident((1, d)),
                  _resident(w1.shape),
                  _resident(w2.shape)],
        out_specs=pl.BlockSpec((1, t, d), tok3),
        out_shape=jax.ShapeDtypeStruct((b, n, d), F32),
        compiler_params=pltpu.CompilerParams(dimension_semantics=("parallel", "parallel"),
                                             vmem_limit_bytes=VMEM_LIMIT),
        name="ffn",
    )(x, gain, shift, scale, gate2, final_gain, w1, w2)


def _rope_tables(n):
    n_rows = n // GRID_W
    rows = jnp.broadcast_to(jnp.arange(n_rows, dtype=F32)[:, None], (n_rows, GRID_W)).reshape(n)
    cols = jnp.broadcast_to(jnp.arange(GRID_W, dtype=F32)[None, :], (n_rows, GRID_W)).reshape(n)
    n_freq = HEAD_DIM // 4
    inv = ROPE_BASE ** (-jnp.arange(n_freq, dtype=F32) / n_freq)
    ang = jnp.concatenate([rows[:, None] * inv, cols[:, None] * inv], axis=-1)
    cos, sin = jnp.cos(ang), jnp.sin(ang)
    cos2 = jnp.concatenate([cos, cos], axis=-1)
    sin2 = jnp.concatenate([-sin, sin], axis=-1)
    return cos2, sin2, cos2.T, sin2.T


def kernel(x, c, ctx, c_ctx, w_ada, b_ada, norm1_gain, norm2_gain, w_in, mlstm_gate_bias, ret_decay_logit,
           w_ret_up, w_ml_up, w_out, w_ffn_in, w_ffn_out, final_gain):
    assert w_ada.shape[0] == 1, "single-layer block"
    b, n, d = x.shape
    assert n % PROJ_TOKENS == 0 and n % FFN_TOKENS == 0 and n % MIX_TOKENS == 0
    assert PROJ_TOKENS % SCAN_CHUNK == 0 and MIX_TOKENS % SCAN_CHUNK == 0 and ctx.shape[1] % SCAN_CHUNK == 0
    assert (n // SCAN_CHUNK) % BWD_CHUNKS == 0

    rows = -(-(b + 1) // 16) * 16
    cc = jnp.concatenate([c, c_ctx[None, :], jnp.zeros((rows - b - 1, d), F32)], axis=0)
    mod = _adaln(cc, w_ada[0], b_ada[0])
    sh1, sc1, g1, sh2, sc2, g2 = (mod[:b, i * d:(i + 1) * d].reshape(b, 1, d) for i in range(6))
    csh1 = jnp.broadcast_to(mod[b, 0:d].reshape(1, 1, d), (b, 1, d))
    csc1 = jnp.broadcast_to(mod[b, d:2 * d].reshape(1, 1, d), (b, 1, d))

    w = w_in[0]
    o = [0]
    for width in (MIX_W, MIX_W, MIX_W, MIX_W, MIX_W, MIX_W, MIX_W, MIX_W, 4 * HEADS, d, d):
        o.append(o[-1] + width)
    col = lambda i: w[:, o[i]:o[i + 1]]
    gates = col(8).reshape(d, 4, HEADS)
    gates = jnp.concatenate([gates[:, 0], gates[:, 2], gates[:, 1], gates[:, 3]], axis=1)
    gb = mlstm_gate_bias[0]
    gbias = jnp.concatenate([gb[0], gb[2], gb[1], gb[3]]).reshape(4 * HEADS, 1).astype(F32)
    wq = jnp.concatenate([col(0), col(4)], axis=1).astype(BF16)
    wv = jnp.concatenate([col(2), col(6)], axis=1).astype(BF16)
    wkt = jnp.concatenate([col(1), col(5), gates], axis=1).T.astype(BF16)
    bg_w = d // HEADS
    wpost = jnp.concatenate(
        [piece for hd in range(HEADS)
         for piece in (col(3)[:, hd * HEAD_DIM:(hd + 1) * HEAD_DIM], col(7)[:, hd * HEAD_DIM:(hd + 1) * HEAD_DIM],
                       col(9)[:, hd * bg_w:(hd + 1) * bg_w], col(10)[:, hd * bg_w:(hd + 1) * bg_w])],
        axis=1).astype(BF16)

    log_gamma = jax.nn.log_sigmoid(ret_decay_logit[0].astype(F32)).reshape(2 * HEADS)
    ret_consts = jnp.concatenate([log_gamma, jnp.exp(log_gamma * SCAN_CHUNK)])

    gain1 = norm1_gain[0].reshape(1, d)
    ctx_final = _ctx_states(ret_consts, ctx, gain1, csh1, csc1, wv, wkt, gbias)
    q_x, v_x, kt_x, gr_x, sf, cf, mf, sb, cb, mb = _proj_scan(
        ret_consts, x, gain1, sh1, sc1, wq, wv, wkt, gbias, _rope_tables(n), ctx_final)
    x1 = _mixer(ret_consts, x, gain1, sh1, sc1, g1, q_x, kt_x, v_x, gr_x, (sf, sb, cf, cb, mf, mb),
                wpost, w_ret_up[0].astype(BF16), w_ml_up[0].astype(BF16), w_out[0].astype(BF16))
    return _ffn(x1, norm2_gain[0].reshape(1, d), sh2, sc2, g2, final_gain.reshape(1, d),
                w_ffn_in[0].astype(BF16), w_ffn_out[0].astype(BF16))
```

```python
import functools

import jax
import jax.numpy as jnp
from jax import lax
from jax.experimental import pallas as pl
from jax.experimental.pallas import tpu as pltpu

HEADS = 4
HEAD_DIM = 128
MIX_W = HEADS * HEAD_DIM
ONES_ROWS = 16
GRID_W = 64
ROPE_BASE = 10000.0
EPS = 1e-6
LOG2E = 1.4426950408889634
SCAN_CHUNK = 256
PROJ_TOKENS = 1024
BWD_CHUNKS = 16
MIX_TOKENS = 512
FFN_TOKENS = 1024
FFN_SPLITS = (1536, 1280)
VMEM_LIMIT = 60 * 1024 * 1024

F32 = jnp.float32
BF16 = jnp.bfloat16


def _resident(shape):
    return pl.BlockSpec(shape, lambda *_: (0,) * len(shape), pipeline_mode=pl.Buffered(1))


def _dot(a, b):
    return jnp.dot(a, b, preferred_element_type=F32)


def _dot_nt(a, b):
    return lax.dot_general(a, b, (((1,), (1,)), ((), ())), preferred_element_type=F32)


def _dot_tn(a, b):
    return lax.dot_general(a, b, (((0,), (0,)), ((), ())), preferred_element_type=F32)


def _sigmoid(t):
    return 0.5 * jnp.tanh(0.5 * t) + 0.5


def _silu(t):
    return t * _sigmoid(t)


def _log_sigmoid(t):
    return jnp.minimum(t, 0.0) - jnp.log1p(jnp.exp(-jnp.abs(t)))


def _rms(t, axis=-1):
    return t * lax.rsqrt(jnp.mean(t * t, axis=axis, keepdims=True) + EPS)


def _split3(t):
    hi = t.astype(BF16)
    r1 = t - hi.astype(F32)
    mid = r1.astype(BF16)
    lo = (r1 - mid.astype(F32)).astype(BF16)
    return hi, mid, lo


def _adaln_kernel(c_ref, w_ref, b_ref, o_ref):
    s = _silu(c_ref[...]).astype(BF16)
    o_ref[...] = _dot(s, w_ref[...].astype(BF16)) + b_ref[...]


def _adaln(cc, w, b):
    rows, d = cc.shape
    cols = w.shape[1]
    blk = 1536
    return pl.pallas_call(
        _adaln_kernel,
        grid=(cols // blk,),
        in_specs=[pl.BlockSpec((rows, d), lambda j: (0, 0)),
                  pl.BlockSpec((d, blk), lambda j: (0, j)),
                  pl.BlockSpec((1, blk), lambda j: (0, j))],
        out_specs=pl.BlockSpec((rows, blk), lambda j: (0, j)),
        out_shape=jax.ShapeDtypeStruct((rows, cols), F32),
        compiler_params=pltpu.CompilerParams(dimension_semantics=("parallel",),
                                             vmem_limit_bytes=VMEM_LIMIT),
        name="adaln",
    )(cc, w, b.reshape(1, cols))


def _project_q(hb, wqt_ref, rope_refs, qt_ref):
    L = SCAN_CHUNK
    n_chunks = hb.shape[0] // L
    qt = _dot_nt(wqt_ref[...], hb)
    for hd in range(HEADS):
        sl = slice(hd * HEAD_DIM, (hd + 1) * HEAD_DIM)
        t = qt[sl, :]
        if rope_refs is not None:
            cost_ref, sint_ref = rope_refs[2], rope_refs[3]
            half = HEAD_DIM // 2
            rot = jnp.concatenate([t[half:, :], t[:half, :]], axis=0)
            t = t * cost_ref[...] + rot * sint_ref[...]
        for ci in range(n_chunks):
            qt_ref[0, ci, sl, :] = t[:, ci * L:(ci + 1) * L].astype(BF16)
    for ci in range(n_chunks):
        qt_ref[0, ci, MIX_W:, :] = qt[MIX_W:, ci * L:(ci + 1) * L].astype(BF16)


def _project_kv(x_ref, gain_ref, sh_ref, sc_ref, wk_ref, wvt_ref, gb_ref, rope_refs, k_ref, vt_ref, gr_ref):
    L = SCAN_CHUNK
    tokens = x_ref.shape[1]
    n_chunks = tokens // L
    h = _rms(x_ref[0]) * gain_ref[...]
    h = h * (1.0 + sc_ref[0]) + sh_ref[0]
    hb = h.astype(BF16)

    vg = _dot_nt(wvt_ref[...], hb)
    for ci in range(n_chunks):
        vt_ref[0, ci] = vg[0:2 * MIX_W, ci * L:(ci + 1) * L].astype(BF16)

    g = vg[2 * MIX_W:, :] + gb_ref[...]
    i_pre = g[0:8, :]
    log_f = _log_sigmoid(g[8:16, :])
    src = lax.broadcasted_iota(jnp.int32, (L, L), 0)
    dst = lax.broadcasted_iota(jnp.int32, (L, L), 1)
    prefix_m = (src <= dst).astype(BF16)
    suffix_m = (src >= dst).astype(BF16)
    is_fwd = lax.broadcasted_iota(jnp.int32, (8, 1), 0) < HEADS
    for ci in range(n_chunks):
        cs = slice(ci * L, (ci + 1) * L)
        lf = log_f[:, cs]
        parts = _split3(jnp.concatenate([lf, lf], axis=0))
        pre = sum(_dot(p, prefix_m) for p in parts)[0:8]
        suf = sum(_dot(p, suffix_m) for p in parts)[0:8]
        cum = jnp.where(is_fwd, pre, suf)
        gr_ref[0, ci] = jnp.concatenate([i_pre[:, cs] - cum, cum], axis=0)

    k = _dot(hb, wk_ref[...]) * (HEAD_DIM ** -0.5)
    for hd in range(HEADS):
        sl = slice(hd * HEAD_DIM, (hd + 1) * HEAD_DIM)
        t = k[:, sl]
        if rope_refs is not None:
            cos_ref, sin_ref = rope_refs[0], rope_refs[1]
            t = t * cos_ref[...] + pltpu.roll(t, HEAD_DIM // 2, 1) * sin_ref[...]
        k_ref[0, :, sl] = t.astype(BF16)
    k_ref[0, :, MIX_W:] = k[:, MIX_W:].astype(BF16)
    return hb


S_SHAPE = (2, HEADS, HEAD_DIM, HEAD_DIM)
C_SHAPE = (2, HEADS, HEAD_DIM + ONES_ROWS, HEAD_DIM)
M_SHAPE = (2 * HEADS, HEAD_DIM)


def _chunk_sources(rc_ref, k_ref, vt_ref, gr_ref, ci, d):
    L = SCAN_CHUNK
    cs = slice(ci * L, (ci + 1) * L)
    pos = lax.broadcasted_iota(jnp.int32, (1, L), 1).astype(F32)
    ones = jnp.ones((ONES_ROWS, L), BF16)
    a = gr_ref[0, ci, 4 * d:4 * d + 4, :]
    cum = gr_ref[0, ci, 8 + 4 * d:12 + 4 * d, :]
    edge = cum[:, L - 1:L] if d == 0 else cum[:, 0:1]
    b_last = jnp.broadcast_to(edge, (HEADS, HEAD_DIM))
    amax = jnp.broadcast_to(jnp.max(a, axis=1, keepdims=True), (HEADS, HEAD_DIM))
    w_loc = jnp.exp(a - jnp.concatenate([amax] * (L // HEAD_DIM), axis=1))
    u_ret, u_ml = [], []
    for hd in range(HEADS):
        sl = slice(hd * HEAD_DIM, (hd + 1) * HEAD_DIM)
        ml = slice(MIX_W + hd * HEAD_DIM, MIX_W + (hd + 1) * HEAD_DIM)
        lg = rc_ref[4 * d + hd]
        to_end = jnp.exp(lg * ((L - 1.0) - pos)) if d == 0 else jnp.exp(lg * pos)
        vw = (vt_ref[0, ci, sl, :].astype(F32) * to_end).astype(BF16)
        u_ret.append(_dot(vw, k_ref[0, cs, sl]))
        v_ext = jnp.concatenate([vt_ref[0, ci, ml, :], ones], axis=0)
        vw = (v_ext.astype(F32) * w_loc[hd:hd + 1, :]).astype(BF16)
        u_ml.append(_dot(vw, k_ref[0, cs, ml]))
    return amax, b_last, u_ret, u_ml


def _advance_state(rc_ref, s_scr, c_scr, m_scr, d, amax, b_last, u_ret, u_ml, emit):
    rows = slice(HEADS * d, HEADS * (d + 1))
    m_old = m_scr[rows, :]
    m_mid = jnp.maximum(m_old, amax)
    w_old = jnp.exp(m_old - m_mid)
    w_new = jnp.exp(amax - m_mid)
    if emit is not None:
        s_out, c_out, m_out, slot = emit
        m_out[0, slot] = m_old
    m_scr[rows, :] = b_last + m_mid
    for hd in range(HEADS):
        s_prev = s_scr[d, hd]
        c_prev = c_scr[d, hd]
        if emit is not None:
            s_out[0, slot, hd] = s_prev.astype(BF16)
            c_out[0, slot, hd] = c_prev.astype(BF16)
        s_scr[d, hd] = rc_ref[2 * HEADS + HEADS * d + hd] * s_prev + u_ret[hd]
        c_scr[d, hd] = w_old[hd:hd + 1, :] * c_prev + w_new[hd:hd + 1, :] * u_ml[hd]


def _ctx_kernel(rc_ref, x_ref, gain_ref, sh_ref, sc_ref, wk_ref, wvt_ref, gb_ref,
                s_fin, c_fin, m_fin, k_scr, vt_scr, gr_scr, s_scr, c_scr, m_scr):
    L = SCAN_CHUNK
    _project_kv(x_ref, gain_ref, sh_ref, sc_ref, wk_ref, wvt_ref, gb_ref, None, k_scr, vt_scr, gr_scr)
    s_scr[...] = jnp.zeros_like(s_scr)
    c_scr[...] = jnp.zeros_like(c_scr)
    m_scr[...] = jnp.zeros_like(m_scr)
    n_chunks = x_ref.shape[1] // L
    for d in range(2):
        for ci in (range(n_chunks) if d == 0 else reversed(range(n_chunks))):
            _advance_state(rc_ref, s_scr, c_scr, m_scr, d, *_chunk_sources(rc_ref, k_scr, vt_scr, gr_scr, ci, d),
                           None)
    s_fin[0] = s_scr[...]
    c_fin[0] = c_scr[...]
    m_fin[0] = m_scr[...]


def _ctx_states(ret_consts, ctx, gain, shift, scale, wk, wvt, gbias):
    b, n, d = ctx.shape
    nck = n // SCAN_CHUNK
    per_b = lambda shape: pl.BlockSpec((1,) + shape, lambda i: (i,) + (0,) * len(shape))
    return pl.pallas_call(
        _ctx_kernel,
        grid=(b,),
        in_specs=[pl.BlockSpec(memory_space=pltpu.SMEM),
                  per_b((n, d)), _resident((1, d)), per_b((1, d)), per_b((1, d)),
                  _resident(wk.shape), _resident(wvt.shape), _resident(gbias.shape)],
        out_specs=(per_b(S_SHAPE), per_b(C_SHAPE), per_b(M_SHAPE)),
        out_shape=tuple(jax.ShapeDtypeStruct((b,) + shape, F32) for shape in (S_SHAPE, C_SHAPE, M_SHAPE)),
        scratch_shapes=[pltpu.VMEM((1, n, 2 * MIX_W), BF16),
                        pltpu.VMEM((1, nck, 2 * MIX_W, SCAN_CHUNK), BF16),
                        pltpu.VMEM((1, nck, 16, SCAN_CHUNK), F32),
                        pltpu.VMEM(S_SHAPE, F32), pltpu.VMEM(C_SHAPE, F32), pltpu.VMEM(M_SHAPE, F32)],
        compiler_params=pltpu.CompilerParams(dimension_semantics=("arbitrary",),
                                             vmem_limit_bytes=VMEM_LIMIT),
        name="ctx_states",
    )(ret_consts, ctx, gain, shift, scale, wk, wvt, gbias)


def _proj_scan_kernel(rc_ref, x_ref, gain_ref, sh_ref, sc_ref, wqt_ref, wk_ref, wvt_ref, gb_ref,
                      cos_ref, sin_ref, cost_ref, sint_ref, s0_ref, c0_ref, m0_ref,
                      qt_ref, k_ref, vt_ref, gr_ref, sf_ref, cf_ref, mf_ref, sb_ref, cb_ref, mb_ref,
                      s_scr, c_scr, m_scr, ub_ret, ub_ml, ub_stat, *, n_tiles, bwd_chunks):
    L = SCAN_CHUNK
    j = pl.program_id(1)
    per_tile = x_ref.shape[1] // L
    n_chunks = n_tiles * per_tile

    @pl.when(j == 0)
    def _():
        s_scr[...] = s0_ref[0]
        c_scr[...] = c0_ref[0]
        m_scr[...] = m0_ref[0]

    @pl.when(j < n_tiles)
    def _():
        rope_refs = (cos_ref, sin_ref, cost_ref, sint_ref)
        hb = _project_kv(x_ref, gain_ref, sh_ref, sc_ref, wk_ref, wvt_ref, gb_ref, rope_refs,
                         k_ref, vt_ref, gr_ref)
        _project_q(hb, wqt_ref, rope_refs, qt_ref)
        for ci in range(per_tile):
            _advance_state(rc_ref, s_scr, c_scr, m_scr, 0, *_chunk_sources(rc_ref, k_ref, vt_ref, gr_ref, ci, 0),
                           (sf_ref, cf_ref, mf_ref, ci))
            amax, b_last, u_ret, u_ml = _chunk_sources(rc_ref, k_ref, vt_ref, gr_ref, ci, 1)
            chunk = j * per_tile + ci
            ub_stat[chunk] = jnp.concatenate([amax, b_last], axis=0)
            for hd in range(HEADS):
                ub_ret[chunk, hd] = u_ret[hd].astype(BF16)
                ub_ml[chunk, hd] = u_ml[hd].astype(BF16)

    @pl.when(j >= n_tiles)
    def _():
        first = n_chunks - 1 - (j - n_tiles) * bwd_chunks
        for i in range(bwd_chunks):
            chunk = first - i
            stat = ub_stat[chunk]
            _advance_state(rc_ref, s_scr, c_scr, m_scr, 1, stat[0:HEADS], stat[HEADS:2 * HEADS],
                           [ub_ret[chunk, hd].astype(F32) for hd in range(HEADS)],
                           [ub_ml[chunk, hd].astype(F32) for hd in range(HEADS)],
                           (sb_ref, cb_ref, mb_ref, bwd_chunks - 1 - i))


def _proj_scan(ret_consts, x, gain, shift, scale, wqt, wk, wvt, gbias, rope, init):
    b, n, d = x.shape
    L = SCAN_CHUNK
    t = PROJ_TOKENS
    nt = n // t
    nc = n // L
    per_tile = t // L
    nbb = nc // BWD_CHUNKS
    tile = lambda j: jnp.minimum(j, nt - 1)
    bblk = lambda j: nbb - 1 - jnp.maximum(j - nt, 0)
    tok3 = lambda i, j: (i, tile(j), 0)
    chunk4 = lambda i, j: (i, tile(j), 0, 0)
    mod3 = lambda i, j: (i, 0, 0)
    per_b = lambda shape: pl.BlockSpec((1,) + shape, lambda i, j: (i,) + (0,) * len(shape))
    cos, sin, cos_t, sin_t = rope
    in_specs = [pl.BlockSpec(memory_space=pltpu.SMEM),
                pl.BlockSpec((1, t, d), tok3),
                _resident((1, d)),
                pl.BlockSpec((1, 1, d), mod3), pl.BlockSpec((1, 1, d), mod3),
                _resident(wqt.shape), _resident(wk.shape), _resident(wvt.shape), _resident(gbias.shape),
                pl.BlockSpec((t, HEAD_DIM), lambda i, j: (tile(j), 0)),
                pl.BlockSpec((t, HEAD_DIM), lambda i, j: (tile(j), 0)),
                pl.BlockSpec((HEAD_DIM, t), lambda i, j: (0, tile(j))),
                pl.BlockSpec((HEAD_DIM, t), lambda i, j: (0, tile(j))),
                per_b(S_SHAPE), per_b(C_SHAPE), per_b(M_SHAPE)]
    out_shape = [jax.ShapeDtypeStruct((b, nc, 2 * MIX_W, L), BF16),
                 jax.ShapeDtypeStruct((b, n, 2 * MIX_W), BF16),
                 jax.ShapeDtypeStruct((b, nc, 2 * MIX_W, L), BF16),
                 jax.ShapeDtypeStruct((b, nc, 16, L), F32)]
    out_specs = [pl.BlockSpec((1, per_tile, 2 * MIX_W, L), chunk4),
                 pl.BlockSpec((1, t, 2 * MIX_W), tok3),
                 pl.BlockSpec((1, per_tile, 2 * MIX_W, L), chunk4),
                 pl.BlockSpec((1, per_tile, 16, L), chunk4)]
    for blk, step in ((per_tile, tile), (BWD_CHUNKS, bblk)):
        out_shape += [jax.ShapeDtypeStruct((b, nc) + S_SHAPE[1:], BF16),
                      jax.ShapeDtypeStruct((b, nc) + C_SHAPE[1:], BF16),
                      jax.ShapeDtypeStruct((b, nc, HEADS, HEAD_DIM), F32)]
        out_specs += [pl.BlockSpec((1, blk) + S_SHAPE[1:], lambda i, j, step=step: (i, step(j), 0, 0, 0)),
                      pl.BlockSpec((1, blk) + C_SHAPE[1:], lambda i, j, step=step: (i, step(j), 0, 0, 0)),
                      pl.BlockSpec((1, blk, HEADS, HEAD_DIM), lambda i, j, step=step: (i, step(j), 0, 0))]
    return pl.pallas_call(
        functools.partial(_proj_scan_kernel, n_tiles=nt, bwd_chunks=BWD_CHUNKS),
        grid=(b, nt + nbb),
        in_specs=in_specs,
        out_specs=tuple(out_specs),
        out_shape=tuple(out_shape),
        scratch_shapes=[pltpu.VMEM(S_SHAPE, F32), pltpu.VMEM(C_SHAPE, F32), pltpu.VMEM(M_SHAPE, F32),
                        pltpu.VMEM((nc,) + S_SHAPE[1:], BF16), pltpu.VMEM((nc,) + C_SHAPE[1:], BF16),
                        pltpu.VMEM((nc,) + M_SHAPE, F32)],
        compiler_params=pltpu.CompilerParams(dimension_semantics=("arbitrary", "arbitrary"),
                                             vmem_limit_bytes=VMEM_LIMIT),
        name="proj_scan",
    )(ret_consts, x, gain, shift, scale, wqt, wk, wvt, gbias, cos, sin, cos_t, sin_t, *init)


def _mixer_kernel(rc_ref, x_ref, gain_ref, sh_ref, sc_ref, g1_ref, qt_ref, k_ref, vt_ref, gr_ref,
                  sf_ref, sb_ref, cf_ref, cb_ref, mf_ref, mb_ref,
                  wgt_ref, wbg_ref, wru_ref, wmu_ref, wout_ref, o_ref, dec_scr, qdec_scr):
    L = SCAN_CHUNK
    key = lax.broadcasted_iota(jnp.int32, (L, L), 0)
    qry = lax.broadcasted_iota(jnp.int32, (L, L), 1)

    @pl.when((pl.program_id(0) == 0) & (pl.program_id(1) == 0))
    def _():
        rel = (qry - key).astype(F32)
        lpos = lax.broadcasted_iota(jnp.int32, (8, L), 1).astype(F32)
        first_row = lax.broadcasted_iota(jnp.int32, (8, L), 0) == 0
        for hd in range(HEADS):
            lg_f = rc_ref[hd]
            lg_b = rc_ref[4 + hd]
            dec_scr[hd] = (jnp.where(rel >= 0, jnp.exp(lg_f * jnp.maximum(rel, 0.0)), 0.0)
                           + jnp.where(rel <= 0, jnp.exp(lg_b * jnp.maximum(-rel, 0.0)), 0.0))
            qdec_scr[hd] = jnp.where(first_row, jnp.exp(lg_f * (lpos + 1.0)), jnp.exp(lg_b * (L - lpos)))

    d_model = x_ref.shape[-1]
    bg_w = d_model // HEADS
    masks = (key <= qry, key >= qry)
    c_refs, m_refs = (cf_ref, cb_ref), (mf_ref, mb_ref)
    ones = jnp.ones((ONES_ROWS, L), BF16)
    neg_inf = jnp.float32(-jnp.inf)
    n_chunks = x_ref.shape[1] // L
    a_cols = [(gr_ref[0, ci, 0:8, :] * LOG2E).T for ci in range(n_chunks)]
    scanned, y_ret, y_ml, bg_ret, bg_ml = [], [], [], [], []
    normed = []

    def gate_head(hd):
        if not normed:
            normed.append((_rms(x_ref[0]) * gain_ref[...] * (1.0 + sc_ref[0]) + sh_ref[0]).astype(BF16))
        hb = normed[0]
        gates = _dot_nt(wgt_ref[2 * hd * HEAD_DIM:2 * (hd + 1) * HEAD_DIM, :], hb)
        bg = _dot(hb, wbg_ref[:, 2 * hd * bg_w:2 * (hd + 1) * bg_w])
        bg_ret.append(bg[:, :bg_w])
        bg_ml.append(bg[:, bg_w:])
        ret_t, ml_t = scanned[hd]
        y_ret.append(ret_t * _silu(gates[:HEAD_DIM, :]))
        y_ml.append(_rms(_sigmoid(gates[HEAD_DIM:, :]) * ml_t, axis=0))

    for hd in range(HEADS):
        sl = slice(hd * HEAD_DIM, (hd + 1) * HEAD_DIM)
        ml = slice(MIX_W + hd * HEAD_DIM, MIX_W + (hd + 1) * HEAD_DIM)
        chunks = range(n_chunks)
        toks = [slice(ci * L, (ci + 1) * L) for ci in chunks]
        qt_r = [qt_ref[0, ci, sl, :] for ci in chunks]
        qt_m = [qt_ref[0, ci, ml, :] for ci in chunks]
        s_ret = [_dot(k_ref[0, toks[ci], sl], qt_r[ci]) for ci in chunks]
        s_ml = [_dot(k_ref[0, toks[ci], ml], qt_m[ci]) for ci in chunks]
        inter_ret, inter_ml = [], []
        for ci in chunks:
            qf = qt_r[ci].astype(F32)
            qw = jnp.concatenate([qf * qdec_scr[hd, 0:1, :], qf * qdec_scr[hd, 1:2, :]], axis=0).astype(BF16)
            st = jnp.concatenate([sf_ref[0, ci, hd], sb_ref[0, ci, hd]], axis=1)
            inter_ret.append(_dot(st, qw))
            inter_ml.append([_dot(c_refs[d][0, ci, hd], qt_m[ci]) for d in range(2)])
        p_ret = [(s_ret[ci] * dec_scr[hd]).astype(BF16) for ci in chunks]
        p_ml, m_locs, m_ins = [], [], []
        for ci in chunks:
            for d in range(2):
                a_col = a_cols[ci][:, 4 * d + hd:4 * d + hd + 1]
                m_in = jnp.concatenate([m_refs[d][0, ci, hd:hd + 1, :] * LOG2E] * (L // HEAD_DIM), axis=1)
                a_masked = jnp.where(masks[d], a_col, neg_inf)
                m_loc = jnp.maximum(jnp.max(a_masked, axis=0, keepdims=True), m_in)
                p_ml.append((s_ml[ci] * jnp.exp2(a_masked - m_loc)).astype(BF16))
                m_locs.append(m_loc)
                m_ins.append(m_in)
        ret_cols, ml_cols = [], []
        for ci in chunks:
            ret_cols.append(_rms(_dot(vt_ref[0, ci, sl, :], p_ret[ci]) + inter_ret[ci], axis=0))
            v_ext = jnp.concatenate([vt_ref[0, ci, ml, :], ones], axis=0)
            hsum = None
            for d in range(2):
                m_loc, m_in = m_locs[2 * ci + d], m_ins[2 * ci + d]
                cum_row = gr_ref[0, ci, 8 + 4 * d + hd:9 + 4 * d + hd, :] * LOG2E
                tot = _dot(v_ext, p_ml[2 * ci + d]) + jnp.exp2(m_in - m_loc) * inter_ml[ci][d]
                den = jnp.maximum(jnp.abs(tot[HEAD_DIM:HEAD_DIM + 1, :]), jnp.exp2(-(cum_row + m_loc)))
                hd_out = tot[0:HEAD_DIM, :] * (1.0 / den)
                hsum = hd_out if hsum is None else hsum + hd_out
            ml_cols.append(hsum)
        scanned.append((jnp.concatenate(ret_cols, axis=1), jnp.concatenate(ml_cols, axis=1)))
        if hd >= 1:
            gate_head(hd - 1)
    gate_head(HEADS - 1)

    y_ret = jnp.concatenate(y_ret, axis=0).astype(BF16)
    y_ml = jnp.concatenate(y_ml, axis=0).astype(BF16)
    bg_ret = jnp.concatenate(bg_ret, axis=1)
    bg_ml = jnp.concatenate(bg_ml, axis=1)
    merged = _sigmoid(bg_ret) * _dot_tn(y_ret, wru_ref[...]) + _sigmoid(bg_ml) * _dot_tn(y_ml, wmu_ref[...])
    o_ref[0] = x_ref[0] + g1_ref[0] * _dot(merged.astype(BF16), wout_ref[...])


def _mixer(ret_consts, x, gain, shift, scale, gate1, qt, k, vt, gr, chunk_states, wgt, wbg, wru, wmu, wout):
    b, n, d = x.shape
    L = SCAN_CHUNK
    t = MIX_TOKENS
    ch = t // L
    sf, sb, cf, cb, mf, mb = chunk_states
    tok3 = lambda i, j: (i, j, 0)
    mod3 = lambda i, j: (i, 0, 0)
    st5 = lambda i, j: (i, j, 0, 0, 0)
    st4 = lambda i, j: (i, j, 0, 0)
    in_specs = [pl.BlockSpec(memory_space=pltpu.SMEM),
                pl.BlockSpec((1, t, d), tok3),
                _resident((1, d)),
                pl.BlockSpec((1, 1, d), mod3), pl.BlockSpec((1, 1, d), mod3), pl.BlockSpec((1, 1, d), mod3),
                pl.BlockSpec((1, ch, 2 * MIX_W, L), st4),
                pl.BlockSpec((1, t, 2 * MIX_W), tok3),
                pl.BlockSpec((1, ch, 2 * MIX_W, L), st4),
                pl.BlockSpec((1, ch, 16, L), st4),
                pl.BlockSpec((1, ch) + S_SHAPE[1:], st5),
                pl.BlockSpec((1, ch) + S_SHAPE[1:], st5),
                pl.BlockSpec((1, ch) + C_SHAPE[1:], st5),
                pl.BlockSpec((1, ch) + C_SHAPE[1:], st5),
                pl.BlockSpec((1, ch, HEADS, HEAD_DIM), st4),
                pl.BlockSpec((1, ch, HEADS, HEAD_DIM), st4),
                _resident(wgt.shape),
                _resident(wbg.shape),
                _resident(wru.shape),
                _resident(wmu.shape),
                _resident(wout.shape)]
    return pl.pallas_call(
        _mixer_kernel,
        grid=(b, n // t),
        in_specs=in_specs,
        out_specs=pl.BlockSpec((1, t, d), tok3),
        out_shape=jax.ShapeDtypeStruct((b, n, d), F32),
        scratch_shapes=[pltpu.VMEM((HEADS, L, L), F32), pltpu.VMEM((HEADS, 8, L), F32)],
        compiler_params=pltpu.CompilerParams(dimension_semantics=("arbitrary", "arbitrary"),
                                             vmem_limit_bytes=VMEM_LIMIT),
        name="mixer",
    )(ret_consts, x, gain, shift, scale, gate1, qt, k, vt, gr, sf, sb, cf, cb, mf, mb,
      wgt, wbg, wru, wmu, wout)


def _ffn_kernel(x_ref, gain_ref, sh_ref, sc_ref, g2_ref, fgain_ref, w1_ref, w2_ref, o_ref, *, hidden):
    xf = x_ref[0]
    h = _rms(xf) * gain_ref[...]
    hb = (h * (1.0 + sc_ref[0]) + sh_ref[0]).astype(BF16)
    acc = None
    start = 0
    for width in FFN_SPLITS:
        gate = _dot(hb, w1_ref[:, start:start + width])
        up = _dot(hb, w1_ref[:, hidden + start:hidden + start + width])
        act = (_silu(gate) * up).astype(BF16)
        part = _dot(act, w2_ref[start:start + width, :])
        acc = part if acc is None else acc + part
        start += width
    o_ref[0] = _rms(xf + g2_ref[0] * acc) * fgain_ref[...]


def _ffn(x, gain, shift, scale, gate2, final_gain, w1, w2):
    b, n, d = x.shape
    t = FFN_TOKENS
    hidden = w2.shape[0]
    assert sum(FFN_SPLITS) == hidden
    tok3 = lambda i, j: (i, j, 0)
    mod3 = lambda i, j: (i, 0, 0)
    return pl.pallas_call(
        functools.partial(_ffn_kernel, hidden=hidden),
        grid=(b, n // t),
        in_specs=[pl.BlockSpec((1, t, d), tok3),
                  _resident((1, d)),
                  pl.BlockSpec((1, 1, d), mod3), pl.BlockSpec((1, 1, d), mod3), pl.BlockSpec((1, 1, d), mod3),
                  _resident((1, d)),
                  _resident(w1.shape),
                  _resident(w2.shape)],
        out_specs=pl.BlockSpec((1, t, d), tok3),
        out_shape=jax.ShapeDtypeStruct((b, n, d), F32),
        compiler_params=pltpu.CompilerParams(dimension_semantics=("parallel", "parallel"),
                                             vmem_limit_bytes=VMEM_LIMIT),
        name="ffn",
    )(x, gain, shift, scale, gate2, final_gain, w1, w2)


def _rope_tables(n):
    n_rows = n // GRID_W
    rows = jnp.broadcast_to(jnp.arange(n_rows, dtype=F32)[:, None], (n_rows, GRID_W)).reshape(n)
    cols = jnp.broadcast_to(jnp.arange(GRID_W, dtype=F32)[None, :], (n_rows, GRID_W)).reshape(n)
    n_freq = HEAD_DIM // 4
    inv = ROPE_BASE ** (-jnp.arange(n_freq, dtype=F32) / n_freq)
    ang = jnp.concatenate([rows[:, None] * inv, cols[:, None] * inv], axis=-1)
    cos, sin = jnp.cos(ang), jnp.sin(ang)
    cos2 = jnp.concatenate([cos, cos], axis=-1)
    sin2 = jnp.concatenate([-sin, sin], axis=-1)
    return cos2, sin2, cos2.T, sin2.T


def kernel(x, c, ctx, c_ctx, w_ada, b_ada, norm1_gain, norm2_gain, w_in, mlstm_gate_bias, ret_decay_logit,
           w_ret_up, w_ml_up, w_out, w_ffn_in, w_ffn_out, final_gain):
    assert w_ada.shape[0] == 1, "single-layer block"
    b, n, d = x.shape
    assert n % PROJ_TOKENS == 0 and n % FFN_TOKENS == 0 and n % MIX_TOKENS == 0
    assert PROJ_TOKENS % SCAN_CHUNK == 0 and MIX_TOKENS % SCAN_CHUNK == 0 and ctx.shape[1] % SCAN_CHUNK == 0
    assert (n // SCAN_CHUNK) % BWD_CHUNKS == 0

    rows = -(-(b + 1) // 16) * 16
    cc = jnp.concatenate([c, c_ctx[None, :], jnp.zeros((rows - b - 1, d), F32)], axis=0)
    mod = _adaln(cc, w_ada[0], b_ada[0])
    sh1, sc1, g1, sh2, sc2, g2 = (mod[:b, i * d:(i + 1) * d].reshape(b, 1, d) for i in range(6))
    csh1 = jnp.broadcast_to(mod[b, 0:d].reshape(1, 1, d), (b, 1, d))
    csc1 = jnp.broadcast_to(mod[b, d:2 * d].reshape(1, 1, d), (b, 1, d))

    w = w_in[0]
    o = [0]
    for width in (MIX_W, MIX_W, MIX_W, MIX_W, MIX_W, MIX_W, MIX_W, MIX_W, 4 * HEADS, d, d):
        o.append(o[-1] + width)
    col = lambda i: w[:, o[i]:o[i + 1]]
    head = lambda t, hd, width: t[:, hd * width:(hd + 1) * width]
    gates = col(8).reshape(d, 4, HEADS)
    gates = jnp.concatenate([gates[:, 0], gates[:, 2], gates[:, 1], gates[:, 3]], axis=1)
    gb = mlstm_gate_bias[0]
    gbias = jnp.concatenate([gb[0], gb[2], gb[1], gb[3]]).reshape(4 * HEADS, 1).astype(F32)
    wqt = jnp.concatenate([col(0), col(4)], axis=1).T.astype(BF16)
    wk = jnp.concatenate([col(1), col(5)], axis=1).astype(BF16)
    wvt = jnp.concatenate([col(2), col(6), gates], axis=1).T.astype(BF16)
    bg_w = d // HEADS
    wgt = jnp.concatenate([piece for hd in range(HEADS)
                           for piece in (head(col(3), hd, HEAD_DIM), head(col(7), hd, HEAD_DIM))],
                          axis=1).T.astype(BF16)
    wbg = jnp.concatenate([piece for hd in range(HEADS)
                           for piece in (head(col(9), hd, bg_w), head(col(10), hd, bg_w))], axis=1).astype(BF16)

    log_gamma = jax.nn.log_sigmoid(ret_decay_logit[0].astype(F32)).reshape(2 * HEADS)
    ret_consts = jnp.concatenate([log_gamma, jnp.exp(log_gamma * SCAN_CHUNK)])

    gain1 = norm1_gain[0].reshape(1, d)
    ctx_final = _ctx_states(ret_consts, ctx, gain1, csh1, csc1, wk, wvt, gbias)
    qt_x, k_x, vt_x, gr_x, sf, cf, mf, sb, cb, mb = _proj_scan(
        ret_consts, x, gain1, sh1, sc1, wqt, wk, wvt, gbias, _rope_tables(n), ctx_final)
    x1 = _mixer(ret_consts, x, gain1, sh1, sc1, g1, qt_x, k_x, vt_x, gr_x, (sf, sb, cf, cb, mf, mb),
                wgt, wbg, w_ret_up[0].astype(BF16), w_ml_up[0].astype(BF16), w_out[0].astype(BF16))
    return _ffn(x1, norm2_gain[0].reshape(1, d), sh2, sc2, g2, final_gain.reshape(1, d),
                w_ffn_in[0].astype(BF16), w_ffn_out[0].astype(BF16))
```

```python
import functools

import jax
import jax.numpy as jnp
from jax import lax
from jax.experimental import pallas as pl
from jax.experimental.pallas import tpu as pltpu

HEADS = 4
HEAD_DIM = 128
MIX_W = HEADS * HEAD_DIM
ONES_ROWS = 16
GRID_W = 64
ROPE_BASE = 10000.0
EPS = 1e-6
LOG2E = 1.4426950408889634
SCAN_CHUNK = 256
PROJ_TOKENS = 1024
BWD_CHUNKS = 16
MIX_TOKENS = 1024
FFN_TOKENS = 1024
FFN_SPLITS = (1536, 1280)
VMEM_LIMIT = 60 * 1024 * 1024

F32 = jnp.float32
BF16 = jnp.bfloat16


def _resident(shape):
    return pl.BlockSpec(shape, lambda *_: (0,) * len(shape), pipeline_mode=pl.Buffered(1))


def _dot(a, b):
    return jnp.dot(a, b, preferred_element_type=F32)


def _dot_nt(a, b):
    return lax.dot_general(a, b, (((1,), (1,)), ((), ())), preferred_element_type=F32)


def _dot_tn(a, b):
    return lax.dot_general(a, b, (((0,), (0,)), ((), ())), preferred_element_type=F32)


def _sigmoid(t):
    return 0.5 * jnp.tanh(0.5 * t) + 0.5


def _silu(t):
    return t * _sigmoid(t)


def _log_sigmoid(t):
    return jnp.minimum(t, 0.0) - jnp.log1p(jnp.exp(-jnp.abs(t)))


def _rms(t, axis=-1):
    return t * lax.rsqrt(jnp.mean(t * t, axis=axis, keepdims=True) + EPS)


def _split3(t):
    hi = t.astype(BF16)
    r1 = t - hi.astype(F32)
    mid = r1.astype(BF16)
    lo = (r1 - mid.astype(F32)).astype(BF16)
    return hi, mid, lo


def _adaln_kernel(c_ref, w_ref, b_ref, o_ref):
    s = _silu(c_ref[...]).astype(BF16)
    o_ref[...] = _dot(s, w_ref[...].astype(BF16)) + b_ref[...]


def _adaln(cc, w, b):
    rows, d = cc.shape
    cols = w.shape[1]
    blk = 1536
    return pl.pallas_call(
        _adaln_kernel,
        grid=(cols // blk,),
        in_specs=[pl.BlockSpec((rows, d), lambda j: (0, 0)),
                  pl.BlockSpec((d, blk), lambda j: (0, j)),
                  pl.BlockSpec((1, blk), lambda j: (0, j))],
        out_specs=pl.BlockSpec((rows, blk), lambda j: (0, j)),
        out_shape=jax.ShapeDtypeStruct((rows, cols), F32),
        compiler_params=pltpu.CompilerParams(dimension_semantics=("parallel",),
                                             vmem_limit_bytes=VMEM_LIMIT),
        name="adaln",
    )(cc, w, b.reshape(1, cols))


def _project_q(hb, wqt_ref, rope_refs, qt_ref):
    L = SCAN_CHUNK
    n_chunks = hb.shape[0] // L
    qt = _dot_nt(wqt_ref[...], hb)
    for hd in range(HEADS):
        sl = slice(hd * HEAD_DIM, (hd + 1) * HEAD_DIM)
        t = qt[sl, :]
        if rope_refs is not None:
            cost_ref, sint_ref = rope_refs[2], rope_refs[3]
            half = HEAD_DIM // 2
            rot = jnp.concatenate([t[half:, :], t[:half, :]], axis=0)
            t = t * cost_ref[...] + rot * sint_ref[...]
        for ci in range(n_chunks):
            qt_ref[0, ci, sl, :] = t[:, ci * L:(ci + 1) * L].astype(BF16)
    for ci in range(n_chunks):
        qt_ref[0, ci, MIX_W:, :] = qt[MIX_W:, ci * L:(ci + 1) * L].astype(BF16)


def _project_kv(x_ref, gain_ref, sh_ref, sc_ref, wk_ref, wvt_ref, gb_ref, rope_refs, k_ref, vt_ref, gr_ref):
    L = SCAN_CHUNK
    tokens = x_ref.shape[1]
    n_chunks = tokens // L
    h = _rms(x_ref[0]) * gain_ref[...]
    h = h * (1.0 + sc_ref[0]) + sh_ref[0]
    hb = h.astype(BF16)

    vg = _dot_nt(wvt_ref[...], hb)
    for ci in range(n_chunks):
        vt_ref[0, ci] = vg[0:2 * MIX_W, ci * L:(ci + 1) * L].astype(BF16)

    g = vg[2 * MIX_W:, :] + gb_ref[...]
    i_pre = g[0:8, :]
    log_f = _log_sigmoid(g[8:16, :])
    src = lax.broadcasted_iota(jnp.int32, (L, L), 0)
    dst = lax.broadcasted_iota(jnp.int32, (L, L), 1)
    prefix_m = (src <= dst).astype(BF16)
    suffix_m = (src >= dst).astype(BF16)
    is_fwd = lax.broadcasted_iota(jnp.int32, (8, 1), 0) < HEADS
    for ci in range(n_chunks):
        cs = slice(ci * L, (ci + 1) * L)
        lf = log_f[:, cs]
        parts = _split3(jnp.concatenate([lf, lf], axis=0))
        pre = sum(_dot(p, prefix_m) for p in parts)[0:8]
        suf = sum(_dot(p, suffix_m) for p in parts)[0:8]
        cum = jnp.where(is_fwd, pre, suf)
        gr_ref[0, ci] = jnp.concatenate([i_pre[:, cs] - cum, cum], axis=0)

    k = _dot(hb, wk_ref[...]) * (HEAD_DIM ** -0.5)
    for hd in range(HEADS):
        sl = slice(hd * HEAD_DIM, (hd + 1) * HEAD_DIM)
        t = k[:, sl]
        if rope_refs is not None:
            cos_ref, sin_ref = rope_refs[0], rope_refs[1]
            t = t * cos_ref[...] + pltpu.roll(t, HEAD_DIM // 2, 1) * sin_ref[...]
        k_ref[0, :, sl] = t.astype(BF16)
    k_ref[0, :, MIX_W:] = k[:, MIX_W:].astype(BF16)
    return hb


S_SHAPE = (2, HEADS, HEAD_DIM, HEAD_DIM)
C_SHAPE = (2, HEADS, HEAD_DIM + ONES_ROWS, HEAD_DIM)
M_SHAPE = (2 * HEADS, HEAD_DIM)


def _chunk_sources(rc_ref, k_ref, vt_ref, gr_ref, ci, d):
    L = SCAN_CHUNK
    cs = slice(ci * L, (ci + 1) * L)
    pos = lax.broadcasted_iota(jnp.int32, (1, L), 1).astype(F32)
    ones = jnp.ones((ONES_ROWS, L), BF16)
    a = gr_ref[0, ci, 4 * d:4 * d + 4, :]
    cum = gr_ref[0, ci, 8 + 4 * d:12 + 4 * d, :]
    edge = cum[:, L - 1:L] if d == 0 else cum[:, 0:1]
    b_last = jnp.broadcast_to(edge, (HEADS, HEAD_DIM))
    amax = jnp.broadcast_to(jnp.max(a, axis=1, keepdims=True), (HEADS, HEAD_DIM))
    w_loc = jnp.exp(a - jnp.concatenate([amax] * (L // HEAD_DIM), axis=1))
    u_ret, u_ml = [], []
    for hd in range(HEADS):
        sl = slice(hd * HEAD_DIM, (hd + 1) * HEAD_DIM)
        ml = slice(MIX_W + hd * HEAD_DIM, MIX_W + (hd + 1) * HEAD_DIM)
        lg = rc_ref[4 * d + hd]
        to_end = jnp.exp(lg * ((L - 1.0) - pos)) if d == 0 else jnp.exp(lg * pos)
        vw = (vt_ref[0, ci, sl, :].astype(F32) * to_end).astype(BF16)
        u_ret.append(_dot(vw, k_ref[0, cs, sl]))
        v_ext = jnp.concatenate([vt_ref[0, ci, ml, :], ones], axis=0)
        vw = (v_ext.astype(F32) * w_loc[hd:hd + 1, :]).astype(BF16)
        u_ml.append(_dot(vw, k_ref[0, cs, ml]))
    return amax, b_last, u_ret, u_ml


def _advance_state(rc_ref, s_scr, c_scr, m_scr, d, amax, b_last, u_ret, u_ml, emit):
    rows = slice(HEADS * d, HEADS * (d + 1))
    m_old = m_scr[rows, :]
    m_mid = jnp.maximum(m_old, amax)
    w_old = jnp.exp(m_old - m_mid)
    w_new = jnp.exp(amax - m_mid)
    if emit is not None:
        s_out, c_out, m_out, slot = emit
        m_out[0, slot] = m_old
    m_scr[rows, :] = b_last + m_mid
    for hd in range(HEADS):
        s_prev = s_scr[d, hd]
        c_prev = c_scr[d, hd]
        if emit is not None:
            s_out[0, slot, hd] = s_prev.astype(BF16)
            c_out[0, slot, hd] = c_prev.astype(BF16)
        s_scr[d, hd] = rc_ref[2 * HEADS + HEADS * d + hd] * s_prev + u_ret[hd]
        c_scr[d, hd] = w_old[hd:hd + 1, :] * c_prev + w_new[hd:hd + 1, :] * u_ml[hd]


def _ctx_kernel(rc_ref, x_ref, gain_ref, sh_ref, sc_ref, wk_ref, wvt_ref, gb_ref,
                s_fin, c_fin, m_fin, k_scr, vt_scr, gr_scr, s_scr, c_scr, m_scr):
    L = SCAN_CHUNK
    _project_kv(x_ref, gain_ref, sh_ref, sc_ref, wk_ref, wvt_ref, gb_ref, None, k_scr, vt_scr, gr_scr)
    s_scr[...] = jnp.zeros_like(s_scr)
    c_scr[...] = jnp.zeros_like(c_scr)
    m_scr[...] = jnp.zeros_like(m_scr)
    n_chunks = x_ref.shape[1] // L
    for d in range(2):
        for ci in (range(n_chunks) if d == 0 else reversed(range(n_chunks))):
            _advance_state(rc_ref, s_scr, c_scr, m_scr, d, *_chunk_sources(rc_ref, k_scr, vt_scr, gr_scr, ci, d),
                           None)
    s_fin[0] = s_scr[...]
    c_fin[0] = c_scr[...]
    m_fin[0] = m_scr[...]


def _ctx_states(ret_consts, ctx, gain, shift, scale, wk, wvt, gbias):
    b, n, d = ctx.shape
    nck = n // SCAN_CHUNK
    per_b = lambda shape: pl.BlockSpec((1,) + shape, lambda i: (i,) + (0,) * len(shape))
    return pl.pallas_call(
        _ctx_kernel,
        grid=(b,),
        in_specs=[pl.BlockSpec(memory_space=pltpu.SMEM),
                  per_b((n, d)), _resident((1, d)), per_b((1, d)), per_b((1, d)),
                  _resident(wk.shape), _resident(wvt.shape), _resident(gbias.shape)],
        out_specs=(per_b(S_SHAPE), per_b(C_SHAPE), per_b(M_SHAPE)),
        out_shape=tuple(jax.ShapeDtypeStruct((b,) + shape, F32) for shape in (S_SHAPE, C_SHAPE, M_SHAPE)),
        scratch_shapes=[pltpu.VMEM((1, n, 2 * MIX_W), BF16),
                        pltpu.VMEM((1, nck, 2 * MIX_W, SCAN_CHUNK), BF16),
                        pltpu.VMEM((1, nck, 16, SCAN_CHUNK), F32),
                        pltpu.VMEM(S_SHAPE, F32), pltpu.VMEM(C_SHAPE, F32), pltpu.VMEM(M_SHAPE, F32)],
        compiler_params=pltpu.CompilerParams(dimension_semantics=("arbitrary",),
                                             vmem_limit_bytes=VMEM_LIMIT),
        name="ctx_states",
    )(ret_consts, ctx, gain, shift, scale, wk, wvt, gbias)


def _proj_scan_kernel(rc_ref, x_ref, gain_ref, sh_ref, sc_ref, wqt_ref, wk_ref, wvt_ref, gb_ref,
                      cos_ref, sin_ref, cost_ref, sint_ref, s0_ref, c0_ref, m0_ref,
                      qt_ref, k_ref, vt_ref, gr_ref, sf_ref, cf_ref, mf_ref, sb_ref, cb_ref, mb_ref,
                      s_scr, c_scr, m_scr, ub_ret, ub_ml, ub_stat, *, n_tiles, bwd_chunks):
    L = SCAN_CHUNK
    j = pl.program_id(1)
    per_tile = x_ref.shape[1] // L
    n_chunks = n_tiles * per_tile

    @pl.when(j == 0)
    def _():
        s_scr[...] = s0_ref[0]
        c_scr[...] = c0_ref[0]
        m_scr[...] = m0_ref[0]

    @pl.when(j < n_tiles)
    def _():
        rope_refs = (cos_ref, sin_ref, cost_ref, sint_ref)
        hb = _project_kv(x_ref, gain_ref, sh_ref, sc_ref, wk_ref, wvt_ref, gb_ref, rope_refs,
                         k_ref, vt_ref, gr_ref)
        _project_q(hb, wqt_ref, rope_refs, qt_ref)
        for ci in range(per_tile):
            _advance_state(rc_ref, s_scr, c_scr, m_scr, 0, *_chunk_sources(rc_ref, k_ref, vt_ref, gr_ref, ci, 0),
                           (sf_ref, cf_ref, mf_ref, ci))
            amax, b_last, u_ret, u_ml = _chunk_sources(rc_ref, k_ref, vt_ref, gr_ref, ci, 1)
            chunk = j * per_tile + ci
            ub_stat[chunk] = jnp.concatenate([amax, b_last], axis=0)
            for hd in range(HEADS):
                ub_ret[chunk, hd] = u_ret[hd].astype(BF16)
                ub_ml[chunk, hd] = u_ml[hd].astype(BF16)

    @pl.when(j >= n_tiles)
    def _():
        first = n_chunks - 1 - (j - n_tiles) * bwd_chunks
        for i in range(bwd_chunks):
            chunk = first - i
            stat = ub_stat[chunk]
            _advance_state(rc_ref, s_scr, c_scr, m_scr, 1, stat[0:HEADS], stat[HEADS:2 * HEADS],
                           [ub_ret[chunk, hd].astype(F32) for hd in range(HEADS)],
                           [ub_ml[chunk, hd].astype(F32) for hd in range(HEADS)],
                           (sb_ref, cb_ref, mb_ref, bwd_chunks - 1 - i))


def _proj_scan(ret_consts, x, gain, shift, scale, wqt, wk, wvt, gbias, rope, init):
    b, n, d = x.shape
    L = SCAN_CHUNK
    t = PROJ_TOKENS
    nt = n // t
    nc = n // L
    per_tile = t // L
    nbb = nc // BWD_CHUNKS
    tile = lambda j: jnp.minimum(j, nt - 1)
    bblk = lambda j: nbb - 1 - jnp.maximum(j - nt, 0)
    tok3 = lambda i, j: (i, tile(j), 0)
    chunk4 = lambda i, j: (i, tile(j), 0, 0)
    mod3 = lambda i, j: (i, 0, 0)
    per_b = lambda shape: pl.BlockSpec((1,) + shape, lambda i, j: (i,) + (0,) * len(shape))
    cos, sin, cos_t, sin_t = rope
    in_specs = [pl.BlockSpec(memory_space=pltpu.SMEM),
                pl.BlockSpec((1, t, d), tok3),
                _resident((1, d)),
                pl.BlockSpec((1, 1, d), mod3), pl.BlockSpec((1, 1, d), mod3),
                _resident(wqt.shape), _resident(wk.shape), _resident(wvt.shape), _resident(gbias.shape),
                pl.BlockSpec((t, HEAD_DIM), lambda i, j: (tile(j), 0)),
                pl.BlockSpec((t, HEAD_DIM), lambda i, j: (tile(j), 0)),
                pl.BlockSpec((HEAD_DIM, t), lambda i, j: (0, tile(j))),
                pl.BlockSpec((HEAD_DIM, t), lambda i, j: (0, tile(j))),
                per_b(S_SHAPE), per_b(C_SHAPE), per_b(M_SHAPE)]
    out_shape = [jax.ShapeDtypeStruct((b, nc, 2 * MIX_W, L), BF16),
                 jax.ShapeDtypeStruct((b, n, 2 * MIX_W), BF16),
                 jax.ShapeDtypeStruct((b, nc, 2 * MIX_W, L), BF16),
                 jax.ShapeDtypeStruct((b, nc, 16, L), F32)]
    out_specs = [pl.BlockSpec((1, per_tile, 2 * MIX_W, L), chunk4),
                 pl.BlockSpec((1, t, 2 * MIX_W), tok3),
                 pl.BlockSpec((1, per_tile, 2 * MIX_W, L), chunk4),
                 pl.BlockSpec((1, per_tile, 16, L), chunk4)]
    for blk, step in ((per_tile, tile), (BWD_CHUNKS, bblk)):
        out_shape += [jax.ShapeDtypeStruct((b, nc) + S_SHAPE[1:], BF16),
                      jax.ShapeDtypeStruct((b, nc) + C_SHAPE[1:], BF16),
                      jax.ShapeDtypeStruct((b, nc, HEADS, HEAD_DIM), F32)]
        out_specs += [pl.BlockSpec((1, blk) + S_SHAPE[1:], lambda i, j, step=step: (i, step(j), 0, 0, 0)),
                      pl.BlockSpec((1, blk) + C_SHAPE[1:], lambda i, j, step=step: (i, step(j), 0, 0, 0)),
                      pl.BlockSpec((1, blk, HEADS, HEAD_DIM), lambda i, j, step=step: (i, step(j), 0, 0))]
    return pl.pallas_call(
        functools.partial(_proj_scan_kernel, n_tiles=nt, bwd_chunks=BWD_CHUNKS),
        grid=(b, nt + nbb),
        in_specs=in_specs,
        out_specs=tuple(out_specs),
        out_shape=tuple(out_shape),
        scratch_shapes=[pltpu.VMEM(S_SHAPE, F32), pltpu.VMEM(C_SHAPE, F32), pltpu.VMEM(M_SHAPE, F32),
                        pltpu.VMEM((nc,) + S_SHAPE[1:], BF16), pltpu.VMEM((nc,) + C_SHAPE[1:], BF16),
                        pltpu.VMEM((nc,) + M_SHAPE, F32)],
        compiler_params=pltpu.CompilerParams(dimension_semantics=("arbitrary", "arbitrary"),
                                             vmem_limit_bytes=VMEM_LIMIT),
        name="proj_scan",
    )(ret_consts, x, gain, shift, scale, wqt, wk, wvt, gbias, cos, sin, cos_t, sin_t, *init)


def _mixer_kernel(rc_ref, x_ref, gain_ref, sh_ref, sc_ref, g1_ref, qt_ref, k_ref, vt_ref, gr_ref,
                  sf_ref, sb_ref, cf_ref, cb_ref, mf_ref, mb_ref,
                  wgt_ref, wbg_ref, wru_ref, wmu_ref, wout_ref, o_ref, dec_scr, qdec_scr):
    L = SCAN_CHUNK
    key = lax.broadcasted_iota(jnp.int32, (L, L), 0)
    qry = lax.broadcasted_iota(jnp.int32, (L, L), 1)

    @pl.when((pl.program_id(0) == 0) & (pl.program_id(1) == 0))
    def _():
        rel = (qry - key).astype(F32)
        lpos = lax.broadcasted_iota(jnp.int32, (8, L), 1).astype(F32)
        first_row = lax.broadcasted_iota(jnp.int32, (8, L), 0) == 0
        for hd in range(HEADS):
            lg_f = rc_ref[hd]
            lg_b = rc_ref[4 + hd]
            dec_scr[hd] = (jnp.where(rel >= 0, jnp.exp(lg_f * jnp.maximum(rel, 0.0)), 0.0)
                           + jnp.where(rel <= 0, jnp.exp(lg_b * jnp.maximum(-rel, 0.0)), 0.0))
            qdec_scr[hd] = jnp.where(first_row, jnp.exp(lg_f * (lpos + 1.0)), jnp.exp(lg_b * (L - lpos)))

    d_model = x_ref.shape[-1]
    bg_w = d_model // HEADS
    masks = (key <= qry, key >= qry)
    c_refs, m_refs = (cf_ref, cb_ref), (mf_ref, mb_ref)
    ones = jnp.ones((ONES_ROWS, L), BF16)
    neg_inf = jnp.float32(-jnp.inf)
    n_chunks = x_ref.shape[1] // L
    a_cols = [(gr_ref[0, ci, 0:8, :] * LOG2E).T for ci in range(n_chunks)]
    scanned, y_ret, y_ml, bg_ret, bg_ml = [], [], [], [], []
    normed = []

    def gate_head(hd):
        if not normed:
            normed.append((_rms(x_ref[0]) * gain_ref[...] * (1.0 + sc_ref[0]) + sh_ref[0]).astype(BF16))
        hb = normed[0]
        gates = _dot_nt(wgt_ref[2 * hd * HEAD_DIM:2 * (hd + 1) * HEAD_DIM, :], hb)
        bg = _dot(hb, wbg_ref[:, 2 * hd * bg_w:2 * (hd + 1) * bg_w])
        bg_ret.append(bg[:, :bg_w])
        bg_ml.append(bg[:, bg_w:])
        ret_t, ml_t = scanned[hd]
        y_ret.append(ret_t * _silu(gates[:HEAD_DIM, :]))
        y_ml.append(_rms(_sigmoid(gates[HEAD_DIM:, :]) * ml_t, axis=0))

    chunks = range(n_chunks)
    toks = [slice(ci * L, (ci + 1) * L) for ci in chunks]

    def first_wave(hd):
        sl = slice(hd * HEAD_DIM, (hd + 1) * HEAD_DIM)
        ml = slice(MIX_W + hd * HEAD_DIM, MIX_W + (hd + 1) * HEAD_DIM)
        qt_r = [qt_ref[0, ci, sl, :] for ci in chunks]
        qt_m = [qt_ref[0, ci, ml, :] for ci in chunks]
        s_ret = [_dot(k_ref[0, toks[ci], sl], qt_r[ci]) for ci in chunks]
        s_ml = [_dot(k_ref[0, toks[ci], ml], qt_m[ci]) for ci in chunks]
        inter_ret, inter_ml = [], []
        for ci in chunks:
            qf = qt_r[ci].astype(F32)
            qw = jnp.concatenate([qf * qdec_scr[hd, 0:1, :], qf * qdec_scr[hd, 1:2, :]], axis=0).astype(BF16)
            st = jnp.concatenate([sf_ref[0, ci, hd], sb_ref[0, ci, hd]], axis=1)
            inter_ret.append(_dot(st, qw))
            inter_ml.append([_dot(c_refs[d][0, ci, hd], qt_m[ci]) for d in range(2)])
        return s_ret, s_ml, inter_ret, inter_ml

    wave = first_wave(0)
    for hd in range(HEADS):
        sl = slice(hd * HEAD_DIM, (hd + 1) * HEAD_DIM)
        ml = slice(MIX_W + hd * HEAD_DIM, MIX_W + (hd + 1) * HEAD_DIM)
        s_ret, s_ml, inter_ret, inter_ml = wave
        if hd + 1 < HEADS:
            wave = first_wave(hd + 1)
        p_ret = [(s_ret[ci] * dec_scr[hd]).astype(BF16) for ci in chunks]
        p_ml, m_locs, m_ins = [], [], []
        for ci in chunks:
            for d in range(2):
                a_col = a_cols[ci][:, 4 * d + hd:4 * d + hd + 1]
                m_in = jnp.concatenate([m_refs[d][0, ci, hd:hd + 1, :] * LOG2E] * (L // HEAD_DIM), axis=1)
                a_masked = jnp.where(masks[d], a_col, neg_inf)
                m_loc = jnp.maximum(jnp.max(a_masked, axis=0, keepdims=True), m_in)
                p_ml.append((s_ml[ci] * jnp.exp2(a_masked - m_loc)).astype(BF16))
                m_locs.append(m_loc)
                m_ins.append(m_in)
        ret_cols, ml_cols = [], []
        for ci in chunks:
            ret_cols.append(_rms(_dot(vt_ref[0, ci, sl, :], p_ret[ci]) + inter_ret[ci], axis=0))
            v_ext = jnp.concatenate([vt_ref[0, ci, ml, :], ones], axis=0)
            hsum = None
            for d in range(2):
                m_loc, m_in = m_locs[2 * ci + d], m_ins[2 * ci + d]
                cum_row = gr_ref[0, ci, 8 + 4 * d + hd:9 + 4 * d + hd, :] * LOG2E
                tot = _dot(v_ext, p_ml[2 * ci + d]) + jnp.exp2(m_in - m_loc) * inter_ml[ci][d]
                den = jnp.maximum(jnp.abs(tot[HEAD_DIM:HEAD_DIM + 1, :]), jnp.exp2(-(cum_row + m_loc)))
                hd_out = tot[0:HEAD_DIM, :] * (1.0 / den)
                hsum = hd_out if hsum is None else hsum + hd_out
            ml_cols.append(hsum)
        scanned.append((jnp.concatenate(ret_cols, axis=1), jnp.concatenate(ml_cols, axis=1)))
        if hd >= 1:
            gate_head(hd - 1)
    gate_head(HEADS - 1)

    y_ret = jnp.concatenate(y_ret, axis=0).astype(BF16)
    y_ml = jnp.concatenate(y_ml, axis=0).astype(BF16)
    bg_ret = jnp.concatenate(bg_ret, axis=1)
    bg_ml = jnp.concatenate(bg_ml, axis=1)
    merged = _sigmoid(bg_ret) * _dot_tn(y_ret, wru_ref[...]) + _sigmoid(bg_ml) * _dot_tn(y_ml, wmu_ref[...])
    o_ref[0] = x_ref[0] + g1_ref[0] * _dot(merged.astype(BF16), wout_ref[...])


def _mixer(ret_consts, x, gain, shift, scale, gate1, qt, k, vt, gr, chunk_states, wgt, wbg, wru, wmu, wout):
    b, n, d = x.shape
    L = SCAN_CHUNK
    t = MIX_TOKENS
    ch = t // L
    sf, sb, cf, cb, mf, mb = chunk_states
    tok3 = lambda i, j: (i, j, 0)
    mod3 = lambda i, j: (i, 0, 0)
    st5 = lambda i, j: (i, j, 0, 0, 0)
    st4 = lambda i, j: (i, j, 0, 0)
    in_specs = [pl.BlockSpec(memory_space=pltpu.SMEM),
                pl.BlockSpec((1, t, d), tok3),
                _resident((1, d)),
                pl.BlockSpec((1, 1, d), mod3), pl.BlockSpec((1, 1, d), mod3), pl.BlockSpec((1, 1, d), mod3),
                pl.BlockSpec((1, ch, 2 * MIX_W, L), st4),
                pl.BlockSpec((1, t, 2 * MIX_W), tok3),
                pl.BlockSpec((1, ch, 2 * MIX_W, L), st4),
                pl.BlockSpec((1, ch, 16, L), st4),
                pl.BlockSpec((1, ch) + S_SHAPE[1:], st5),
                pl.BlockSpec((1, ch) + S_SHAPE[1:], st5),
                pl.BlockSpec((1, ch) + C_SHAPE[1:], st5),
                pl.BlockSpec((1, ch) + C_SHAPE[1:], st5),
                pl.BlockSpec((1, ch, HEADS, HEAD_DIM), st4),
                pl.BlockSpec((1, ch, HEADS, HEAD_DIM), st4),
                _resident(wgt.shape),
                _resident(wbg.shape),
                _resident(wru.shape),
                _resident(wmu.shape),
                _resident(wout.shape)]
    return pl.pallas_call(
        _mixer_kernel,
        grid=(b, n // t),
        in_specs=in_specs,
        out_specs=pl.BlockSpec((1, t, d), tok3),
        out_shape=jax.ShapeDtypeStruct((b, n, d), F32),
        scratch_shapes=[pltpu.VMEM((HEADS, L, L), F32), pltpu.VMEM((HEADS, 8, L), F32)],
        compiler_params=pltpu.CompilerParams(dimension_semantics=("arbitrary", "arbitrary"),
                                             vmem_limit_bytes=VMEM_LIMIT),
        name="mixer",
    )(ret_consts, x, gain, shift, scale, gate1, qt, k, vt, gr, sf, sb, cf, cb, mf, mb,
      wgt, wbg, wru, wmu, wout)


def _ffn_kernel(x_ref, gain_ref, sh_ref, sc_ref, g2_ref, fgain_ref, w1_ref, w2_ref, o_ref, *, hidden):
    xf = x_ref[0]
    h = _rms(xf) * gain_ref[...]
    hb = (h * (1.0 + sc_ref[0]) + sh_ref[0]).astype(BF16)
    acc = None
    start = 0
    for width in FFN_SPLITS:
        gate = _dot(hb, w1_ref[:, start:start + width])
        up = _dot(hb, w1_ref[:, hidden + start:hidden + start + width])
        act = (_silu(gate) * up).astype(BF16)
        part = _dot(act, w2_ref[start:start + width, :])
        acc = part if acc is None else acc + part
        start += width
    o_ref[0] = _rms(xf + g2_ref[0] * acc) * fgain_ref[...]


def _ffn(x, gain, shift, scale, gate2, final_gain, w1, w2):
    b, n, d = x.shape
    t = FFN_TOKENS
    hidden = w2.shape[0]
    assert sum(FFN_SPLITS) == hidden
    tok3 = lambda i, j: (i, j, 0)
    mod3 = lambda i, j: (i, 0, 0)
    return pl.pallas_call(
        functools.partial(_ffn_kernel, hidden=hidden),
        grid=(b, n // t),
        in_specs=[pl.BlockSpec((1, t, d), tok3),
                  _resident((1, d)),
                  pl.BlockSpec((1, 1, d), mod3), pl.BlockSpec((1, 1, d), mod3), pl.BlockSpec((1, 1, d), mod3),
                  _resident((1, d)),
                  _resident(w1.shape),
                  _resident(w2.shape)],
        out_specs=pl.BlockSpec((1, t, d), tok3),
        out_shape=jax.ShapeDtypeStruct((b, n, d), F32),
        compiler_params=pltpu.CompilerParams(dimension_semantics=("parallel", "parallel"),
                                             vmem_limit_bytes=VMEM_LIMIT),
        name="ffn",
    )(x, gain, shift, scale, gate2, final_gain, w1, w2)


def _rope_tables(n):
    n_rows = n // GRID_W
    rows = jnp.broadcast_to(jnp.arange(n_rows, dtype=F32)[:, None], (n_rows, GRID_W)).reshape(n)
    cols = jnp.broadcast_to(jnp.arange(GRID_W, dtype=F32)[None, :], (n_rows, GRID_W)).reshape(n)
    n_freq = HEAD_DIM // 4
    inv = ROPE_BASE ** (-jnp.arange(n_freq, dtype=F32) / n_freq)
    ang = jnp.concatenate([rows[:, None] * inv, cols[:, None] * inv], axis=-1)
    cos, sin = jnp.cos(ang), jnp.sin(ang)
    cos2 = jnp.concatenate([cos, cos], axis=-1)
    sin2 = jnp.concatenate([-sin, sin], axis=-1)
    return cos2, sin2, cos2.T, sin2.T


def kernel(x, c, ctx, c_ctx, w_ada, b_ada, norm1_gain, norm2_gain, w_in, mlstm_gate_bias, ret_decay_logit,
           w_ret_up, w_ml_up, w_out, w_ffn_in, w_ffn_out, final_gain):
    assert w_ada.shape[0] == 1, "single-layer block"
    b, n, d = x.shape
    assert n % PROJ_TOKENS == 0 and n % FFN_TOKENS == 0 and n % MIX_TOKENS == 0
    assert PROJ_TOKENS % SCAN_CHUNK == 0 and MIX_TOKENS % SCAN_CHUNK == 0 and ctx.shape[1] % SCAN_CHUNK == 0
    assert (n // SCAN_CHUNK) % BWD_CHUNKS == 0

    rows = -(-(b + 1) // 16) * 16
    cc = jnp.concatenate([c, c_ctx[None, :], jnp.zeros((rows - b - 1, d), F32)], axis=0)
    mod = _adaln(cc, w_ada[0], b_ada[0])
    sh1, sc1, g1, sh2, sc2, g2 = (mod[:b, i * d:(i + 1) * d].reshape(b, 1, d) for i in range(6))
    csh1 = jnp.broadcast_to(mod[b, 0:d].reshape(1, 1, d), (b, 1, d))
    csc1 = jnp.broadcast_to(mod[b, d:2 * d].reshape(1, 1, d), (b, 1, d))

    w = w_in[0]
    o = [0]
    for width in (MIX_W, MIX_W, MIX_W, MIX_W, MIX_W, MIX_W, MIX_W, MIX_W, 4 * HEADS, d, d):
        o.append(o[-1] + width)
    col = lambda i: w[:, o[i]:o[i + 1]]
    head = lambda t, hd, width: t[:, hd * width:(hd + 1) * width]
    gates = col(8).reshape(d, 4, HEADS)
    gates = jnp.concatenate([gates[:, 0], gates[:, 2], gates[:, 1], gates[:, 3]], axis=1)
    gb = mlstm_gate_bias[0]
    gbias = jnp.concatenate([gb[0], gb[2], gb[1], gb[3]]).reshape(4 * HEADS, 1).astype(F32)
    wqt = jnp.concatenate([col(0), col(4)], axis=1).T.astype(BF16)
    wk = jnp.concatenate([col(1), col(5)], axis=1).astype(BF16)
    wvt = jnp.concatenate([col(2), col(6), gates], axis=1).T.astype(BF16)
    bg_w = d // HEADS
    wgt = jnp.concatenate([piece for hd in range(HEADS)
                           for piece in (head(col(3), hd, HEAD_DIM), head(col(7), hd, HEAD_DIM))],
                          axis=1).T.astype(BF16)
    wbg = jnp.concatenate([piece for hd in range(HEADS)
                           for piece in (head(col(9), hd, bg_w), head(col(10), hd, bg_w))], axis=1).astype(BF16)

    log_gamma = jax.nn.log_sigmoid(ret_decay_logit[0].astype(F32)).reshape(2 * HEADS)
    ret_consts = jnp.concatenate([log_gamma, jnp.exp(log_gamma * SCAN_CHUNK)])

    gain1 = norm1_gain[0].reshape(1, d)
    ctx_final = _ctx_states(ret_consts, ctx, gain1, csh1, csc1, wk, wvt, gbias)
    qt_x, k_x, vt_x, gr_x, sf, cf, mf, sb, cb, mb = _proj_scan(
        ret_consts, x, gain1, sh1, sc1, wqt, wk, wvt, gbias, _rope_tables(n), ctx_final)
    x1 = _mixer(ret_consts, x, gain1, sh1, sc1, g1, qt_x, k_x, vt_x, gr_x, (sf, sb, cf, cb, mf, mb),
                wgt, wbg, w_ret_up[0].astype(BF16), w_ml_up[0].astype(BF16), w_out[0].astype(BF16))
    return _ffn(x1, norm2_gain[0].reshape(1, d), sh2, sc2, g2, final_gain.reshape(1, d),
                w_ffn_in[0].astype(BF16), w_ffn_out[0].astype(BF16))
```

```python
import functools

import jax
import jax.numpy as jnp
from jax import lax
from jax.experimental import pallas as pl
from jax.experimental.pallas import tpu as pltpu

HEADS = 4
HEAD_DIM = 128
MIX_W = HEADS * HEAD_DIM
ONES_ROWS = 16
GRID_W = 64
ROPE_BASE = 10000.0
EPS = 1e-6
LOG2E = 1.4426950408889634
SCAN_CHUNK = 256
CTX_GROUP = 4
PROJ_TOKENS = 1024
BWD_CHUNKS = 16
MIX_TOKENS = 1024
FFN_TOKENS = 1024
FFN_SPLITS = (1536, 1280)
VMEM_LIMIT = 60 * 1024 * 1024

F32 = jnp.float32
BF16 = jnp.bfloat16


def _resident(shape):
    return pl.BlockSpec(shape, lambda *_: (0,) * len(shape), pipeline_mode=pl.Buffered(1))


def _dot(a, b):
    return jnp.dot(a, b, preferred_element_type=F32)


def _dot_nt(a, b):
    return lax.dot_general(a, b, (((1,), (1,)), ((), ())), preferred_element_type=F32)


def _dot_tn(a, b):
    return lax.dot_general(a, b, (((0,), (0,)), ((), ())), preferred_element_type=F32)


def _sigmoid(t):
    return 0.5 * jnp.tanh(0.5 * t) + 0.5


def _silu(t):
    return t * _sigmoid(t)


def _log_sigmoid(t):
    return jnp.minimum(t, 0.0) - jnp.log1p(jnp.exp(-jnp.abs(t)))


def _rms(t, axis=-1):
    return t * lax.rsqrt(jnp.mean(t * t, axis=axis, keepdims=True) + EPS)


def _split3(t):
    hi = t.astype(BF16)
    r1 = t - hi.astype(F32)
    mid = r1.astype(BF16)
    lo = (r1 - mid.astype(F32)).astype(BF16)
    return hi, mid, lo


def _adaln_kernel(c_ref, w_ref, b_ref, o_ref):
    s = _silu(c_ref[...]).astype(BF16)
    o_ref[...] = _dot(s, w_ref[...].astype(BF16)) + b_ref[...]


def _adaln(cc, w, b):
    rows, d = cc.shape
    cols = w.shape[1]
    blk = 1536
    return pl.pallas_call(
        _adaln_kernel,
        grid=(cols // blk,),
        in_specs=[pl.BlockSpec((rows, d), lambda j: (0, 0)),
                  pl.BlockSpec((d, blk), lambda j: (0, j)),
                  pl.BlockSpec((1, blk), lambda j: (0, j))],
        out_specs=pl.BlockSpec((rows, blk), lambda j: (0, j)),
        out_shape=jax.ShapeDtypeStruct((rows, cols), F32),
        compiler_params=pltpu.CompilerParams(dimension_semantics=("parallel",),
                                             vmem_limit_bytes=VMEM_LIMIT),
        name="adaln",
    )(cc, w, b.reshape(1, cols))


def _project_q(hb, wqt_ref, rope_refs, qt_ref):
    L = SCAN_CHUNK
    n_chunks = hb.shape[0] // L
    qt = _dot_nt(wqt_ref[...], hb)
    for hd in range(HEADS):
        sl = slice(hd * HEAD_DIM, (hd + 1) * HEAD_DIM)
        t = qt[sl, :]
        if rope_refs is not None:
            cost_ref, sint_ref = rope_refs[2], rope_refs[3]
            half = HEAD_DIM // 2
            rot = jnp.concatenate([t[half:, :], t[:half, :]], axis=0)
            t = t * cost_ref[...] + rot * sint_ref[...]
        for ci in range(n_chunks):
            qt_ref[0, ci, sl, :] = t[:, ci * L:(ci + 1) * L].astype(BF16)
    for ci in range(n_chunks):
        qt_ref[0, ci, MIX_W:, :] = qt[MIX_W:, ci * L:(ci + 1) * L].astype(BF16)


def _project_kv(x_ref, gain_ref, sh_ref, sc_ref, wk_ref, wvt_ref, gb_ref, rope_refs, k_ref, vt_ref, gr_ref):
    L = SCAN_CHUNK
    tokens = x_ref.shape[1]
    n_chunks = tokens // L
    h = _rms(x_ref[0]) * gain_ref[...]
    h = h * (1.0 + sc_ref[0]) + sh_ref[0]
    hb = h.astype(BF16)

    vg = _dot_nt(wvt_ref[...], hb)
    for ci in range(n_chunks):
        vt_ref[0, ci] = vg[0:2 * MIX_W, ci * L:(ci + 1) * L].astype(BF16)

    g = vg[2 * MIX_W:, :] + gb_ref[...]
    i_pre = g[0:8, :]
    log_f = _log_sigmoid(g[8:16, :])
    src = lax.broadcasted_iota(jnp.int32, (L, L), 0)
    dst = lax.broadcasted_iota(jnp.int32, (L, L), 1)
    prefix_m = (src <= dst).astype(BF16)
    suffix_m = (src >= dst).astype(BF16)
    is_fwd = lax.broadcasted_iota(jnp.int32, (8, 1), 0) < HEADS
    parts = _split3(jnp.concatenate([log_f, log_f], axis=0))
    stacked = jnp.concatenate([p[:, ci * L:(ci + 1) * L] for ci in range(n_chunks) for p in parts], axis=0)
    pre_all = _dot(stacked, prefix_m)
    suf_all = _dot(stacked, suffix_m)
    for ci in range(n_chunks):
        cs = slice(ci * L, (ci + 1) * L)
        pieces = [slice((3 * ci + i) * 16, (3 * ci + i) * 16 + 8) for i in range(3)]
        pre = sum(pre_all[rows] for rows in pieces)
        suf = sum(suf_all[rows] for rows in pieces)
        cum = jnp.where(is_fwd, pre, suf)
        gr_ref[0, ci] = jnp.concatenate([i_pre[:, cs] - cum, cum], axis=0)

    k = _dot(hb, wk_ref[...]) * (HEAD_DIM ** -0.5)
    for hd in range(HEADS):
        sl = slice(hd * HEAD_DIM, (hd + 1) * HEAD_DIM)
        t = k[:, sl]
        if rope_refs is not None:
            cos_ref, sin_ref = rope_refs[0], rope_refs[1]
            t = t * cos_ref[...] + pltpu.roll(t, HEAD_DIM // 2, 1) * sin_ref[...]
        k_ref[0, :, sl] = t.astype(BF16)
    k_ref[0, :, MIX_W:] = k[:, MIX_W:].astype(BF16)
    return hb


S_SHAPE = (2, HEADS, HEAD_DIM, HEAD_DIM)
C_SHAPE = (2, HEADS, HEAD_DIM + ONES_ROWS, HEAD_DIM)
M_SHAPE = (2 * HEADS, HEAD_DIM)


def _chunk_sources(rc_ref, k_ref, vt_ref, gr_ref, ci, d):
    L = SCAN_CHUNK
    cs = slice(ci * L, (ci + 1) * L)
    pos = lax.broadcasted_iota(jnp.int32, (1, L), 1).astype(F32)
    ones = jnp.ones((ONES_ROWS, L), BF16)
    a = gr_ref[0, ci, 4 * d:4 * d + 4, :]
    cum = gr_ref[0, ci, 8 + 4 * d:12 + 4 * d, :]
    edge = cum[:, L - 1:L] if d == 0 else cum[:, 0:1]
    b_last = jnp.broadcast_to(edge, (HEADS, HEAD_DIM))
    amax = jnp.broadcast_to(jnp.max(a, axis=1, keepdims=True), (HEADS, HEAD_DIM))
    w_loc = jnp.exp(a - jnp.concatenate([amax] * (L // HEAD_DIM), axis=1))
    u_ret, u_ml = [], []
    for hd in range(HEADS):
        sl = slice(hd * HEAD_DIM, (hd + 1) * HEAD_DIM)
        ml = slice(MIX_W + hd * HEAD_DIM, MIX_W + (hd + 1) * HEAD_DIM)
        lg = rc_ref[4 * d + hd]
        to_end = jnp.exp(lg * ((L - 1.0) - pos)) if d == 0 else jnp.exp(lg * pos)
        vw = (vt_ref[0, ci, sl, :].astype(F32) * to_end).astype(BF16)
        u_ret.append(_dot(vw, k_ref[0, cs, sl]))
        v_ext = jnp.concatenate([vt_ref[0, ci, ml, :], ones], axis=0)
        vw = (v_ext.astype(F32) * w_loc[hd:hd + 1, :]).astype(BF16)
        u_ml.append(_dot(vw, k_ref[0, cs, ml]))
    return amax, b_last, u_ret, u_ml


def _advance_state(rc_ref, s_scr, c_scr, m_scr, d, amax, b_last, u_ret, u_ml, emit):
    rows = slice(HEADS * d, HEADS * (d + 1))
    m_old = m_scr[rows, :]
    m_mid = jnp.maximum(m_old, amax)
    w_old = jnp.exp(m_old - m_mid)
    w_new = jnp.exp(amax - m_mid)
    if emit is not None:
        s_out, c_out, m_out, slot = emit
        m_out[0, slot] = m_old
    m_scr[rows, :] = b_last + m_mid
    for hd in range(HEADS):
        s_prev = s_scr[d, hd]
        c_prev = c_scr[d, hd]
        if emit is not None:
            s_out[0, slot, hd] = s_prev.astype(BF16)
            c_out[0, slot, hd] = c_prev.astype(BF16)
        s_scr[d, hd] = rc_ref[2 * HEADS + HEADS * d + hd] * s_prev + u_ret[hd]
        c_scr[d, hd] = w_old[hd:hd + 1, :] * c_prev + w_new[hd:hd + 1, :] * u_ml[hd]


def _ctx_kernel(rc_ref, x_ref, gain_ref, sh_ref, sc_ref, wk_ref, wvt_ref, gb_ref,
                s_fin, c_fin, m_fin, k_scr, vt_scr, gr_scr, s_scr, c_scr, m_scr, *, group):
    L = SCAN_CHUNK
    _project_kv(x_ref, gain_ref, sh_ref, sc_ref, wk_ref, wvt_ref, gb_ref, None, k_scr, vt_scr, gr_scr)
    per_element = x_ref.shape[1] // L // group
    for g in range(group):
        s_scr[...] = jnp.zeros_like(s_scr)
        c_scr[...] = jnp.zeros_like(c_scr)
        m_scr[...] = jnp.zeros_like(m_scr)
        own = range(g * per_element, (g + 1) * per_element)
        for d in range(2):
            for ci in (own if d == 0 else reversed(own)):
                _advance_state(rc_ref, s_scr, c_scr, m_scr, d,
                               *_chunk_sources(rc_ref, k_scr, vt_scr, gr_scr, ci, d), None)
        s_fin[g] = s_scr[...]
        c_fin[g] = c_scr[...]
        m_fin[g] = m_scr[...]


def _ctx_states(ret_consts, ctx, gain, shift, scale, wk, wvt, gbias):
    b, n, d = ctx.shape
    group = CTX_GROUP if b % CTX_GROUP == 0 else 1
    tokens = group * n
    nck = tokens // SCAN_CHUNK
    blk = lambda shape: pl.BlockSpec((group,) + shape, lambda i: (i,) + (0,) * len(shape))
    return pl.pallas_call(
        functools.partial(_ctx_kernel, group=group),
        grid=(b // group,),
        in_specs=[pl.BlockSpec(memory_space=pltpu.SMEM),
                  pl.BlockSpec((1, tokens, d), lambda i: (i, 0, 0)),
                  _resident((1, d)), _resident((1, 1, d)), _resident((1, 1, d)),
                  _resident(wk.shape), _resident(wvt.shape), _resident(gbias.shape)],
        out_specs=(blk(S_SHAPE), blk(C_SHAPE), blk(M_SHAPE)),
        out_shape=tuple(jax.ShapeDtypeStruct((b,) + shape, F32) for shape in (S_SHAPE, C_SHAPE, M_SHAPE)),
        scratch_shapes=[pltpu.VMEM((1, tokens, 2 * MIX_W), BF16),
                        pltpu.VMEM((1, nck, 2 * MIX_W, SCAN_CHUNK), BF16),
                        pltpu.VMEM((1, nck, 16, SCAN_CHUNK), F32),
                        pltpu.VMEM(S_SHAPE, F32), pltpu.VMEM(C_SHAPE, F32), pltpu.VMEM(M_SHAPE, F32)],
        compiler_params=pltpu.CompilerParams(dimension_semantics=("arbitrary",),
                                             vmem_limit_bytes=VMEM_LIMIT),
        name="ctx_states",
    )(ret_consts, ctx.reshape(b // group, tokens, d), gain, shift, scale, wk, wvt, gbias)


def _proj_scan_kernel(rc_ref, x_ref, gain_ref, sh_ref, sc_ref, wqt_ref, wk_ref, wvt_ref, gb_ref,
                      cos_ref, sin_ref, cost_ref, sint_ref, s0_ref, c0_ref, m0_ref,
                      qt_ref, k_ref, vt_ref, gr_ref, sf_ref, cf_ref, mf_ref, sb_ref, cb_ref, mb_ref,
                      s_scr, c_scr, m_scr, ub_ret, ub_ml, ub_stat, *, n_tiles, bwd_chunks):
    L = SCAN_CHUNK
    j = pl.program_id(1)
    per_tile = x_ref.shape[1] // L
    n_chunks = n_tiles * per_tile

    @pl.when(j == 0)
    def _():
        s_scr[...] = s0_ref[0]
        c_scr[...] = c0_ref[0]
        m_scr[...] = m0_ref[0]

    @pl.when(j < n_tiles)
    def _():
        rope_refs = (cos_ref, sin_ref, cost_ref, sint_ref)
        hb = _project_kv(x_ref, gain_ref, sh_ref, sc_ref, wk_ref, wvt_ref, gb_ref, rope_refs,
                         k_ref, vt_ref, gr_ref)
        _project_q(hb, wqt_ref, rope_refs, qt_ref)
        for ci in range(per_tile):
            _advance_state(rc_ref, s_scr, c_scr, m_scr, 0, *_chunk_sources(rc_ref, k_ref, vt_ref, gr_ref, ci, 0),
                           (sf_ref, cf_ref, mf_ref, ci))
            amax, b_last, u_ret, u_ml = _chunk_sources(rc_ref, k_ref, vt_ref, gr_ref, ci, 1)
            chunk = j * per_tile + ci
            ub_stat[chunk] = jnp.concatenate([amax, b_last], axis=0)
            for hd in range(HEADS):
                ub_ret[chunk, hd] = u_ret[hd].astype(BF16)
                ub_ml[chunk, hd] = u_ml[hd].astype(BF16)

    @pl.when(j >= n_tiles)
    def _():
        first = n_chunks - 1 - (j - n_tiles) * bwd_chunks
        for i in range(bwd_chunks):
            chunk = first - i
            stat = ub_stat[chunk]
            _advance_state(rc_ref, s_scr, c_scr, m_scr, 1, stat[0:HEADS], stat[HEADS:2 * HEADS],
                           [ub_ret[chunk, hd].astype(F32) for hd in range(HEADS)],
                           [ub_ml[chunk, hd].astype(F32) for hd in range(HEADS)],
                           (sb_ref, cb_ref, mb_ref, bwd_chunks - 1 - i))


def _proj_scan(ret_consts, x, gain, shift, scale, wqt, wk, wvt, gbias, rope, init):
    b, n, d = x.shape
    L = SCAN_CHUNK
    t = PROJ_TOKENS
    nt = n // t
    nc = n // L
    per_tile = t // L
    nbb = nc // BWD_CHUNKS
    tile = lambda j: jnp.minimum(j, nt - 1)
    bblk = lambda j: nbb - 1 - jnp.maximum(j - nt, 0)
    tok3 = lambda i, j: (i, tile(j), 0)
    chunk4 = lambda i, j: (i, tile(j), 0, 0)
    mod3 = lambda i, j: (i, 0, 0)
    per_b = lambda shape: pl.BlockSpec((1,) + shape, lambda i, j: (i,) + (0,) * len(shape))
    cos, sin, cos_t, sin_t = rope
    in_specs = [pl.BlockSpec(memory_space=pltpu.SMEM),
                pl.BlockSpec((1, t, d), tok3),
                _resident((1, d)),
                pl.BlockSpec((1, 1, d), mod3), pl.BlockSpec((1, 1, d), mod3),
                _resident(wqt.shape), _resident(wk.shape), _resident(wvt.shape), _resident(gbias.shape),
                pl.BlockSpec((t, HEAD_DIM), lambda i, j: (tile(j), 0)),
                pl.BlockSpec((t, HEAD_DIM), lambda i, j: (tile(j), 0)),
                pl.BlockSpec((HEAD_DIM, t), lambda i, j: (0, tile(j))),
                pl.BlockSpec((HEAD_DIM, t), lambda i, j: (0, tile(j))),
                per_b(S_SHAPE), per_b(C_SHAPE), per_b(M_SHAPE)]
    out_shape = [jax.ShapeDtypeStruct((b, nc, 2 * MIX_W, L), BF16),
                 jax.ShapeDtypeStruct((b, n, 2 * MIX_W), BF16),
                 jax.ShapeDtypeStruct((b, nc, 2 * MIX_W, L), BF16),
                 jax.ShapeDtypeStruct((b, nc, 16, L), F32)]
    out_specs = [pl.BlockSpec((1, per_tile, 2 * MIX_W, L), chunk4),
                 pl.BlockSpec((1, t, 2 * MIX_W), tok3),
                 pl.BlockSpec((1, per_tile, 2 * MIX_W, L), chunk4),
                 pl.BlockSpec((1, per_tile, 16, L), chunk4)]
    for blk, step in ((per_tile, tile), (BWD_CHUNKS, bblk)):
        out_shape += [jax.ShapeDtypeStruct((b, nc) + S_SHAPE[1:], BF16),
                      jax.ShapeDtypeStruct((b, nc) + C_SHAPE[1:], BF16),
                      jax.ShapeDtypeStruct((b, nc, HEADS, HEAD_DIM), F32)]
        out_specs += [pl.BlockSpec((1, blk) + S_SHAPE[1:], lambda i, j, step=step: (i, step(j), 0, 0, 0)),
                      pl.BlockSpec((1, blk) + C_SHAPE[1:], lambda i, j, step=step: (i, step(j), 0, 0, 0)),
                      pl.BlockSpec((1, blk, HEADS, HEAD_DIM), lambda i, j, step=step: (i, step(j), 0, 0))]
    return pl.pallas_call(
        functools.partial(_proj_scan_kernel, n_tiles=nt, bwd_chunks=BWD_CHUNKS),
        grid=(b, nt + nbb),
        in_specs=in_specs,
        out_specs=tuple(out_specs),
        out_shape=tuple(out_shape),
        scratch_shapes=[pltpu.VMEM(S_SHAPE, F32), pltpu.VMEM(C_SHAPE, F32), pltpu.VMEM(M_SHAPE, F32),
                        pltpu.VMEM((nc,) + S_SHAPE[1:], BF16), pltpu.VMEM((nc,) + C_SHAPE[1:], BF16),
                        pltpu.VMEM((nc,) + M_SHAPE, F32)],
        compiler_params=pltpu.CompilerParams(dimension_semantics=("arbitrary", "arbitrary"),
                                             vmem_limit_bytes=VMEM_LIMIT),
        name="proj_scan",
    )(ret_consts, x, gain, shift, scale, wqt, wk, wvt, gbias, cos, sin, cos_t, sin_t, *init)


def _mixer_kernel(rc_ref, x_ref, gain_ref, sh_ref, sc_ref, g1_ref, qt_ref, k_ref, vt_ref, gr_ref,
                  sf_ref, sb_ref, cf_ref, cb_ref, mf_ref, mb_ref,
                  wgt_ref, wbg_ref, wru_ref, wmu_ref, wout_ref, o_ref, dec_scr, qdec_scr):
    L = SCAN_CHUNK
    key = lax.broadcasted_iota(jnp.int32, (L, L), 0)
    qry = lax.broadcasted_iota(jnp.int32, (L, L), 1)

    @pl.when((pl.program_id(0) == 0) & (pl.program_id(1) == 0))
    def _():
        rel = (qry - key).astype(F32)
        lpos = lax.broadcasted_iota(jnp.int32, (8, L), 1).astype(F32)
        first_row = lax.broadcasted_iota(jnp.int32, (8, L), 0) == 0
        for hd in range(HEADS):
            lg_f = rc_ref[hd]
            lg_b = rc_ref[4 + hd]
            dec_scr[hd] = (jnp.where(rel >= 0, jnp.exp(lg_f * jnp.maximum(rel, 0.0)), 0.0)
                           + jnp.where(rel <= 0, jnp.exp(lg_b * jnp.maximum(-rel, 0.0)), 0.0))
            qdec_scr[hd] = jnp.where(first_row, jnp.exp(lg_f * (lpos + 1.0)), jnp.exp(lg_b * (L - lpos)))

    d_model = x_ref.shape[-1]
    bg_w = d_model // HEADS
    masks = (key <= qry, key >= qry)
    c_refs, m_refs = (cf_ref, cb_ref), (mf_ref, mb_ref)
    ones = jnp.ones((ONES_ROWS, L), BF16)
    neg_inf = jnp.float32(-jnp.inf)
    n_chunks = x_ref.shape[1] // L
    a_cols = [(gr_ref[0, ci, 0:8, :] * LOG2E).T for ci in range(n_chunks)]
    scanned, y_ret, y_ml, bg_ret, bg_ml = [], [], [], [], []
    normed = []

    def gate_head(hd):
        if not normed:
            normed.append((_rms(x_ref[0]) * gain_ref[...] * (1.0 + sc_ref[0]) + sh_ref[0]).astype(BF16))
        hb = normed[0]
        gates = _dot_nt(wgt_ref[2 * hd * HEAD_DIM:2 * (hd + 1) * HEAD_DIM, :], hb)
        bg = _dot(hb, wbg_ref[:, 2 * hd * bg_w:2 * (hd + 1) * bg_w])
        bg_ret.append(bg[:, :bg_w])
        bg_ml.append(bg[:, bg_w:])
        ret_t, ml_t = scanned[hd]
        y_ret.append(ret_t * _silu(gates[:HEAD_DIM, :]))
        y_ml.append(_rms(_sigmoid(gates[HEAD_DIM:, :]) * ml_t, axis=0))

    chunks = range(n_chunks)
    toks = [slice(ci * L, (ci + 1) * L) for ci in chunks]

    def first_wave(hd):
        sl = slice(hd * HEAD_DIM, (hd + 1) * HEAD_DIM)
        ml = slice(MIX_W + hd * HEAD_DIM, MIX_W + (hd + 1) * HEAD_DIM)
        qt_r = [qt_ref[0, ci, sl, :] for ci in chunks]
        qt_m = [qt_ref[0, ci, ml, :] for ci in chunks]
        s_ret = [_dot(k_ref[0, toks[ci], sl], qt_r[ci]) for ci in chunks]
        s_ml = [_dot(k_ref[0, toks[ci], ml], qt_m[ci]) for ci in chunks]
        inter_ret, inter_ml = [], []
        for ci in chunks:
            qf = qt_r[ci].astype(F32)
            qw = jnp.concatenate([qf * qdec_scr[hd, 0:1, :], qf * qdec_scr[hd, 1:2, :]], axis=0).astype(BF16)
            st = jnp.concatenate([sf_ref[0, ci, hd], sb_ref[0, ci, hd]], axis=1)
            inter_ret.append(_dot(st, qw))
            inter_ml.append([_dot(c_refs[d][0, ci, hd], qt_m[ci]) for d in range(2)])
        return s_ret, s_ml, inter_ret, inter_ml

    wave = first_wave(0)
    for hd in range(HEADS):
        sl = slice(hd * HEAD_DIM, (hd + 1) * HEAD_DIM)
        ml = slice(MIX_W + hd * HEAD_DIM, MIX_W + (hd + 1) * HEAD_DIM)
        s_ret, s_ml, inter_ret, inter_ml = wave
        if hd + 1 < HEADS:
            wave = first_wave(hd + 1)
        p_ret = [(s_ret[ci] * dec_scr[hd]).astype(BF16) for ci in chunks]
        p_ml, m_locs, m_ins = [], [], []
        for ci in chunks:
            for d in range(2):
                a_col = a_cols[ci][:, 4 * d + hd:4 * d + hd + 1]
                m_in = jnp.concatenate([m_refs[d][0, ci, hd:hd + 1, :] * LOG2E] * (L // HEAD_DIM), axis=1)
                a_masked = jnp.where(masks[d], a_col, neg_inf)
                m_loc = jnp.maximum(jnp.max(a_masked, axis=0, keepdims=True), m_in)
                p_ml.append((s_ml[ci] * jnp.exp2(a_masked - m_loc)).astype(BF16))
                m_locs.append(m_loc)
                m_ins.append(m_in)
        ret_cols, ml_cols = [], []
        for ci in chunks:
            ret_cols.append(_rms(_dot(vt_ref[0, ci, sl, :], p_ret[ci]) + inter_ret[ci], axis=0))
            v_ext = jnp.concatenate([vt_ref[0, ci, ml, :], ones], axis=0)
            hsum = None
            for d in range(2):
                m_loc, m_in = m_locs[2 * ci + d], m_ins[2 * ci + d]
                cum_row = gr_ref[0, ci, 8 + 4 * d + hd:9 + 4 * d + hd, :] * LOG2E
                tot = _dot(v_ext, p_ml[2 * ci + d]) + jnp.exp2(m_in - m_loc) * inter_ml[ci][d]
                den = jnp.maximum(jnp.abs(tot[HEAD_DIM:HEAD_DIM + 1, :]), jnp.exp2(-(cum_row + m_loc)))
                hd_out = tot[0:HEAD_DIM, :] * (1.0 / den)
                hsum = hd_out if hsum is None else hsum + hd_out
            ml_cols.append(hsum)
        scanned.append((jnp.concatenate(ret_cols, axis=1), jnp.concatenate(ml_cols, axis=1)))
        if hd >= 1:
            gate_head(hd - 1)
    gate_head(HEADS - 1)

    y_ret = jnp.concatenate(y_ret, axis=0).astype(BF16)
    y_ml = jnp.concatenate(y_ml, axis=0).astype(BF16)
    bg_ret = jnp.concatenate(bg_ret, axis=1)
    bg_ml = jnp.concatenate(bg_ml, axis=1)
    merged = _sigmoid(bg_ret) * _dot_tn(y_ret, wru_ref[...]) + _sigmoid(bg_ml) * _dot_tn(y_ml, wmu_ref[...])
    o_ref[0] = x_ref[0] + g1_ref[0] * _dot(merged.astype(BF16), wout_ref[...])


def _mixer(ret_consts, x, gain, shift, scale, gate1, qt, k, vt, gr, chunk_states, wgt, wbg, wru, wmu, wout):
    b, n, d = x.shape
    L = SCAN_CHUNK
    t = MIX_TOKENS
    ch = t // L
    sf, sb, cf, cb, mf, mb = chunk_states
    tok3 = lambda i, j: (i, j, 0)
    mod3 = lambda i, j: (i, 0, 0)
    st5 = lambda i, j: (i, j, 0, 0, 0)
    st4 = lambda i, j: (i, j, 0, 0)
    in_specs = [pl.BlockSpec(memory_space=pltpu.SMEM),
                pl.BlockSpec((1, t, d), tok3),
                _resident((1, d)),
                pl.BlockSpec((1, 1, d), mod3), pl.BlockSpec((1, 1, d), mod3), pl.BlockSpec((1, 1, d), mod3),
                pl.BlockSpec((1, ch, 2 * MIX_W, L), st4),
                pl.BlockSpec((1, t, 2 * MIX_W), tok3),
                pl.BlockSpec((1, ch, 2 * MIX_W, L), st4),
                pl.BlockSpec((1, ch, 16, L), st4),
                pl.BlockSpec((1, ch) + S_SHAPE[1:], st5),
                pl.BlockSpec((1, ch) + S_SHAPE[1:], st5),
                pl.BlockSpec((1, ch) + C_SHAPE[1:], st5),
                pl.BlockSpec((1, ch) + C_SHAPE[1:], st5),
                pl.BlockSpec((1, ch, HEADS, HEAD_DIM), st4),
                pl.BlockSpec((1, ch, HEADS, HEAD_DIM), st4),
                _resident(wgt.shape),
                _resident(wbg.shape),
                _resident(wru.shape),
                _resident(wmu.shape),
                _resident(wout.shape)]
    return pl.pallas_call(
        _mixer_kernel,
        grid=(b, n // t),
        in_specs=in_specs,
        out_specs=pl.BlockSpec((1, t, d), tok3),
        out_shape=jax.ShapeDtypeStruct((b, n, d), F32),
        scratch_shapes=[pltpu.VMEM((HEADS, L, L), F32), pltpu.VMEM((HEADS, 8, L), F32)],
        compiler_params=pltpu.CompilerParams(dimension_semantics=("arbitrary", "arbitrary"),
                                             vmem_limit_bytes=VMEM_LIMIT),
        name="mixer",
    )(ret_consts, x, gain, shift, scale, gate1, qt, k, vt, gr, sf, sb, cf, cb, mf, mb,
      wgt, wbg, wru, wmu, wout)


def _ffn_kernel(x_ref, gain_ref, sh_ref, sc_ref, g2_ref, fgain_ref, w1_ref, w2_ref, o_ref, *, hidden):
    xf = x_ref[0]
    h = _rms(xf) * gain_ref[...]
    hb = (h * (1.0 + sc_ref[0]) + sh_ref[0]).astype(BF16)
    acc = None
    start = 0
    for width in FFN_SPLITS:
        gate = _dot(hb, w1_ref[:, start:start + width])
        up = _dot(hb, w1_ref[:, hidden + start:hidden + start + width])
        act = (_silu(gate) * up).astype(BF16)
        part = _dot(act, w2_ref[start:start + width, :])
        acc = part if acc is None else acc + part
        start += width
    o_ref[0] = _rms(xf + g2_ref[0] * acc) * fgain_ref[...]


def _ffn(x, gain, shift, scale, gate2, final_gain, w1, w2):
    b, n, d = x.shape
    t = FFN_TOKENS
    hidden = w2.shape[0]
    assert sum(FFN_SPLITS) == hidden
    tok3 = lambda i, j: (i, j, 0)
    mod3 = lambda i, j: (i, 0, 0)
    return pl.pallas_call(
        functools.partial(_ffn_kernel, hidden=hidden),
        grid=(b, n // t),
        in_specs=[pl.BlockSpec((1, t, d), tok3),
                  _resident((1, d)),
                  pl.BlockSpec((1, 1, d), mod3), pl.BlockSpec((1, 1, d), mod3), pl.BlockSpec((1, 1, d), mod3),
                  _resident((1, d)),
                  _resident(w1.shape),
                  _resident(w2.shape)],
        out_specs=pl.BlockSpec((1, t, d), tok3),
        out_shape=jax.ShapeDtypeStruct((b, n, d), F32),
        compiler_params=pltpu.CompilerParams(dimension_semantics=("parallel", "parallel"),
                                             vmem_limit_bytes=VMEM_LIMIT),
        name="ffn",
    )(x, gain, shift, scale, gate2, final_gain, w1, w2)


def _rope_tables(n):
    pos = jnp.arange(n, dtype=jnp.int32)
    rows = (pos // GRID_W).astype(F32)
    cols = (pos % GRID_W).astype(F32)
    n_freq = HEAD_DIM // 4
    inv = ROPE_BASE ** (-jnp.arange(n_freq, dtype=F32) / n_freq)
    ang_t = jnp.concatenate([inv[:, None] * rows[None, :], inv[:, None] * cols[None, :]], axis=0)
    cos_t, sin_t = jnp.cos(ang_t), jnp.sin(ang_t)
    cos2_t = jnp.concatenate([cos_t, cos_t], axis=0)
    sin2_t = jnp.concatenate([-sin_t, sin_t], axis=0)
    return cos2_t.T, sin2_t.T, cos2_t, sin2_t


def kernel(x, c, ctx, c_ctx, w_ada, b_ada, norm1_gain, norm2_gain, w_in, mlstm_gate_bias, ret_decay_logit,
           w_ret_up, w_ml_up, w_out, w_ffn_in, w_ffn_out, final_gain):
    assert w_ada.shape[0] == 1, "single-layer block"
    b, n, d = x.shape
    assert n % PROJ_TOKENS == 0 and n % FFN_TOKENS == 0 and n % MIX_TOKENS == 0
    assert PROJ_TOKENS % SCAN_CHUNK == 0 and MIX_TOKENS % SCAN_CHUNK == 0 and ctx.shape[1] % SCAN_CHUNK == 0
    assert (n // SCAN_CHUNK) % BWD_CHUNKS == 0

    rows = -(-(b + 1) // 16) * 16
    cc = jnp.concatenate([c, c_ctx[None, :], jnp.zeros((rows - b - 1, d), F32)], axis=0)
    mod = _adaln(cc, w_ada[0], b_ada[0])
    sh1, sc1, g1, sh2, sc2, g2 = (mod[:b, i * d:(i + 1) * d].reshape(b, 1, d) for i in range(6))
    csh1 = mod[b, 0:d].reshape(1, 1, d)
    csc1 = mod[b, d:2 * d].reshape(1, 1, d)

    w = w_in[0]
    o = [0]
    for width in (MIX_W, MIX_W, MIX_W, MIX_W, MIX_W, MIX_W, MIX_W, MIX_W, 4 * HEADS, d, d):
        o.append(o[-1] + width)
    col = lambda i: w[:, o[i]:o[i + 1]]
    head = lambda t, hd, width: t[:, hd * width:(hd + 1) * width]
    gates = col(8).reshape(d, 4, HEADS)
    gates = jnp.concatenate([gates[:, 0], gates[:, 2], gates[:, 1], gates[:, 3]], axis=1)
    gb = mlstm_gate_bias[0]
    gbias = jnp.concatenate([gb[0], gb[2], gb[1], gb[3]]).reshape(4 * HEADS, 1).astype(F32)
    wqt = jnp.concatenate([col(0), col(4)], axis=1).T.astype(BF16)
    wk = jnp.concatenate([col(1), col(5)], axis=1).astype(BF16)
    wvt = jnp.concatenate([col(2), col(6), gates], axis=1).T.astype(BF16)
    bg_w = d // HEADS
    wgt = jnp.concatenate([piece for hd in range(HEADS)
                           for piece in (head(col(3), hd, HEAD_DIM), head(col(7), hd, HEAD_DIM))],
                          axis=1).T.astype(BF16)
    wbg = jnp.concatenate([piece for hd in range(HEADS)
                           for piece in (head(col(9), hd, bg_w), head(col(10), hd, bg_w))], axis=1).astype(BF16)

    log_gamma = jax.nn.log_sigmoid(ret_decay_logit[0].astype(F32)).reshape(2 * HEADS)
    ret_consts = jnp.concatenate([log_gamma, jnp.exp(log_gamma * SCAN_CHUNK)])

    gain1 = norm1_gain[0].reshape(1, d)
    ctx_final = _ctx_states(ret_consts, ctx, gain1, csh1, csc1, wk, wvt, gbias)
    qt_x, k_x, vt_x, gr_x, sf, cf, mf, sb, cb, mb = _proj_scan(
        ret_consts, x, gain1, sh1, sc1, wqt, wk, wvt, gbias, _rope_tables(n), ctx_final)
    x1 = _mixer(ret_consts, x, gain1, sh1, sc1, g1, qt_x, k_x, vt_x, gr_x, (sf, sb, cf, cb, mf, mb),
                wgt, wbg, w_ret_up[0].astype(BF16), w_ml_up[0].astype(BF16), w_out[0].astype(BF16))
    return _ffn(x1, norm2_gain[0].reshape(1, d), sh2, sc2, g2, final_gain.reshape(1, d),
                w_ffn_in[0].astype(BF16), w_ffn_out[0].astype(BF16))
```

```python
import functools

import jax
import jax.numpy as jnp
from jax import lax
from jax.experimental import pallas as pl
from jax.experimental.pallas import tpu as pltpu

HEADS = 4
HEAD_DIM = 128
MIX_W = HEADS * HEAD_DIM
ONES_ROWS = 16
GRID_W = 64
ROPE_BASE = 10000.0
EPS = 1e-6
LOG2E = 1.4426950408889634
SCAN_CHUNK = 256
CTX_GROUP = 4
PROJ_TOKENS = 1024
BWD_CHUNKS = 16
MIX_TOKENS = 1024
FFN_TOKENS = 1024
FFN_SPLITS = (1536, 1280)
VMEM_LIMIT = 60 * 1024 * 1024

F32 = jnp.float32
BF16 = jnp.bfloat16


def _resident(shape):
    return pl.BlockSpec(shape, lambda *_: (0,) * len(shape), pipeline_mode=pl.Buffered(1))


def _dot(a, b):
    return jnp.dot(a, b, preferred_element_type=F32)


def _dot_nt(a, b):
    return lax.dot_general(a, b, (((1,), (1,)), ((), ())), preferred_element_type=F32)


def _dot_tn(a, b):
    return lax.dot_general(a, b, (((0,), (0,)), ((), ())), preferred_element_type=F32)


def _sigmoid(t):
    return 0.5 * jnp.tanh(0.5 * t) + 0.5


def _silu(t):
    return t * _sigmoid(t)


def _log_sigmoid(t):
    return jnp.minimum(t, 0.0) - jnp.log1p(jnp.exp(-jnp.abs(t)))


def _rms(t, axis=-1):
    return t * lax.rsqrt(jnp.mean(t * t, axis=axis, keepdims=True) + EPS)


def _split3(t):
    hi = t.astype(BF16)
    r1 = t - hi.astype(F32)
    mid = r1.astype(BF16)
    lo = (r1 - mid.astype(F32)).astype(BF16)
    return hi, mid, lo


def _adaln_kernel(c_ref, w_ref, b_ref, o_ref):
    s = _silu(c_ref[...]).astype(BF16)
    o_ref[...] = _dot(s, w_ref[...].astype(BF16)) + b_ref[...]


def _adaln(cc, w, b):
    rows, d = cc.shape
    cols = w.shape[1]
    blk = 1536
    return pl.pallas_call(
        _adaln_kernel,
        grid=(cols // blk,),
        in_specs=[pl.BlockSpec((rows, d), lambda j: (0, 0)),
                  pl.BlockSpec((d, blk), lambda j: (0, j)),
                  pl.BlockSpec((1, blk), lambda j: (0, j))],
        out_specs=pl.BlockSpec((rows, blk), lambda j: (0, j)),
        out_shape=jax.ShapeDtypeStruct((rows, cols), F32),
        compiler_params=pltpu.CompilerParams(dimension_semantics=("parallel",),
                                             vmem_limit_bytes=VMEM_LIMIT),
        name="adaln",
    )(cc, w, b.reshape(1, cols))


def _project_q(hb, wqt_ref, rope_refs, qt_ref):
    L = SCAN_CHUNK
    n_chunks = hb.shape[0] // L
    qt = _dot_nt(wqt_ref[...], hb)
    for hd in range(HEADS):
        sl = slice(hd * HEAD_DIM, (hd + 1) * HEAD_DIM)
        t = qt[sl, :]
        if rope_refs is not None:
            cost_ref, sint_ref = rope_refs[2], rope_refs[3]
            half = HEAD_DIM // 2
            rot = jnp.concatenate([t[half:, :], t[:half, :]], axis=0)
            t = t * cost_ref[...] + rot * sint_ref[...]
        for ci in range(n_chunks):
            qt_ref[0, ci, sl, :] = t[:, ci * L:(ci + 1) * L].astype(BF16)
    for ci in range(n_chunks):
        qt_ref[0, ci, MIX_W:, :] = qt[MIX_W:, ci * L:(ci + 1) * L].astype(BF16)


def _project_kv(x_ref, gain_ref, sh_ref, sc_ref, wk_ref, wvt_ref, gb_ref, rope_refs, k_ref, vt_ref, gr_ref):
    L = SCAN_CHUNK
    tokens = x_ref.shape[1]
    n_chunks = tokens // L
    h = _rms(x_ref[0]) * gain_ref[...]
    h = h * (1.0 + sc_ref[0]) + sh_ref[0]
    hb = h.astype(BF16)

    vg = _dot_nt(wvt_ref[...], hb)
    for ci in range(n_chunks):
        vt_ref[0, ci] = vg[0:2 * MIX_W, ci * L:(ci + 1) * L].astype(BF16)

    g = vg[2 * MIX_W:, :] + gb_ref[...]
    i_pre = g[0:8, :]
    log_f = _log_sigmoid(g[8:16, :])
    src = lax.broadcasted_iota(jnp.int32, (L, L), 0)
    dst = lax.broadcasted_iota(jnp.int32, (L, L), 1)
    prefix_m = (src <= dst).astype(BF16)
    suffix_m = (src >= dst).astype(BF16)
    is_fwd = lax.broadcasted_iota(jnp.int32, (8, 1), 0) < HEADS
    parts = _split3(jnp.concatenate([log_f, log_f], axis=0))
    stacked = jnp.concatenate([p[:, ci * L:(ci + 1) * L] for ci in range(n_chunks) for p in parts], axis=0)
    pre_all = _dot(stacked, prefix_m)
    suf_all = _dot(stacked, suffix_m)
    for ci in range(n_chunks):
        cs = slice(ci * L, (ci + 1) * L)
        pieces = [slice((3 * ci + i) * 16, (3 * ci + i) * 16 + 8) for i in range(3)]
        pre = sum(pre_all[rows] for rows in pieces)
        suf = sum(suf_all[rows] for rows in pieces)
        cum = jnp.where(is_fwd, pre, suf)
        gr_ref[0, ci] = jnp.concatenate([i_pre[:, cs] - cum, cum], axis=0)

    k = _dot(hb, wk_ref[...]) * (HEAD_DIM ** -0.5)
    for hd in range(HEADS):
        sl = slice(hd * HEAD_DIM, (hd + 1) * HEAD_DIM)
        t = k[:, sl]
        if rope_refs is not None:
            cos_ref, sin_ref = rope_refs[0], rope_refs[1]
            t = t * cos_ref[...] + pltpu.roll(t, HEAD_DIM // 2, 1) * sin_ref[...]
        k_ref[0, :, sl] = t.astype(BF16)
    k_ref[0, :, MIX_W:] = k[:, MIX_W:].astype(BF16)
    return hb


S_SHAPE = (2, HEADS, HEAD_DIM, HEAD_DIM)
C_SHAPE = (2, HEADS, HEAD_DIM + ONES_ROWS, HEAD_DIM)
M_SHAPE = (2 * HEADS, HEAD_DIM)


def _chunk_sources(rc_ref, k_ref, vt_ref, gr_ref, ci, d):
    L = SCAN_CHUNK
    cs = slice(ci * L, (ci + 1) * L)
    pos = lax.broadcasted_iota(jnp.int32, (1, L), 1).astype(F32)
    ones = jnp.ones((ONES_ROWS, L), BF16)
    a = gr_ref[0, ci, 4 * d:4 * d + 4, :]
    cum = gr_ref[0, ci, 8 + 4 * d:12 + 4 * d, :]
    edge = cum[:, L - 1:L] if d == 0 else cum[:, 0:1]
    b_last = jnp.broadcast_to(edge, (HEADS, HEAD_DIM))
    amax = jnp.broadcast_to(jnp.max(a, axis=1, keepdims=True), (HEADS, HEAD_DIM))
    w_loc = jnp.exp(a - jnp.concatenate([amax] * (L // HEAD_DIM), axis=1))
    u_ret, u_ml = [], []
    for hd in range(HEADS):
        sl = slice(hd * HEAD_DIM, (hd + 1) * HEAD_DIM)
        ml = slice(MIX_W + hd * HEAD_DIM, MIX_W + (hd + 1) * HEAD_DIM)
        lg = rc_ref[4 * d + hd]
        to_end = jnp.exp(lg * ((L - 1.0) - pos)) if d == 0 else jnp.exp(lg * pos)
        u_ret.append(_dot(vt_ref[0, ci, sl, :] * to_end.astype(BF16), k_ref[0, cs, sl]))
        v_ext = jnp.concatenate([vt_ref[0, ci, ml, :], ones], axis=0)
        u_ml.append(_dot(v_ext * w_loc[hd:hd + 1, :].astype(BF16), k_ref[0, cs, ml]))
    return amax, b_last, u_ret, u_ml


def _advance_state(rc_ref, s_scr, c_scr, m_scr, d, amax, b_last, u_ret, u_ml, emit):
    rows = slice(HEADS * d, HEADS * (d + 1))
    m_old = m_scr[rows, :]
    m_mid = jnp.maximum(m_old, amax)
    w_old = jnp.exp(m_old - m_mid)
    w_new = jnp.exp(amax - m_mid)
    if emit is not None:
        s_out, c_out, m_out, slot = emit
        m_out[0, slot] = m_old
    m_scr[rows, :] = b_last + m_mid
    for hd in range(HEADS):
        s_prev = s_scr[d, hd]
        c_prev = c_scr[d, hd]
        if emit is not None:
            s_out[0, slot, hd] = s_prev.astype(BF16)
            c_out[0, slot, hd] = c_prev.astype(BF16)
        s_scr[d, hd] = rc_ref[2 * HEADS + HEADS * d + hd] * s_prev + u_ret[hd]
        c_scr[d, hd] = w_old[hd:hd + 1, :] * c_prev + w_new[hd:hd + 1, :] * u_ml[hd]


def _ctx_kernel(rc_ref, x_ref, gain_ref, sh_ref, sc_ref, wk_ref, wvt_ref, gb_ref,
                s_fin, c_fin, m_fin, k_scr, vt_scr, gr_scr, s_scr, c_scr, m_scr, *, group):
    L = SCAN_CHUNK
    _project_kv(x_ref, gain_ref, sh_ref, sc_ref, wk_ref, wvt_ref, gb_ref, None, k_scr, vt_scr, gr_scr)
    per_element = x_ref.shape[1] // L // group
    for g in range(group):
        s_scr[...] = jnp.zeros_like(s_scr)
        c_scr[...] = jnp.zeros_like(c_scr)
        m_scr[...] = jnp.zeros_like(m_scr)
        own = range(g * per_element, (g + 1) * per_element)
        for d in range(2):
            for ci in (own if d == 0 else reversed(own)):
                _advance_state(rc_ref, s_scr, c_scr, m_scr, d,
                               *_chunk_sources(rc_ref, k_scr, vt_scr, gr_scr, ci, d), None)
        s_fin[g] = s_scr[...]
        c_fin[g] = c_scr[...]
        m_fin[g] = m_scr[...]


def _ctx_states(ret_consts, ctx, gain, shift, scale, wk, wvt, gbias):
    b, n, d = ctx.shape
    group = CTX_GROUP if b % CTX_GROUP == 0 else 1
    tokens = group * n
    nck = tokens // SCAN_CHUNK
    blk = lambda shape: pl.BlockSpec((group,) + shape, lambda i: (i,) + (0,) * len(shape))
    return pl.pallas_call(
        functools.partial(_ctx_kernel, group=group),
        grid=(b // group,),
        in_specs=[pl.BlockSpec(memory_space=pltpu.SMEM),
                  pl.BlockSpec((1, tokens, d), lambda i: (i, 0, 0)),
                  _resident((1, d)), _resident((1, 1, d)), _resident((1, 1, d)),
                  _resident(wk.shape), _resident(wvt.shape), _resident(gbias.shape)],
        out_specs=(blk(S_SHAPE), blk(C_SHAPE), blk(M_SHAPE)),
        out_shape=tuple(jax.ShapeDtypeStruct((b,) + shape, F32) for shape in (S_SHAPE, C_SHAPE, M_SHAPE)),
        scratch_shapes=[pltpu.VMEM((1, tokens, 2 * MIX_W), BF16),
                        pltpu.VMEM((1, nck, 2 * MIX_W, SCAN_CHUNK), BF16),
                        pltpu.VMEM((1, nck, 16, SCAN_CHUNK), F32),
                        pltpu.VMEM(S_SHAPE, F32), pltpu.VMEM(C_SHAPE, F32), pltpu.VMEM(M_SHAPE, F32)],
        compiler_params=pltpu.CompilerParams(dimension_semantics=("arbitrary",),
                                             vmem_limit_bytes=VMEM_LIMIT),
        name="ctx_states",
    )(ret_consts, ctx.reshape(b // group, tokens, d), gain, shift, scale, wk, wvt, gbias)


def _proj_scan_kernel(rc_ref, x_ref, gain_ref, sh_ref, sc_ref, wqt_ref, wk_ref, wvt_ref, gb_ref,
                      cos_ref, sin_ref, cost_ref, sint_ref, s0_ref, c0_ref, m0_ref,
                      qt_ref, k_ref, vt_ref, gr_ref, sf_ref, cf_ref, mf_ref, sb_ref, cb_ref, mb_ref,
                      s_scr, c_scr, m_scr, ub_ret, ub_ml, ub_stat, *, n_tiles, bwd_chunks):
    L = SCAN_CHUNK
    j = pl.program_id(1)
    per_tile = x_ref.shape[1] // L
    n_chunks = n_tiles * per_tile

    @pl.when(j == 0)
    def _():
        s_scr[...] = s0_ref[0]
        c_scr[...] = c0_ref[0]
        m_scr[...] = m0_ref[0]

    @pl.when(j < n_tiles)
    def _():
        rope_refs = (cos_ref, sin_ref, cost_ref, sint_ref)
        hb = _project_kv(x_ref, gain_ref, sh_ref, sc_ref, wk_ref, wvt_ref, gb_ref, rope_refs,
                         k_ref, vt_ref, gr_ref)
        _project_q(hb, wqt_ref, rope_refs, qt_ref)
        for ci in range(per_tile):
            _advance_state(rc_ref, s_scr, c_scr, m_scr, 0, *_chunk_sources(rc_ref, k_ref, vt_ref, gr_ref, ci, 0),
                           (sf_ref, cf_ref, mf_ref, ci))
            amax, b_last, u_ret, u_ml = _chunk_sources(rc_ref, k_ref, vt_ref, gr_ref, ci, 1)
            chunk = j * per_tile + ci
            ub_stat[chunk] = jnp.concatenate([amax, b_last], axis=0)
            for hd in range(HEADS):
                ub_ret[chunk, hd] = u_ret[hd].astype(BF16)
                ub_ml[chunk, hd] = u_ml[hd].astype(BF16)

    @pl.when(j >= n_tiles)
    def _():
        first = n_chunks - 1 - (j - n_tiles) * bwd_chunks
        for i in range(bwd_chunks):
            chunk = first - i
            stat = ub_stat[chunk]
            _advance_state(rc_ref, s_scr, c_scr, m_scr, 1, stat[0:HEADS], stat[HEADS:2 * HEADS],
                           [ub_ret[chunk, hd].astype(F32) for hd in range(HEADS)],
                           [ub_ml[chunk, hd].astype(F32) for hd in range(HEADS)],
                           (sb_ref, cb_ref, mb_ref, bwd_chunks - 1 - i))


def _proj_scan(ret_consts, x, gain, shift, scale, wqt, wk, wvt, gbias, rope, init):
    b, n, d = x.shape
    L = SCAN_CHUNK
    t = PROJ_TOKENS
    nt = n // t
    nc = n // L
    per_tile = t // L
    nbb = nc // BWD_CHUNKS
    tile = lambda j: jnp.minimum(j, nt - 1)
    bblk = lambda j: nbb - 1 - jnp.maximum(j - nt, 0)
    tok3 = lambda i, j: (i, tile(j), 0)
    chunk4 = lambda i, j: (i, tile(j), 0, 0)
    mod3 = lambda i, j: (i, 0, 0)
    per_b = lambda shape: pl.BlockSpec((1,) + shape, lambda i, j: (i,) + (0,) * len(shape))
    cos, sin, cos_t, sin_t = rope
    in_specs = [pl.BlockSpec(memory_space=pltpu.SMEM),
                pl.BlockSpec((1, t, d), tok3),
                _resident((1, d)),
                pl.BlockSpec((1, 1, d), mod3), pl.BlockSpec((1, 1, d), mod3),
                _resident(wqt.shape), _resident(wk.shape), _resident(wvt.shape), _resident(gbias.shape),
                pl.BlockSpec((t, HEAD_DIM), lambda i, j: (tile(j), 0)),
                pl.BlockSpec((t, HEAD_DIM), lambda i, j: (tile(j), 0)),
                pl.BlockSpec((HEAD_DIM, t), lambda i, j: (0, tile(j))),
                pl.BlockSpec((HEAD_DIM, t), lambda i, j: (0, tile(j))),
                per_b(S_SHAPE), per_b(C_SHAPE), per_b(M_SHAPE)]
    out_shape = [jax.ShapeDtypeStruct((b, nc, 2 * MIX_W, L), BF16),
                 jax.ShapeDtypeStruct((b, n, 2 * MIX_W), BF16),
                 jax.ShapeDtypeStruct((b, nc, 2 * MIX_W, L), BF16),
                 jax.ShapeDtypeStruct((b, nc, 16, L), F32)]
    out_specs = [pl.BlockSpec((1, per_tile, 2 * MIX_W, L), chunk4),
                 pl.BlockSpec((1, t, 2 * MIX_W), tok3),
                 pl.BlockSpec((1, per_tile, 2 * MIX_W, L), chunk4),
                 pl.BlockSpec((1, per_tile, 16, L), chunk4)]
    for blk, step in ((per_tile, tile), (BWD_CHUNKS, bblk)):
        out_shape += [jax.ShapeDtypeStruct((b, nc) + S_SHAPE[1:], BF16),
                      jax.ShapeDtypeStruct((b, nc) + C_SHAPE[1:], BF16),
                      jax.ShapeDtypeStruct((b, nc, HEADS, HEAD_DIM), F32)]
        out_specs += [pl.BlockSpec((1, blk) + S_SHAPE[1:], lambda i, j, step=step: (i, step(j), 0, 0, 0)),
                      pl.BlockSpec((1, blk) + C_SHAPE[1:], lambda i, j, step=step: (i, step(j), 0, 0, 0)),
                      pl.BlockSpec((1, blk, HEADS, HEAD_DIM), lambda i, j, step=step: (i, step(j), 0, 0))]
    return pl.pallas_call(
        functools.partial(_proj_scan_kernel, n_tiles=nt, bwd_chunks=BWD_CHUNKS),
        grid=(b, nt + nbb),
        in_specs=in_specs,
        out_specs=tuple(out_specs),
        out_shape=tuple(out_shape),
        scratch_shapes=[pltpu.VMEM(S_SHAPE, F32), pltpu.VMEM(C_SHAPE, F32), pltpu.VMEM(M_SHAPE, F32),
                        pltpu.VMEM((nc,) + S_SHAPE[1:], BF16), pltpu.VMEM((nc,) + C_SHAPE[1:], BF16),
                        pltpu.VMEM((nc,) + M_SHAPE, F32)],
        compiler_params=pltpu.CompilerParams(dimension_semantics=("arbitrary", "arbitrary"),
                                             vmem_limit_bytes=VMEM_LIMIT),
        name="proj_scan",
    )(ret_consts, x, gain, shift, scale, wqt, wk, wvt, gbias, cos, sin, cos_t, sin_t, *init)


def _mixer_kernel(rc_ref, x_ref, gain_ref, sh_ref, sc_ref, g1_ref, qt_ref, k_ref, vt_ref, gr_ref,
                  sf_ref, sb_ref, cf_ref, cb_ref, mf_ref, mb_ref,
                  wgt_ref, wbg_ref, wru_ref, wmu_ref, wout_ref, o_ref, dec_scr, qdec_scr):
    L = SCAN_CHUNK
    key = lax.broadcasted_iota(jnp.int32, (L, L), 0)
    qry = lax.broadcasted_iota(jnp.int32, (L, L), 1)

    @pl.when((pl.program_id(0) == 0) & (pl.program_id(1) == 0))
    def _():
        rel = (qry - key).astype(F32)
        lpos = lax.broadcasted_iota(jnp.int32, (8, L), 1).astype(F32)
        first_row = lax.broadcasted_iota(jnp.int32, (8, L), 0) == 0
        for hd in range(HEADS):
            lg_f = rc_ref[hd]
            lg_b = rc_ref[4 + hd]
            dec_scr[hd] = (jnp.where(rel >= 0, jnp.exp(lg_f * jnp.maximum(rel, 0.0)), 0.0)
                           + jnp.where(rel <= 0, jnp.exp(lg_b * jnp.maximum(-rel, 0.0)), 0.0))
            qdec_scr[hd] = jnp.where(first_row, jnp.exp(lg_f * (lpos + 1.0)), jnp.exp(lg_b * (L - lpos)))

    d_model = x_ref.shape[-1]
    bg_w = d_model // HEADS
    masks = (key <= qry, key >= qry)
    c_refs, m_refs = (cf_ref, cb_ref), (mf_ref, mb_ref)
    ones = jnp.ones((ONES_ROWS, L), BF16)
    neg_inf = jnp.float32(-jnp.inf)
    n_chunks = x_ref.shape[1] // L
    a_cols = [(gr_ref[0, ci, 0:8, :] * LOG2E).T for ci in range(n_chunks)]
    scanned, y_ret, y_ml, bg_ret, bg_ml = [], [], [], [], []
    normed = []

    def gate_head(hd):
        if not normed:
            normed.append((_rms(x_ref[0]) * gain_ref[...] * (1.0 + sc_ref[0]) + sh_ref[0]).astype(BF16))
        hb = normed[0]
        gates = _dot_nt(wgt_ref[2 * hd * HEAD_DIM:2 * (hd + 1) * HEAD_DIM, :], hb)
        bg = _dot(hb, wbg_ref[:, 2 * hd * bg_w:2 * (hd + 1) * bg_w])
        bg_ret.append(bg[:, :bg_w])
        bg_ml.append(bg[:, bg_w:])
        ret_t, ml_t = scanned[hd]
        y_ret.append(ret_t * _silu(gates[:HEAD_DIM, :]))
        y_ml.append(_rms(_sigmoid(gates[HEAD_DIM:, :]) * ml_t, axis=0))

    chunks = range(n_chunks)
    toks = [slice(ci * L, (ci + 1) * L) for ci in chunks]

    def first_wave(hd):
        sl = slice(hd * HEAD_DIM, (hd + 1) * HEAD_DIM)
        ml = slice(MIX_W + hd * HEAD_DIM, MIX_W + (hd + 1) * HEAD_DIM)
        qt_r = [qt_ref[0, ci, sl, :] for ci in chunks]
        qt_m = [qt_ref[0, ci, ml, :] for ci in chunks]
        s_ret = [_dot(k_ref[0, toks[ci], sl], qt_r[ci]) for ci in chunks]
        s_ml = [_dot(k_ref[0, toks[ci], ml], qt_m[ci]) for ci in chunks]
        inter_ret, inter_ml = [], []
        for ci in chunks:
            qw = jnp.concatenate([qt_r[ci] * qdec_scr[hd, 0:1, :].astype(BF16),
                                  qt_r[ci] * qdec_scr[hd, 1:2, :].astype(BF16)], axis=0)
            st = jnp.concatenate([sf_ref[0, ci, hd], sb_ref[0, ci, hd]], axis=1)
            inter_ret.append(_dot(st, qw))
            inter_ml.append([_dot(c_refs[d][0, ci, hd], qt_m[ci]) for d in range(2)])
        return s_ret, s_ml, inter_ret, inter_ml

    wave = first_wave(0)
    for hd in range(HEADS):
        sl = slice(hd * HEAD_DIM, (hd + 1) * HEAD_DIM)
        ml = slice(MIX_W + hd * HEAD_DIM, MIX_W + (hd + 1) * HEAD_DIM)
        s_ret, s_ml, inter_ret, inter_ml = wave
        if hd + 1 < HEADS:
            wave = first_wave(hd + 1)
        p_ret = [(s_ret[ci] * dec_scr[hd]).astype(BF16) for ci in chunks]
        p_ml, m_locs, m_ins = [], [], []
        for ci in chunks:
            for d in range(2):
                a_col = a_cols[ci][:, 4 * d + hd:4 * d + hd + 1]
                m_in = jnp.concatenate([m_refs[d][0, ci, hd:hd + 1, :] * LOG2E] * (L // HEAD_DIM), axis=1)
                a_masked = jnp.where(masks[d], a_col, neg_inf)
                m_loc = jnp.maximum(jnp.max(a_masked, axis=0, keepdims=True), m_in)
                p_ml.append((s_ml[ci] * jnp.exp2(a_masked - m_loc)).astype(BF16))
                m_locs.append(m_loc)
                m_ins.append(m_in)
        ret_cols, ml_cols = [], []
        for ci in chunks:
            ret_cols.append(_rms(_dot(vt_ref[0, ci, sl, :], p_ret[ci]) + inter_ret[ci], axis=0))
            v_ext = jnp.concatenate([vt_ref[0, ci, ml, :], ones], axis=0)
            hsum = None
            for d in range(2):
                m_loc, m_in = m_locs[2 * ci + d], m_ins[2 * ci + d]
                cum_row = gr_ref[0, ci, 8 + 4 * d + hd:9 + 4 * d + hd, :] * LOG2E
                tot = _dot(v_ext, p_ml[2 * ci + d]) + jnp.exp2(m_in - m_loc) * inter_ml[ci][d]
                den = jnp.maximum(jnp.abs(tot[HEAD_DIM:HEAD_DIM + 1, :]), jnp.exp2(-(cum_row + m_loc)))
                hd_out = tot[0:HEAD_DIM, :] * (1.0 / den)
                hsum = hd_out if hsum is None else hsum + hd_out
            ml_cols.append(hsum)
        scanned.append((jnp.concatenate(ret_cols, axis=1), jnp.concatenate(ml_cols, axis=1)))
        if hd >= 1:
            gate_head(hd - 1)
    gate_head(HEADS - 1)

    y_ret = jnp.concatenate(y_ret, axis=0).astype(BF16)
    y_ml = jnp.concatenate(y_ml, axis=0).astype(BF16)
    bg_ret = jnp.concatenate(bg_ret, axis=1)
    bg_ml = jnp.concatenate(bg_ml, axis=1)
    merged = _sigmoid(bg_ret) * _dot_tn(y_ret, wru_ref[...]) + _sigmoid(bg_ml) * _dot_tn(y_ml, wmu_ref[...])
    o_ref[0] = x_ref[0] + g1_ref[0] * _dot(merged.astype(BF16), wout_ref[...])


def _mixer(ret_consts, x, gain, shift, scale, gate1, qt, k, vt, gr, chunk_states, wgt, wbg, wru, wmu, wout):
    b, n, d = x.shape
    L = SCAN_CHUNK
    t = MIX_TOKENS
    ch = t // L
    sf, sb, cf, cb, mf, mb = chunk_states
    tok3 = lambda i, j: (i, j, 0)
    mod3 = lambda i, j: (i, 0, 0)
    st5 = lambda i, j: (i, j, 0, 0, 0)
    st4 = lambda i, j: (i, j, 0, 0)
    in_specs = [pl.BlockSpec(memory_space=pltpu.SMEM),
                pl.BlockSpec((1, t, d), tok3),
                _resident((1, d)),
                pl.BlockSpec((1, 1, d), mod3), pl.BlockSpec((1, 1, d), mod3), pl.BlockSpec((1, 1, d), mod3),
                pl.BlockSpec((1, ch, 2 * MIX_W, L), st4),
                pl.BlockSpec((1, t, 2 * MIX_W), tok3),
                pl.BlockSpec((1, ch, 2 * MIX_W, L), st4),
                pl.BlockSpec((1, ch, 16, L), st4),
                pl.BlockSpec((1, ch) + S_SHAPE[1:], st5),
                pl.BlockSpec((1, ch) + S_SHAPE[1:], st5),
                pl.BlockSpec((1, ch) + C_SHAPE[1:], st5),
                pl.BlockSpec((1, ch) + C_SHAPE[1:], st5),
                pl.BlockSpec((1, ch, HEADS, HEAD_DIM), st4),
                pl.BlockSpec((1, ch, HEADS, HEAD_DIM), st4),
                _resident(wgt.shape),
                _resident(wbg.shape),
                _resident(wru.shape),
                _resident(wmu.shape),
                _resident(wout.shape)]
    return pl.pallas_call(
        _mixer_kernel,
        grid=(b, n // t),
        in_specs=in_specs,
        out_specs=pl.BlockSpec((1, t, d), tok3),
        out_shape=jax.ShapeDtypeStruct((b, n, d), F32),
        scratch_shapes=[pltpu.VMEM((HEADS, L, L), F32), pltpu.VMEM((HEADS, 8, L), F32)],
        compiler_params=pltpu.CompilerParams(dimension_semantics=("arbitrary", "arbitrary"),
                                             vmem_limit_bytes=VMEM_LIMIT),
        name="mixer",
    )(ret_consts, x, gain, shift, scale, gate1, qt, k, vt, gr, sf, sb, cf, cb, mf, mb,
      wgt, wbg, wru, wmu, wout)


def _ffn_kernel(x_ref, gain_ref, sh_ref, sc_ref, g2_ref, fgain_ref, w1_ref, w2_ref, o_ref, *, hidden):
    xf = x_ref[0]
    h = _rms(xf) * gain_ref[...]
    hb = (h * (1.0 + sc_ref[0]) + sh_ref[0]).astype(BF16)
    acc = None
    start = 0
    for width in FFN_SPLITS:
        gate = _dot(hb, w1_ref[:, start:start + width])
        up = _dot(hb, w1_ref[:, hidden + start:hidden + start + width])
        act = (_silu(gate) * up).astype(BF16)
        part = _dot(act, w2_ref[start:start + width, :])
        acc = part if acc is None else acc + part
        start += width
    o_ref[0] = _rms(xf + g2_ref[0] * acc) * fgain_ref[...]


def _ffn(x, gain, shift, scale, gate2, final_gain, w1, w2):
    b, n, d = x.shape
    t = FFN_TOKENS
    hidden = w2.shape[0]
    assert sum(FFN_SPLITS) == hidden
    tok3 = lambda i, j: (i, j, 0)
    mod3 = lambda i, j: (i, 0, 0)
    return pl.pallas_call(
        functools.partial(_ffn_kernel, hidden=hidden),
        grid=(b, n // t),
        in_specs=[pl.BlockSpec((1, t, d), tok3),
                  _resident((1, d)),
                  pl.BlockSpec((1, 1, d), mod3), pl.BlockSpec((1, 1, d), mod3), pl.BlockSpec((1, 1, d), mod3),
                  _resident((1, d)),
                  _resident(w1.shape),
                  _resident(w2.shape)],
        out_specs=pl.BlockSpec((1, t, d), tok3),
        out_shape=jax.ShapeDtypeStruct((b, n, d), F32),
        compiler_params=pltpu.CompilerParams(dimension_semantics=("parallel", "parallel"),
                                             vmem_limit_bytes=VMEM_LIMIT),
        name="ffn",
    )(x, gain, shift, scale, gate2, final_gain, w1, w2)


def _rope_tables(n):
    pos = jnp.arange(n, dtype=jnp.int32)
    rows = (pos // GRID_W).astype(F32)
    cols = (pos % GRID_W).astype(F32)
    n_freq = HEAD_DIM // 4
    inv = ROPE_BASE ** (-jnp.arange(n_freq, dtype=F32) / n_freq)
    ang_t = jnp.concatenate([inv[:, None] * rows[None, :], inv[:, None] * cols[None, :]], axis=0)
    cos_t, sin_t = jnp.cos(ang_t), jnp.sin(ang_t)
    cos2_t = jnp.concatenate([cos_t, cos_t], axis=0)
    sin2_t = jnp.concatenate([-sin_t, sin_t], axis=0)
    return cos2_t.T, sin2_t.T, cos2_t, sin2_t


def kernel(x, c, ctx, c_ctx, w_ada, b_ada, norm1_gain, norm2_gain, w_in, mlstm_gate_bias, ret_decay_logit,
           w_ret_up, w_ml_up, w_out, w_ffn_in, w_ffn_out, final_gain):
    assert w_ada.shape[0] == 1, "single-layer block"
    b, n, d = x.shape
    assert n % PROJ_TOKENS == 0 and n % FFN_TOKENS == 0 and n % MIX_TOKENS == 0
    assert PROJ_TOKENS % SCAN_CHUNK == 0 and MIX_TOKENS % SCAN_CHUNK == 0 and ctx.shape[1] % SCAN_CHUNK == 0
    assert (n // SCAN_CHUNK) % BWD_CHUNKS == 0

    rows = -(-(b + 1) // 16) * 16
    cc = jnp.concatenate([c, c_ctx[None, :], jnp.zeros((rows - b - 1, d), F32)], axis=0)
    mod = _adaln(cc, w_ada[0], b_ada[0])
    sh1, sc1, g1, sh2, sc2, g2 = (mod[:b, i * d:(i + 1) * d].reshape(b, 1, d) for i in range(6))
    csh1 = mod[b, 0:d].reshape(1, 1, d)
    csc1 = mod[b, d:2 * d].reshape(1, 1, d)

    w = w_in[0]
    o = [0]
    for width in (MIX_W, MIX_W, MIX_W, MIX_W, MIX_W, MIX_W, MIX_W, MIX_W, 4 * HEADS, d, d):
        o.append(o[-1] + width)
    col = lambda i: w[:, o[i]:o[i + 1]]
    gates = col(8).reshape(d, 4, HEADS)
    gates = jnp.concatenate([gates[:, 0], gates[:, 2], gates[:, 1], gates[:, 3]], axis=1)
    gb = mlstm_gate_bias[0]
    gbias = jnp.concatenate([gb[0], gb[2], gb[1], gb[3]]).reshape(4 * HEADS, 1).astype(F32)
    wqt = jnp.concatenate([col(0), col(4)], axis=1).T.astype(BF16)
    wk = jnp.concatenate([col(1), col(5)], axis=1).astype(BF16)
    wvt = jnp.concatenate([col(2), col(6), gates], axis=1).T.astype(BF16)
    bg_w = d // HEADS
    by_head = lambda u, v, width: jnp.stack([u.reshape(d, HEADS, width), v.reshape(d, HEADS, width)],
                                            axis=2).reshape(d, 2 * HEADS * width)
    wgt = by_head(col(3), col(7), HEAD_DIM).T.astype(BF16)
    wbg = by_head(col(9), col(10), bg_w).astype(BF16)

    log_gamma = jax.nn.log_sigmoid(ret_decay_logit[0].astype(F32)).reshape(2 * HEADS)
    ret_consts = jnp.concatenate([log_gamma, jnp.exp(log_gamma * SCAN_CHUNK)])

    gain1 = norm1_gain[0].reshape(1, d)
    ctx_final = _ctx_states(ret_consts, ctx, gain1, csh1, csc1, wk, wvt, gbias)
    qt_x, k_x, vt_x, gr_x, sf, cf, mf, sb, cb, mb = _proj_scan(
        ret_consts, x, gain1, sh1, sc1, wqt, wk, wvt, gbias, _rope_tables(n), ctx_final)
    x1 = _mixer(ret_consts, x, gain1, sh1, sc1, g1, qt_x, k_x, vt_x, gr_x, (sf, sb, cf, cb, mf, mb),
                wgt, wbg, w_ret_up[0].astype(BF16), w_ml_up[0].astype(BF16), w_out[0].astype(BF16))
    return _ffn(x1, norm2_gain[0].reshape(1, d), sh2, sc2, g2, final_gain.reshape(1, d),
                w_ffn_in[0].astype(BF16), w_ffn_out[0].astype(BF16))
```

```python
import functools

import jax
import jax.numpy as jnp
from jax import lax
from jax.experimental import pallas as pl
from jax.experimental.pallas import tpu as pltpu

HEADS = 4
HEAD_DIM = 128
MIX_W = HEADS * HEAD_DIM
ONES_ROWS = 16
GRID_W = 64
ROPE_BASE = 10000.0
EPS = 1e-6
LOG2E = 1.4426950408889634
SCAN_CHUNK = 256
CTX_GROUP = 4
PROJ_TOKENS = 1024
BWD_CHUNKS = 16
MIX_TOKENS = 1024
FFN_TOKENS = 1024
FFN_SPLITS = (1536, 1280)
VMEM_LIMIT = 60 * 1024 * 1024

F32 = jnp.float32
BF16 = jnp.bfloat16


def _resident(shape):
    return pl.BlockSpec(shape, lambda *_: (0,) * len(shape), pipeline_mode=pl.Buffered(1))


def _dot(a, b):
    return jnp.dot(a, b, preferred_element_type=F32)


def _dot_nt(a, b):
    return lax.dot_general(a, b, (((1,), (1,)), ((), ())), preferred_element_type=F32)


def _dot_tn(a, b):
    return lax.dot_general(a, b, (((0,), (0,)), ((), ())), preferred_element_type=F32)


def _sigmoid(t):
    return 0.5 * jnp.tanh(0.5 * t) + 0.5


def _silu(t):
    return t * _sigmoid(t)


def _log_sigmoid(t):
    return jnp.minimum(t, 0.0) - jnp.log1p(jnp.exp(-jnp.abs(t)))


def _rms(t, axis=-1):
    return t * lax.rsqrt(jnp.mean(t * t, axis=axis, keepdims=True) + EPS)


def _split3(t):
    hi = t.astype(BF16)
    r1 = t - hi.astype(F32)
    mid = r1.astype(BF16)
    lo = (r1 - mid.astype(F32)).astype(BF16)
    return hi, mid, lo


def _adaln_kernel(c_ref, w_ref, b_ref, o_ref):
    s = _silu(c_ref[...]).astype(BF16)
    o_ref[...] = _dot(s, w_ref[...].astype(BF16)) + b_ref[...]


def _adaln(cc, w, b):
    rows, d = cc.shape
    cols = w.shape[1]
    blk = 1536
    return pl.pallas_call(
        _adaln_kernel,
        grid=(cols // blk,),
        in_specs=[pl.BlockSpec((rows, d), lambda j: (0, 0)),
                  pl.BlockSpec((d, blk), lambda j: (0, j)),
                  pl.BlockSpec((1, blk), lambda j: (0, j))],
        out_specs=pl.BlockSpec((rows, blk), lambda j: (0, j)),
        out_shape=jax.ShapeDtypeStruct((rows, cols), F32),
        compiler_params=pltpu.CompilerParams(dimension_semantics=("parallel",),
                                             vmem_limit_bytes=VMEM_LIMIT),
        name="adaln",
    )(cc, w, b.reshape(1, cols))


def _project_q(hb, wqt_ref, rope_refs, qt_ref):
    L = SCAN_CHUNK
    n_chunks = hb.shape[0] // L
    qt = _dot_nt(wqt_ref[...], hb)
    for hd in range(HEADS):
        sl = slice(hd * HEAD_DIM, (hd + 1) * HEAD_DIM)
        t = qt[sl, :]
        if rope_refs is not None:
            cost_ref, sint_ref = rope_refs[2], rope_refs[3]
            half = HEAD_DIM // 2
            rot = jnp.concatenate([t[half:, :], t[:half, :]], axis=0)
            t = t * cost_ref[...] + rot * sint_ref[...]
        for ci in range(n_chunks):
            qt_ref[0, ci, sl, :] = t[:, ci * L:(ci + 1) * L].astype(BF16)
    for ci in range(n_chunks):
        qt_ref[0, ci, MIX_W:, :] = qt[MIX_W:, ci * L:(ci + 1) * L].astype(BF16)


def _project_kv(x_ref, gain_ref, sh_ref, sc_ref, wk_ref, wvt_ref, gb_ref, rope_refs, k_ref, vt_ref, gr_ref):
    L = SCAN_CHUNK
    tokens = x_ref.shape[1]
    n_chunks = tokens // L
    h = _rms(x_ref[0]) * gain_ref[...]
    h = h * (1.0 + sc_ref[0]) + sh_ref[0]
    hb = h.astype(BF16)

    vg = _dot_nt(wvt_ref[...], hb)
    for ci in range(n_chunks):
        vt_ref[0, ci] = vg[0:2 * MIX_W, ci * L:(ci + 1) * L].astype(BF16)

    g = vg[2 * MIX_W:, :] + gb_ref[...]
    i_pre = g[0:8, :]
    log_f = _log_sigmoid(g[8:16, :])
    src = lax.broadcasted_iota(jnp.int32, (L, L), 0)
    dst = lax.broadcasted_iota(jnp.int32, (L, L), 1)
    prefix_m = (src <= dst).astype(BF16)
    suffix_m = (src >= dst).astype(BF16)
    is_fwd = lax.broadcasted_iota(jnp.int32, (8, 1), 0) < HEADS
    parts = _split3(jnp.concatenate([log_f, log_f], axis=0))
    stacked = jnp.concatenate([p[:, ci * L:(ci + 1) * L] for ci in range(n_chunks) for p in parts], axis=0)
    pre_all = _dot(stacked, prefix_m)
    suf_all = _dot(stacked, suffix_m)
    for ci in range(n_chunks):
        cs = slice(ci * L, (ci + 1) * L)
        pieces = [slice((3 * ci + i) * 16, (3 * ci + i) * 16 + 8) for i in range(3)]
        pre = sum(pre_all[rows] for rows in pieces)
        suf = sum(suf_all[rows] for rows in pieces)
        cum = jnp.where(is_fwd, pre, suf)
        gr_ref[0, ci] = jnp.concatenate([i_pre[:, cs] - cum, cum], axis=0)

    k = _dot(hb, wk_ref[...]) * (HEAD_DIM ** -0.5)
    for hd in range(HEADS):
        sl = slice(hd * HEAD_DIM, (hd + 1) * HEAD_DIM)
        t = k[:, sl]
        if rope_refs is not None:
            cos_ref, sin_ref = rope_refs[0], rope_refs[1]
            t = t * cos_ref[...] + pltpu.roll(t, HEAD_DIM // 2, 1) * sin_ref[...]
        k_ref[0, :, sl] = t.astype(BF16)
    k_ref[0, :, MIX_W:] = k[:, MIX_W:].astype(BF16)
    return hb


S_SHAPE = (2, HEADS, HEAD_DIM, HEAD_DIM)
C_SHAPE = (2, HEADS, HEAD_DIM + ONES_ROWS, HEAD_DIM)
M_SHAPE = (2 * HEADS, HEAD_DIM)


def _chunk_sources(rc_ref, k_ref, vt_ref, gr_ref, ci, d):
    L = SCAN_CHUNK
    cs = slice(ci * L, (ci + 1) * L)
    pos = lax.broadcasted_iota(jnp.int32, (1, L), 1).astype(F32)
    ones = jnp.ones((ONES_ROWS, L), BF16)
    a = gr_ref[0, ci, 4 * d:4 * d + 4, :]
    cum = gr_ref[0, ci, 8 + 4 * d:12 + 4 * d, :]
    edge = cum[:, L - 1:L] if d == 0 else cum[:, 0:1]
    b_last = jnp.broadcast_to(edge, (HEADS, HEAD_DIM))
    amax = jnp.broadcast_to(jnp.max(a, axis=1, keepdims=True), (HEADS, HEAD_DIM))
    w_loc = jnp.exp(a - jnp.concatenate([amax] * (L // HEAD_DIM), axis=1))
    u_ret, u_ml = [], []
    for hd in range(HEADS):
        sl = slice(hd * HEAD_DIM, (hd + 1) * HEAD_DIM)
        ml = slice(MIX_W + hd * HEAD_DIM, MIX_W + (hd + 1) * HEAD_DIM)
        lg = rc_ref[4 * d + hd]
        to_end = jnp.exp(lg * ((L - 1.0) - pos)) if d == 0 else jnp.exp(lg * pos)
        u_ret.append(_dot(vt_ref[0, ci, sl, :] * to_end.astype(BF16), k_ref[0, cs, sl]))
        v_ext = jnp.concatenate([vt_ref[0, ci, ml, :], ones], axis=0)
        u_ml.append(_dot(v_ext * w_loc[hd:hd + 1, :].astype(BF16), k_ref[0, cs, ml]))
    return amax, b_last, u_ret, u_ml


def _advance_state(rc_ref, s_scr, c_scr, m_scr, d, amax, b_last, u_ret, u_ml, emit):
    rows = slice(HEADS * d, HEADS * (d + 1))
    m_old = m_scr[rows, :]
    m_mid = jnp.maximum(m_old, amax)
    w_old = jnp.exp(m_old - m_mid)
    w_new = jnp.exp(amax - m_mid)
    if emit is not None:
        s_out, c_out, m_out, slot = emit
        m_out[0, slot] = m_old
    m_scr[rows, :] = b_last + m_mid
    for hd in range(HEADS):
        s_prev = s_scr[d, hd]
        c_prev = c_scr[d, hd]
        if emit is not None:
            s_out[0, slot, hd] = s_prev.astype(BF16)
            c_out[0, slot, hd] = c_prev.astype(BF16)
        s_scr[d, hd] = rc_ref[2 * HEADS + HEADS * d + hd] * s_prev + u_ret[hd]
        c_scr[d, hd] = w_old[hd:hd + 1, :] * c_prev + w_new[hd:hd + 1, :] * u_ml[hd]


def _ctx_kernel(rc_ref, x_ref, gain_ref, sh_ref, sc_ref, wk_ref, wvt_ref, gb_ref,
                s_fin, c_fin, m_fin, k_scr, vt_scr, gr_scr, s_scr, c_scr, m_scr, *, group):
    L = SCAN_CHUNK
    _project_kv(x_ref, gain_ref, sh_ref, sc_ref, wk_ref, wvt_ref, gb_ref, None, k_scr, vt_scr, gr_scr)
    per_element = x_ref.shape[1] // L // group
    for g in range(group):
        s_scr[...] = jnp.zeros_like(s_scr)
        c_scr[...] = jnp.zeros_like(c_scr)
        m_scr[...] = jnp.zeros_like(m_scr)
        own = range(g * per_element, (g + 1) * per_element)
        for d in range(2):
            for ci in (own if d == 0 else reversed(own)):
                _advance_state(rc_ref, s_scr, c_scr, m_scr, d,
                               *_chunk_sources(rc_ref, k_scr, vt_scr, gr_scr, ci, d), None)
        s_fin[g] = s_scr[...]
        c_fin[g] = c_scr[...]
        m_fin[g] = m_scr[...]


def _ctx_states(ret_consts, ctx, gain, shift, scale, wk, wvt, gbias):
    b, n, d = ctx.shape
    group = CTX_GROUP if b % CTX_GROUP == 0 else 1
    tokens = group * n
    nck = tokens // SCAN_CHUNK
    blk = lambda shape: pl.BlockSpec((group,) + shape, lambda i: (i,) + (0,) * len(shape))
    return pl.pallas_call(
        functools.partial(_ctx_kernel, group=group),
        grid=(b // group,),
        in_specs=[pl.BlockSpec(memory_space=pltpu.SMEM),
                  pl.BlockSpec((1, tokens, d), lambda i: (i, 0, 0)),
                  _resident((1, d)), _resident((1, 1, d)), _resident((1, 1, d)),
                  _resident(wk.shape), _resident(wvt.shape), _resident(gbias.shape)],
        out_specs=(blk(S_SHAPE), blk(C_SHAPE), blk(M_SHAPE)),
        out_shape=tuple(jax.ShapeDtypeStruct((b,) + shape, F32) for shape in (S_SHAPE, C_SHAPE, M_SHAPE)),
        scratch_shapes=[pltpu.VMEM((1, tokens, 2 * MIX_W), BF16),
                        pltpu.VMEM((1, nck, 2 * MIX_W, SCAN_CHUNK), BF16),
                        pltpu.VMEM((1, nck, 16, SCAN_CHUNK), F32),
                        pltpu.VMEM(S_SHAPE, F32), pltpu.VMEM(C_SHAPE, F32), pltpu.VMEM(M_SHAPE, F32)],
        compiler_params=pltpu.CompilerParams(dimension_semantics=("arbitrary",),
                                             vmem_limit_bytes=VMEM_LIMIT),
        name="ctx_states",
    )(ret_consts, ctx.reshape(b // group, tokens, d), gain, shift, scale, wk, wvt, gbias)


def _proj_scan_kernel(rc_ref, x_ref, gain_ref, sh_ref, sc_ref, wqt_ref, wk_ref, wvt_ref, gb_ref,
                      cos_ref, sin_ref, cost_ref, sint_ref, s0_ref, c0_ref, m0_ref,
                      qt_ref, k_ref, vt_ref, gr_ref, sf_ref, cf_ref, mf_ref, sb_ref, cb_ref, mb_ref,
                      s_scr, c_scr, m_scr, ub_ret, ub_ml, ub_stat, *, n_tiles, bwd_chunks):
    L = SCAN_CHUNK
    j = pl.program_id(1)
    per_tile = x_ref.shape[1] // L
    n_chunks = n_tiles * per_tile

    @pl.when(j == 0)
    def _():
        s_scr[...] = s0_ref[0]
        c_scr[...] = c0_ref[0]
        m_scr[...] = m0_ref[0]

    @pl.when(j < n_tiles)
    def _():
        rope_refs = (cos_ref, sin_ref, cost_ref, sint_ref)
        hb = _project_kv(x_ref, gain_ref, sh_ref, sc_ref, wk_ref, wvt_ref, gb_ref, rope_refs,
                         k_ref, vt_ref, gr_ref)
        _project_q(hb, wqt_ref, rope_refs, qt_ref)
        for ci in range(per_tile):
            _advance_state(rc_ref, s_scr, c_scr, m_scr, 0, *_chunk_sources(rc_ref, k_ref, vt_ref, gr_ref, ci, 0),
                           (sf_ref, cf_ref, mf_ref, ci))
            amax, b_last, u_ret, u_ml = _chunk_sources(rc_ref, k_ref, vt_ref, gr_ref, ci, 1)
            chunk = j * per_tile + ci
            ub_stat[chunk] = jnp.concatenate([amax, b_last], axis=0)
            for hd in range(HEADS):
                ub_ret[chunk, hd] = u_ret[hd].astype(BF16)
                ub_ml[chunk, hd] = u_ml[hd].astype(BF16)

    @pl.when(j >= n_tiles)
    def _():
        first = n_chunks - 1 - (j - n_tiles) * bwd_chunks
        for i in range(bwd_chunks):
            chunk = first - i
            stat = ub_stat[chunk]
            _advance_state(rc_ref, s_scr, c_scr, m_scr, 1, stat[0:HEADS], stat[HEADS:2 * HEADS],
                           [ub_ret[chunk, hd].astype(F32) for hd in range(HEADS)],
                           [ub_ml[chunk, hd].astype(F32) for hd in range(HEADS)],
                           (sb_ref, cb_ref, mb_ref, bwd_chunks - 1 - i))


def _proj_scan(ret_consts, x, gain, shift, scale, wqt, wk, wvt, gbias, rope, init):
    b, n, d = x.shape
    L = SCAN_CHUNK
    t = PROJ_TOKENS
    nt = n // t
    nc = n // L
    per_tile = t // L
    nbb = nc // BWD_CHUNKS
    tile = lambda j: jnp.minimum(j, nt - 1)
    bblk = lambda j: nbb - 1 - jnp.maximum(j - nt, 0)
    tok3 = lambda i, j: (i, tile(j), 0)
    chunk4 = lambda i, j: (i, tile(j), 0, 0)
    mod3 = lambda i, j: (i, 0, 0)
    per_b = lambda shape: pl.BlockSpec((1,) + shape, lambda i, j: (i,) + (0,) * len(shape))
    cos, sin, cos_t, sin_t = rope
    in_specs = [pl.BlockSpec(memory_space=pltpu.SMEM),
                pl.BlockSpec((1, t, d), tok3),
                _resident((1, d)),
                pl.BlockSpec((1, 1, d), mod3), pl.BlockSpec((1, 1, d), mod3),
                _resident(wqt.shape), _resident(wk.shape), _resident(wvt.shape), _resident(gbias.shape),
                pl.BlockSpec((t, HEAD_DIM), lambda i, j: (tile(j), 0)),
                pl.BlockSpec((t, HEAD_DIM), lambda i, j: (tile(j), 0)),
                pl.BlockSpec((HEAD_DIM, t), lambda i, j: (0, tile(j))),
                pl.BlockSpec((HEAD_DIM, t), lambda i, j: (0, tile(j))),
                per_b(S_SHAPE), per_b(C_SHAPE), per_b(M_SHAPE)]
    out_shape = [jax.ShapeDtypeStruct((b, nc, 2 * MIX_W, L), BF16),
                 jax.ShapeDtypeStruct((b, n, 2 * MIX_W), BF16),
                 jax.ShapeDtypeStruct((b, nc, 2 * MIX_W, L), BF16),
                 jax.ShapeDtypeStruct((b, nc, 16, L), F32)]
    out_specs = [pl.BlockSpec((1, per_tile, 2 * MIX_W, L), chunk4),
                 pl.BlockSpec((1, t, 2 * MIX_W), tok3),
                 pl.BlockSpec((1, per_tile, 2 * MIX_W, L), chunk4),
                 pl.BlockSpec((1, per_tile, 16, L), chunk4)]
    for blk, step in ((per_tile, tile), (BWD_CHUNKS, bblk)):
        out_shape += [jax.ShapeDtypeStruct((b, nc) + S_SHAPE[1:], BF16),
                      jax.ShapeDtypeStruct((b, nc) + C_SHAPE[1:], BF16),
                      jax.ShapeDtypeStruct((b, nc, HEADS, HEAD_DIM), F32)]
        out_specs += [pl.BlockSpec((1, blk) + S_SHAPE[1:], lambda i, j, step=step: (i, step(j), 0, 0, 0)),
                      pl.BlockSpec((1, blk) + C_SHAPE[1:], lambda i, j, step=step: (i, step(j), 0, 0, 0)),
                      pl.BlockSpec((1, blk, HEADS, HEAD_DIM), lambda i, j, step=step: (i, step(j), 0, 0))]
    return pl.pallas_call(
        functools.partial(_proj_scan_kernel, n_tiles=nt, bwd_chunks=BWD_CHUNKS),
        grid=(b, nt + nbb),
        in_specs=in_specs,
        out_specs=tuple(out_specs),
        out_shape=tuple(out_shape),
        scratch_shapes=[pltpu.VMEM(S_SHAPE, F32), pltpu.VMEM(C_SHAPE, F32), pltpu.VMEM(M_SHAPE, F32),
                        pltpu.VMEM((nc,) + S_SHAPE[1:], BF16), pltpu.VMEM((nc,) + C_SHAPE[1:], BF16),
                        pltpu.VMEM((nc,) + M_SHAPE, F32)],
        compiler_params=pltpu.CompilerParams(dimension_semantics=("arbitrary", "arbitrary"),
                                             vmem_limit_bytes=VMEM_LIMIT),
        name="proj_scan",
    )(ret_consts, x, gain, shift, scale, wqt, wk, wvt, gbias, cos, sin, cos_t, sin_t, *init)


def _mixer_kernel(rc_ref, x_ref, gain_ref, sh_ref, sc_ref, g1_ref, qt_ref, k_ref, vt_ref, gr_ref,
                  sf_ref, sb_ref, cf_ref, cb_ref, mf_ref, mb_ref,
                  wgt_ref, wbr_ref, wbm_ref, wru_ref, wmu_ref, wout_ref, o_ref, dec_scr, qdec_scr):
    L = SCAN_CHUNK
    key = lax.broadcasted_iota(jnp.int32, (L, L), 0)
    qry = lax.broadcasted_iota(jnp.int32, (L, L), 1)

    @pl.when((pl.program_id(0) == 0) & (pl.program_id(1) == 0))
    def _():
        rel = (qry - key).astype(F32)
        lpos = lax.broadcasted_iota(jnp.int32, (8, L), 1).astype(F32)
        first_row = lax.broadcasted_iota(jnp.int32, (8, L), 0) == 0
        for hd in range(HEADS):
            lg_f = rc_ref[hd]
            lg_b = rc_ref[4 + hd]
            dec_scr[hd] = (jnp.where(rel >= 0, jnp.exp(lg_f * jnp.maximum(rel, 0.0)), 0.0)
                           + jnp.where(rel <= 0, jnp.exp(lg_b * jnp.maximum(-rel, 0.0)), 0.0))
            qdec_scr[hd] = jnp.where(first_row, jnp.exp(lg_f * (lpos + 1.0)), jnp.exp(lg_b * (L - lpos)))

    d_model = x_ref.shape[-1]
    bg_w = d_model // HEADS
    masks = (key <= qry, key >= qry)
    c_refs, m_refs = (cf_ref, cb_ref), (mf_ref, mb_ref)
    ones = jnp.ones((ONES_ROWS, L), BF16)
    neg_inf = jnp.float32(-jnp.inf)
    n_chunks = x_ref.shape[1] // L
    a_cols = [(gr_ref[0, ci, 0:8, :] * LOG2E).T for ci in range(n_chunks)]
    scanned, y_ret, y_ml, bg_ret, bg_ml = [], [], [], [], []
    normed = []

    def gate_head(hd):
        if not normed:
            normed.append((_rms(x_ref[0]) * gain_ref[...] * (1.0 + sc_ref[0]) + sh_ref[0]).astype(BF16))
        hb = normed[0]
        gates = _dot_nt(wgt_ref[2 * hd * HEAD_DIM:2 * (hd + 1) * HEAD_DIM, :], hb)
        bg_ret.append(_dot(hb, wbr_ref[:, hd * bg_w:(hd + 1) * bg_w]))
        bg_ml.append(_dot(hb, wbm_ref[:, hd * bg_w:(hd + 1) * bg_w]))
        ret_t, ml_t = scanned[hd]
        y_ret.append(ret_t * _silu(gates[:HEAD_DIM, :]))
        y_ml.append(_rms(_sigmoid(gates[HEAD_DIM:, :]) * ml_t, axis=0))

    chunks = range(n_chunks)
    toks = [slice(ci * L, (ci + 1) * L) for ci in chunks]

    def first_wave(hd):
        sl = slice(hd * HEAD_DIM, (hd + 1) * HEAD_DIM)
        ml = slice(MIX_W + hd * HEAD_DIM, MIX_W + (hd + 1) * HEAD_DIM)
        qt_r = [qt_ref[0, ci, sl, :] for ci in chunks]
        qt_m = [qt_ref[0, ci, ml, :] for ci in chunks]
        s_ret = [_dot(k_ref[0, toks[ci], sl], qt_r[ci]) for ci in chunks]
        s_ml = [_dot(k_ref[0, toks[ci], ml], qt_m[ci]) for ci in chunks]
        inter_ret, inter_ml = [], []
        for ci in chunks:
            qw = jnp.concatenate([qt_r[ci] * qdec_scr[hd, 0:1, :].astype(BF16),
                                  qt_r[ci] * qdec_scr[hd, 1:2, :].astype(BF16)], axis=0)
            st = jnp.concatenate([sf_ref[0, ci, hd], sb_ref[0, ci, hd]], axis=1)
            inter_ret.append(_dot(st, qw))
            inter_ml.append([_dot(c_refs[d][0, ci, hd], qt_m[ci]) for d in range(2)])
        return s_ret, s_ml, inter_ret, inter_ml

    wave = first_wave(0)
    for hd in range(HEADS):
        sl = slice(hd * HEAD_DIM, (hd + 1) * HEAD_DIM)
        ml = slice(MIX_W + hd * HEAD_DIM, MIX_W + (hd + 1) * HEAD_DIM)
        s_ret, s_ml, inter_ret, inter_ml = wave
        if hd + 1 < HEADS:
            wave = first_wave(hd + 1)
        p_ret = [(s_ret[ci] * dec_scr[hd]).astype(BF16) for ci in chunks]
        p_ml, m_locs, m_ins = [], [], []
        for ci in chunks:
            for d in range(2):
                a_col = a_cols[ci][:, 4 * d + hd:4 * d + hd + 1]
                m_in = jnp.concatenate([m_refs[d][0, ci, hd:hd + 1, :] * LOG2E] * (L // HEAD_DIM), axis=1)
                a_masked = jnp.where(masks[d], a_col, neg_inf)
                m_loc = jnp.maximum(jnp.max(a_masked, axis=0, keepdims=True), m_in)
                p_ml.append((s_ml[ci] * jnp.exp2(a_masked - m_loc)).astype(BF16))
                m_locs.append(m_loc)
                m_ins.append(m_in)
        ret_cols, ml_cols = [], []
        for ci in chunks:
            ret_cols.append(_rms(_dot(vt_ref[0, ci, sl, :], p_ret[ci]) + inter_ret[ci], axis=0))
            v_ext = jnp.concatenate([vt_ref[0, ci, ml, :], ones], axis=0)
            hsum = None
            for d in range(2):
                m_loc, m_in = m_locs[2 * ci + d], m_ins[2 * ci + d]
                cum_row = gr_ref[0, ci, 8 + 4 * d + hd:9 + 4 * d + hd, :] * LOG2E
                tot = _dot(v_ext, p_ml[2 * ci + d]) + jnp.exp2(m_in - m_loc) * inter_ml[ci][d]
                den = jnp.maximum(jnp.abs(tot[HEAD_DIM:HEAD_DIM + 1, :]), jnp.exp2(-(cum_row + m_loc)))
                hd_out = tot[0:HEAD_DIM, :] * (1.0 / den)
                hsum = hd_out if hsum is None else hsum + hd_out
            ml_cols.append(hsum)
        scanned.append((jnp.concatenate(ret_cols, axis=1), jnp.concatenate(ml_cols, axis=1)))
        if hd >= 1:
            gate_head(hd - 1)
    gate_head(HEADS - 1)

    y_ret = jnp.concatenate(y_ret, axis=0).astype(BF16)
    y_ml = jnp.concatenate(y_ml, axis=0).astype(BF16)
    bg_ret = jnp.concatenate(bg_ret, axis=1)
    bg_ml = jnp.concatenate(bg_ml, axis=1)
    merged = _sigmoid(bg_ret) * _dot_tn(y_ret, wru_ref[...]) + _sigmoid(bg_ml) * _dot_tn(y_ml, wmu_ref[...])
    o_ref[0] = x_ref[0] + g1_ref[0] * _dot(merged.astype(BF16), wout_ref[...])


def _mixer(ret_consts, x, gain, shift, scale, gate1, qt, k, vt, gr, chunk_states, wgt, wbr, wbm, wru, wmu, wout):
    b, n, d = x.shape
    L = SCAN_CHUNK
    t = MIX_TOKENS
    ch = t // L
    sf, sb, cf, cb, mf, mb = chunk_states
    tok3 = lambda i, j: (i, j, 0)
    mod3 = lambda i, j: (i, 0, 0)
    st5 = lambda i, j: (i, j, 0, 0, 0)
    st4 = lambda i, j: (i, j, 0, 0)
    in_specs = [pl.BlockSpec(memory_space=pltpu.SMEM),
                pl.BlockSpec((1, t, d), tok3),
                _resident((1, d)),
                pl.BlockSpec((1, 1, d), mod3), pl.BlockSpec((1, 1, d), mod3), pl.BlockSpec((1, 1, d), mod3),
                pl.BlockSpec((1, ch, 2 * MIX_W, L), st4),
                pl.BlockSpec((1, t, 2 * MIX_W), tok3),
                pl.BlockSpec((1, ch, 2 * MIX_W, L), st4),
                pl.BlockSpec((1, ch, 16, L), st4),
                pl.BlockSpec((1, ch) + S_SHAPE[1:], st5),
                pl.BlockSpec((1, ch) + S_SHAPE[1:], st5),
                pl.BlockSpec((1, ch) + C_SHAPE[1:], st5),
                pl.BlockSpec((1, ch) + C_SHAPE[1:], st5),
                pl.BlockSpec((1, ch, HEADS, HEAD_DIM), st4),
                pl.BlockSpec((1, ch, HEADS, HEAD_DIM), st4),
                _resident(wgt.shape),
                _resident(wbr.shape),
                _resident(wbm.shape),
                _resident(wru.shape),
                _resident(wmu.shape),
                _resident(wout.shape)]
    return pl.pallas_call(
        _mixer_kernel,
        grid=(b, n // t),
        in_specs=in_specs,
        out_specs=pl.BlockSpec((1, t, d), tok3),
        out_shape=jax.ShapeDtypeStruct((b, n, d), F32),
        scratch_shapes=[pltpu.VMEM((HEADS, L, L), F32), pltpu.VMEM((HEADS, 8, L), F32)],
        compiler_params=pltpu.CompilerParams(dimension_semantics=("arbitrary", "arbitrary"),
                                             vmem_limit_bytes=VMEM_LIMIT),
        name="mixer",
    )(ret_consts, x, gain, shift, scale, gate1, qt, k, vt, gr, sf, sb, cf, cb, mf, mb,
      wgt, wbr, wbm, wru, wmu, wout)


def _ffn_kernel(x_ref, gain_ref, sh_ref, sc_ref, g2_ref, fgain_ref, w1_ref, w2_ref, o_ref, *, hidden):
    xf = x_ref[0]
    h = _rms(xf) * gain_ref[...]
    hb = (h * (1.0 + sc_ref[0]) + sh_ref[0]).astype(BF16)
    acc = None
    start = 0
    for width in FFN_SPLITS:
        gate = _dot(hb, w1_ref[:, start:start + width])
        up = _dot(hb, w1_ref[:, hidden + start:hidden + start + width])
        act = (_silu(gate) * up).astype(BF16)
        part = _dot(act, w2_ref[start:start + width, :])
        acc = part if acc is None else acc + part
        start += width
    o_ref[0] = _rms(xf + g2_ref[0] * acc) * fgain_ref[...]


def _ffn(x, gain, shift, scale, gate2, final_gain, w1, w2):
    b, n, d = x.shape
    t = FFN_TOKENS
    hidden = w2.shape[0]
    assert sum(FFN_SPLITS) == hidden
    tok3 = lambda i, j: (i, j, 0)
    mod3 = lambda i, j: (i, 0, 0)
    return pl.pallas_call(
        functools.partial(_ffn_kernel, hidden=hidden),
        grid=(b, n // t),
        in_specs=[pl.BlockSpec((1, t, d), tok3),
                  _resident((1, d)),
                  pl.BlockSpec((1, 1, d), mod3), pl.BlockSpec((1, 1, d), mod3), pl.BlockSpec((1, 1, d), mod3),
                  _resident((1, d)),
                  _resident(w1.shape),
                  _resident(w2.shape)],
        out_specs=pl.BlockSpec((1, t, d), tok3),
        out_shape=jax.ShapeDtypeStruct((b, n, d), F32),
        compiler_params=pltpu.CompilerParams(dimension_semantics=("parallel", "parallel"),
                                             vmem_limit_bytes=VMEM_LIMIT),
        name="ffn",
    )(x, gain, shift, scale, gate2, final_gain, w1, w2)


def _rope_tables(n):
    n_rows = n // GRID_W
    n_freq = HEAD_DIM // 4
    inv = ROPE_BASE ** (-jnp.arange(n_freq, dtype=F32) / n_freq)
    row_ang = inv[:, None] * jnp.arange(n_rows, dtype=F32)[None, :]
    col_ang = inv[:, None] * jnp.arange(GRID_W, dtype=F32)[None, :]
    per_row = lambda t: jnp.repeat(t, GRID_W, axis=1)
    per_col = lambda t: jnp.tile(t, (1, n_rows))
    cos_t = jnp.concatenate([per_row(jnp.cos(row_ang)), per_col(jnp.cos(col_ang))], axis=0)
    sin_t = jnp.concatenate([per_row(jnp.sin(row_ang)), per_col(jnp.sin(col_ang))], axis=0)
    cos2_t = jnp.concatenate([cos_t, cos_t], axis=0)
    sin2_t = jnp.concatenate([-sin_t, sin_t], axis=0)
    return cos2_t.T, sin2_t.T, cos2_t, sin2_t


def kernel(x, c, ctx, c_ctx, w_ada, b_ada, norm1_gain, norm2_gain, w_in, mlstm_gate_bias, ret_decay_logit,
           w_ret_up, w_ml_up, w_out, w_ffn_in, w_ffn_out, final_gain):
    assert w_ada.shape[0] == 1, "single-layer block"
    b, n, d = x.shape
    assert n % PROJ_TOKENS == 0 and n % FFN_TOKENS == 0 and n % MIX_TOKENS == 0
    assert PROJ_TOKENS % SCAN_CHUNK == 0 and MIX_TOKENS % SCAN_CHUNK == 0 and ctx.shape[1] % SCAN_CHUNK == 0
    assert (n // SCAN_CHUNK) % BWD_CHUNKS == 0

    rows = -(-(b + 1) // 16) * 16
    cc = jnp.concatenate([c, c_ctx[None, :], jnp.zeros((rows - b - 1, d), F32)], axis=0)
    mod = _adaln(cc, w_ada[0], b_ada[0])
    sh1, sc1, g1, sh2, sc2, g2 = (mod[:b, i * d:(i + 1) * d].reshape(b, 1, d) for i in range(6))
    csh1 = mod[b, 0:d].reshape(1, 1, d)
    csc1 = mod[b, d:2 * d].reshape(1, 1, d)

    w = w_in[0]
    o = [0]
    for width in (MIX_W, MIX_W, MIX_W, MIX_W, MIX_W, MIX_W, MIX_W, MIX_W, 4 * HEADS, d, d):
        o.append(o[-1] + width)
    col = lambda i: w[:, o[i]:o[i + 1]]
    gates = col(8).reshape(d, 4, HEADS)
    gates = jnp.concatenate([gates[:, 0], gates[:, 2], gates[:, 1], gates[:, 3]], axis=1)
    gb = mlstm_gate_bias[0]
    gbias = jnp.concatenate([gb[0], gb[2], gb[1], gb[3]]).reshape(4 * HEADS, 1).astype(F32)
    wqt = jnp.concatenate([col(0), col(4)], axis=1).T.astype(BF16)
    wk = jnp.concatenate([col(1), col(5)], axis=1).astype(BF16)
    wvt = jnp.concatenate([col(2), col(6), gates], axis=1).T.astype(BF16)
    bg_w = d // HEADS
    wgt = jnp.stack([col(3).T.reshape(HEADS, HEAD_DIM, d), col(7).T.reshape(HEADS, HEAD_DIM, d)],
                    axis=1).reshape(2 * MIX_W, d).astype(BF16)
    wbr = col(9).astype(BF16)
    wbm = col(10).astype(BF16)

    log_gamma = jax.nn.log_sigmoid(ret_decay_logit[0].astype(F32)).reshape(2 * HEADS)
    ret_consts = jnp.concatenate([log_gamma, jnp.exp(log_gamma * SCAN_CHUNK)])

    gain1 = norm1_gain[0].reshape(1, d)
    ctx_final = _ctx_states(ret_consts, ctx, gain1, csh1, csc1, wk, wvt, gbias)
    qt_x, k_x, vt_x, gr_x, sf, cf, mf, sb, cb, mb = _proj_scan(
        ret_consts, x, gain1, sh1, sc1, wqt, wk, wvt, gbias, _rope_tables(n), ctx_final)
    x1 = _mixer(ret_consts, x, gain1, sh1, sc1, g1, qt_x, k_x, vt_x, gr_x, (sf, sb, cf, cb, mf, mb),
                wgt, wbr, wbm, w_ret_up[0].astype(BF16), w_ml_up[0].astype(BF16), w_out[0].astype(BF16))
    return _ffn(x1, norm2_gain[0].reshape(1, d), sh2, sc2, g2, final_gain.reshape(1, d),
                w_ffn_in[0].astype(BF16), w_ffn_out[0].astype(BF16))
```

```python
import functools

import jax
import jax.numpy as jnp
from jax import lax
from jax.experimental import pallas as pl
from jax.experimental.pallas import tpu as pltpu

HEADS = 4
HEAD_DIM = 128
MIX_W = HEADS * HEAD_DIM
ONES_ROWS = 16
GRID_W = 64
ROPE_BASE = 10000.0
EPS = 1e-6
LOG2E = 1.4426950408889634
SCAN_CHUNK = 256
CTX_GROUP = 4
PROJ_TOKENS = 1024
BWD_CHUNKS = 16
MIX_TOKENS = 1024
FFN_TOKENS = 1024
FFN_SPLITS = (1536, 1280)
VMEM_LIMIT = 60 * 1024 * 1024

F32 = jnp.float32
BF16 = jnp.bfloat16


def _resident(shape):
    return pl.BlockSpec(shape, lambda *_: (0,) * len(shape), pipeline_mode=pl.Buffered(1))


def _dot(a, b):
    return jnp.dot(a, b, preferred_element_type=F32)


def _dot_nt(a, b):
    return lax.dot_general(a, b, (((1,), (1,)), ((), ())), preferred_element_type=F32)


def _dot_tn(a, b):
    return lax.dot_general(a, b, (((0,), (0,)), ((), ())), preferred_element_type=F32)


def _sigmoid(t):
    return 0.5 * jnp.tanh(0.5 * t) + 0.5


def _silu(t):
    return t * _sigmoid(t)


def _log_sigmoid(t):
    return jnp.minimum(t, 0.0) - jnp.log1p(jnp.exp(-jnp.abs(t)))


def _rms(t, axis=-1):
    return t * lax.rsqrt(jnp.mean(t * t, axis=axis, keepdims=True) + EPS)


def _norm_modulate(x, gain, shift, scale):
    return (_rms(x) * (gain * (1.0 + scale)) + shift).astype(BF16)


def _split3(t):
    hi = t.astype(BF16)
    r1 = t - hi.astype(F32)
    mid = r1.astype(BF16)
    lo = (r1 - mid.astype(F32)).astype(BF16)
    return hi, mid, lo


def _adaln_kernel(c_ref, w_ref, b_ref, o_ref):
    s = _silu(c_ref[...]).astype(BF16)
    o_ref[...] = _dot(s, w_ref[...].astype(BF16)) + b_ref[...]


def _adaln(cc, w, b):
    rows, d = cc.shape
    cols = w.shape[1]
    blk = 1536
    return pl.pallas_call(
        _adaln_kernel,
        grid=(cols // blk,),
        in_specs=[pl.BlockSpec((rows, d), lambda j: (0, 0)),
                  pl.BlockSpec((d, blk), lambda j: (0, j)),
                  pl.BlockSpec((1, blk), lambda j: (0, j))],
        out_specs=pl.BlockSpec((rows, blk), lambda j: (0, j)),
        out_shape=jax.ShapeDtypeStruct((rows, cols), F32),
        compiler_params=pltpu.CompilerParams(dimension_semantics=("parallel",),
                                             vmem_limit_bytes=VMEM_LIMIT),
        name="adaln",
    )(cc, w, b.reshape(1, cols))


def _project_q(hb, wqt_ref, rope_refs, qt_ref):
    L = SCAN_CHUNK
    n_chunks = hb.shape[0] // L
    qt = _dot_nt(wqt_ref[...], hb)
    for hd in range(HEADS):
        sl = slice(hd * HEAD_DIM, (hd + 1) * HEAD_DIM)
        t = qt[sl, :]
        if rope_refs is not None:
            cost_ref, sint_ref = rope_refs[2], rope_refs[3]
            half = HEAD_DIM // 2
            rot = jnp.concatenate([t[half:, :], t[:half, :]], axis=0)
            t = t * cost_ref[...] + rot * sint_ref[...]
        for ci in range(n_chunks):
            qt_ref[0, ci, sl, :] = t[:, ci * L:(ci + 1) * L].astype(BF16)
    for ci in range(n_chunks):
        qt_ref[0, ci, MIX_W:, :] = qt[MIX_W:, ci * L:(ci + 1) * L].astype(BF16)


def _project_kv(x_ref, gain_ref, sh_ref, sc_ref, wk_ref, wvt_ref, gb_ref, rope_refs, k_ref, vt_ref, gr_ref):
    L = SCAN_CHUNK
    tokens = x_ref.shape[1]
    n_chunks = tokens // L
    hb = _norm_modulate(x_ref[0], gain_ref[...], sh_ref[0], sc_ref[0])

    vg =_dot_nt(wvt_ref[...], hb)
    for ci in range(n_chunks):
        vt_ref[0, ci] = vg[0:2 * MIX_W, ci * L:(ci + 1) * L].astype(BF16)

    g = vg[2 * MIX_W:, :] + gb_ref[...]
    i_pre = g[0:8, :]
    log_f = _log_sigmoid(g[8:16, :])
    src = lax.broadcasted_iota(jnp.int32, (L, L), 0)
    dst = lax.broadcasted_iota(jnp.int32, (L, L), 1)
    prefix_m = (src <= dst).astype(BF16)
    suffix_m = (src >= dst).astype(BF16)
    is_fwd = lax.broadcasted_iota(jnp.int32, (8, 1), 0) < HEADS
    parts = _split3(jnp.concatenate([log_f, log_f], axis=0))
    stacked = jnp.concatenate([p[:, ci * L:(ci + 1) * L] for ci in range(n_chunks) for p in parts], axis=0)
    pre_all = _dot(stacked, prefix_m)
    suf_all = _dot(stacked, suffix_m)
    for ci in range(n_chunks):
        cs = slice(ci * L, (ci + 1) * L)
        pieces = [slice((3 * ci + i) * 16, (3 * ci + i) * 16 + 8) for i in range(3)]
        pre = sum(pre_all[rows] for rows in pieces)
        suf = sum(suf_all[rows] for rows in pieces)
        cum = jnp.where(is_fwd, pre, suf)
        gr_ref[0, ci] = jnp.concatenate([i_pre[:, cs] - cum, cum], axis=0)

    k = _dot(hb, wk_ref[...]) * (HEAD_DIM ** -0.5)
    for hd in range(HEADS):
        sl = slice(hd * HEAD_DIM, (hd + 1) * HEAD_DIM)
        t = k[:, sl]
        if rope_refs is not None:
            cos_ref, sin_ref = rope_refs[0], rope_refs[1]
            t = t * cos_ref[...] + pltpu.roll(t, HEAD_DIM // 2, 1) * sin_ref[...]
        k_ref[0, :, sl] = t.astype(BF16)
    k_ref[0, :, MIX_W:] = k[:, MIX_W:].astype(BF16)
    return hb


S_SHAPE = (2, HEADS, HEAD_DIM, HEAD_DIM)
C_SHAPE = (2, HEADS, HEAD_DIM + ONES_ROWS, HEAD_DIM)
M_SHAPE = (2 * HEADS, HEAD_DIM)


def _chunk_sources(rc_ref, k_ref, vt_ref, gr_ref, ci, d):
    L = SCAN_CHUNK
    cs = slice(ci * L, (ci + 1) * L)
    pos = lax.broadcasted_iota(jnp.int32, (1, L), 1).astype(F32)
    ones = jnp.ones((ONES_ROWS, L), BF16)
    a = gr_ref[0, ci, 4 * d:4 * d + 4, :]
    cum = gr_ref[0, ci, 8 + 4 * d:12 + 4 * d, :]
    edge = cum[:, L - 1:L] if d == 0 else cum[:, 0:1]
    b_last = jnp.broadcast_to(edge, (HEADS, HEAD_DIM))
    amax = jnp.broadcast_to(jnp.max(a, axis=1, keepdims=True), (HEADS, HEAD_DIM))
    w_loc = jnp.exp(a - jnp.concatenate([amax] * (L // HEAD_DIM), axis=1))
    u_ret, u_ml = [], []
    for hd in range(HEADS):
        sl = slice(hd * HEAD_DIM, (hd + 1) * HEAD_DIM)
        ml = slice(MIX_W + hd * HEAD_DIM, MIX_W + (hd + 1) * HEAD_DIM)
        lg = rc_ref[4 * d + hd]
        to_end = jnp.exp(lg * ((L - 1.0) - pos)) if d == 0 else jnp.exp(lg * pos)
        u_ret.append(_dot(vt_ref[0, ci, sl, :] * to_end.astype(BF16), k_ref[0, cs, sl]))
        v_ext = jnp.concatenate([vt_ref[0, ci, ml, :], ones], axis=0)
        u_ml.append(_dot(v_ext * w_loc[hd:hd + 1, :].astype(BF16), k_ref[0, cs, ml]))
    return amax, b_last, u_ret, u_ml


def _advance_state(rc_ref, s_scr, c_scr, m_scr, d, amax, b_last, u_ret, u_ml, emit):
    rows = slice(HEADS * d, HEADS * (d + 1))
    m_old = m_scr[rows, :]
    m_mid = jnp.maximum(m_old, amax)
    w_old = jnp.exp(m_old - m_mid)
    w_new = jnp.exp(amax - m_mid)
    if emit is not None:
        s_out, c_out, m_out, slot = emit
        m_out[0, slot] = m_old
    m_scr[rows, :] = b_last + m_mid
    for hd in range(HEADS):
        s_prev = s_scr[d, hd]
        c_prev = c_scr[d, hd]
        if emit is not None:
            s_out[0, slot, hd] = s_prev.astype(BF16)
            c_out[0, slot, hd] = c_prev.astype(BF16)
        s_scr[d, hd] = rc_ref[2 * HEADS + HEADS * d + hd] * s_prev + u_ret[hd]
        c_scr[d, hd] = w_old[hd:hd + 1, :] * c_prev + w_new[hd:hd + 1, :] * u_ml[hd]


def _ctx_kernel(rc_ref, x_ref, gain_ref, sh_ref, sc_ref, wk_ref, wvt_ref, gb_ref,
                s_fin, c_fin, m_fin, k_scr, vt_scr, gr_scr, s_scr, c_scr, m_scr, *, group):
    L = SCAN_CHUNK
    _project_kv(x_ref, gain_ref, sh_ref, sc_ref, wk_ref, wvt_ref, gb_ref, None, k_scr, vt_scr, gr_scr)
    per_element = x_ref.shape[1] // L // group
    for g in range(group):
        s_scr[...] = jnp.zeros_like(s_scr)
        c_scr[...] = jnp.zeros_like(c_scr)
        m_scr[...] = jnp.zeros_like(m_scr)
        own = range(g * per_element, (g + 1) * per_element)
        for d in range(2):
            for ci in (own if d == 0 else reversed(own)):
                _advance_state(rc_ref, s_scr, c_scr, m_scr, d,
                               *_chunk_sources(rc_ref, k_scr, vt_scr, gr_scr, ci, d), None)
        s_fin[g] = s_scr[...]
        c_fin[g] = c_scr[...]
        m_fin[g] = m_scr[...]


def _ctx_states(ret_consts, ctx, gain, shift, scale, wk, wvt, gbias):
    b, n, d = ctx.shape
    group = CTX_GROUP if b % CTX_GROUP == 0 else 1
    tokens = group * n
    nck = tokens // SCAN_CHUNK
    blk = lambda shape: pl.BlockSpec((group,) + shape, lambda i: (i,) + (0,) * len(shape))
    return pl.pallas_call(
        functools.partial(_ctx_kernel, group=group),
        grid=(b // group,),
        in_specs=[pl.BlockSpec(memory_space=pltpu.SMEM),
                  pl.BlockSpec((1, tokens, d), lambda i: (i, 0, 0)),
                  _resident((1, d)), _resident((1, 1, d)), _resident((1, 1, d)),
                  _resident(wk.shape), _resident(wvt.shape), _resident(gbias.shape)],
        out_specs=(blk(S_SHAPE), blk(C_SHAPE), blk(M_SHAPE)),
        out_shape=tuple(jax.ShapeDtypeStruct((b,) + shape, F32) for shape in (S_SHAPE, C_SHAPE, M_SHAPE)),
        scratch_shapes=[pltpu.VMEM((1, tokens, 2 * MIX_W), BF16),
                        pltpu.VMEM((1, nck, 2 * MIX_W, SCAN_CHUNK), BF16),
                        pltpu.VMEM((1, nck, 16, SCAN_CHUNK), F32),
                        pltpu.VMEM(S_SHAPE, F32), pltpu.VMEM(C_SHAPE, F32), pltpu.VMEM(M_SHAPE, F32)],
        compiler_params=pltpu.CompilerParams(dimension_semantics=("arbitrary",),
                                             vmem_limit_bytes=VMEM_LIMIT),
        name="ctx_states",
    )(ret_consts, ctx.reshape(b // group, tokens, d), gain, shift, scale, wk, wvt, gbias)


def _proj_scan_kernel(rc_ref, x_ref, gain_ref, sh_ref, sc_ref, wqt_ref, wk_ref, wvt_ref, gb_ref,
                      cos_ref, sin_ref, cost_ref, sint_ref, s0_ref, c0_ref, m0_ref,
                      qt_ref, k_ref, vt_ref, gr_ref, sf_ref, cf_ref, mf_ref, sb_ref, cb_ref, mb_ref,
                      s_scr, c_scr, m_scr, ub_ret, ub_ml, ub_stat, *, n_tiles, bwd_chunks):
    L = SCAN_CHUNK
    j = pl.program_id(1)
    per_tile = x_ref.shape[1] // L
    n_chunks = n_tiles * per_tile

    @pl.when(j == 0)
    def _():
        s_scr[...] = s0_ref[0]
        c_scr[...] = c0_ref[0]
        m_scr[...] = m0_ref[0]

    @pl.when(j < n_tiles)
    def _():
        rope_refs = (cos_ref, sin_ref, cost_ref, sint_ref)
        hb = _project_kv(x_ref, gain_ref, sh_ref, sc_ref, wk_ref, wvt_ref, gb_ref, rope_refs,
                         k_ref, vt_ref, gr_ref)
        _project_q(hb, wqt_ref, rope_refs, qt_ref)
        for ci in range(per_tile):
            _advance_state(rc_ref, s_scr, c_scr, m_scr, 0, *_chunk_sources(rc_ref, k_ref, vt_ref, gr_ref, ci, 0),
                           (sf_ref, cf_ref, mf_ref, ci))
            amax, b_last, u_ret, u_ml = _chunk_sources(rc_ref, k_ref, vt_ref, gr_ref, ci, 1)
            chunk = j * per_tile + ci
            ub_stat[chunk] = jnp.concatenate([amax, b_last], axis=0)
            for hd in range(HEADS):
                ub_ret[chunk, hd] = u_ret[hd].astype(BF16)
                ub_ml[chunk, hd] = u_ml[hd].astype(BF16)

    @pl.when(j >= n_tiles)
    def _():
        first = n_chunks - 1 - (j - n_tiles) * bwd_chunks
        for i in range(bwd_chunks):
            chunk = first - i
            stat = ub_stat[chunk]
            _advance_state(rc_ref, s_scr, c_scr, m_scr, 1, stat[0:HEADS], stat[HEADS:2 * HEADS],
                           [ub_ret[chunk, hd].astype(F32) for hd in range(HEADS)],
                           [ub_ml[chunk, hd].astype(F32) for hd in range(HEADS)],
                           (sb_ref, cb_ref, mb_ref, bwd_chunks - 1 - i))


def _proj_scan(ret_consts, x, gain, shift, scale, wqt, wk, wvt, gbias, rope, init):
    b, n, d = x.shape
    L = SCAN_CHUNK
    t = PROJ_TOKENS
    nt = n // t
    nc = n // L
    per_tile = t // L
    nbb = nc // BWD_CHUNKS
    tile = lambda j: jnp.minimum(j, nt - 1)
    bblk = lambda j: nbb - 1 - jnp.maximum(j - nt, 0)
    tok3 = lambda i, j: (i, tile(j), 0)
    chunk4 = lambda i, j: (i, tile(j), 0, 0)
    mod3 = lambda i, j: (i, 0, 0)
    per_b = lambda shape: pl.BlockSpec((1,) + shape, lambda i, j: (i,) + (0,) * len(shape))
    cos, sin, cos_t, sin_t = rope
    in_specs = [pl.BlockSpec(memory_space=pltpu.SMEM),
                pl.BlockSpec((1, t, d), tok3),
                _resident((1, d)),
                pl.BlockSpec((1, 1, d), mod3), pl.BlockSpec((1, 1, d), mod3),
                _resident(wqt.shape), _resident(wk.shape), _resident(wvt.shape), _resident(gbias.shape),
                pl.BlockSpec((t, HEAD_DIM), lambda i, j: (tile(j), 0)),
                pl.BlockSpec((t, HEAD_DIM), lambda i, j: (tile(j), 0)),
                pl.BlockSpec((HEAD_DIM, t), lambda i, j: (0, tile(j))),
                pl.BlockSpec((HEAD_DIM, t), lambda i, j: (0, tile(j))),
                per_b(S_SHAPE), per_b(C_SHAPE), per_b(M_SHAPE)]
    out_shape = [jax.ShapeDtypeStruct((b, nc, 2 * MIX_W, L), BF16),
                 jax.ShapeDtypeStruct((b, n, 2 * MIX_W), BF16),
                 jax.ShapeDtypeStruct((b, nc, 2 * MIX_W, L), BF16),
                 jax.ShapeDtypeStruct((b, nc, 16, L), F32)]
    out_specs = [pl.BlockSpec((1, per_tile, 2 * MIX_W, L), chunk4),
                 pl.BlockSpec((1, t, 2 * MIX_W), tok3),
                 pl.BlockSpec((1, per_tile, 2 * MIX_W, L), chunk4),
                 pl.BlockSpec((1, per_tile, 16, L), chunk4)]
    for blk, step in ((per_tile, tile), (BWD_CHUNKS, bblk)):
        out_shape += [jax.ShapeDtypeStruct((b, nc) + S_SHAPE[1:], BF16),
                      jax.ShapeDtypeStruct((b, nc) + C_SHAPE[1:], BF16),
                      jax.ShapeDtypeStruct((b, nc, HEADS, HEAD_DIM), F32)]
        out_specs += [pl.BlockSpec((1, blk) + S_SHAPE[1:], lambda i, j, step=step: (i, step(j), 0, 0, 0)),
                      pl.BlockSpec((1, blk) + C_SHAPE[1:], lambda i, j, step=step: (i, step(j), 0, 0, 0)),
                      pl.BlockSpec((1, blk, HEADS, HEAD_DIM), lambda i, j, step=step: (i, step(j), 0, 0))]
    return pl.pallas_call(
        functools.partial(_proj_scan_kernel, n_tiles=nt, bwd_chunks=BWD_CHUNKS),
        grid=(b, nt + nbb),
        in_specs=in_specs,
        out_specs=tuple(out_specs),
        out_shape=tuple(out_shape),
        scratch_shapes=[pltpu.VMEM(S_SHAPE, F32), pltpu.VMEM(C_SHAPE, F32), pltpu.VMEM(M_SHAPE, F32),
                        pltpu.VMEM((nc,) + S_SHAPE[1:], BF16), pltpu.VMEM((nc,) + C_SHAPE[1:], BF16),
                        pltpu.VMEM((nc,) + M_SHAPE, F32)],
        compiler_params=pltpu.CompilerParams(dimension_semantics=("arbitrary", "arbitrary"),
                                             vmem_limit_bytes=VMEM_LIMIT),
        name="proj_scan",
    )(ret_consts, x, gain, shift, scale, wqt, wk, wvt, gbias, cos, sin, cos_t, sin_t, *init)


def _mixer_kernel(rc_ref, x_ref, gain_ref, sh_ref, sc_ref, g1_ref, qt_ref, k_ref, vt_ref, gr_ref,
                  sf_ref, sb_ref, cf_ref, cb_ref, mf_ref, mb_ref,
                  wgt_ref, wbr_ref, wbm_ref, wru_ref, wmu_ref, wout_ref, o_ref, dec_scr, qdec_scr):
    L = SCAN_CHUNK
    key = lax.broadcasted_iota(jnp.int32, (L, L), 0)
    qry = lax.broadcasted_iota(jnp.int32, (L, L), 1)

    @pl.when((pl.program_id(0) == 0) & (pl.program_id(1) == 0))
    def _():
        rel = (qry - key).astype(F32)
        lpos = lax.broadcasted_iota(jnp.int32, (8, L), 1).astype(F32)
        first_row = lax.broadcasted_iota(jnp.int32, (8, L), 0) == 0
        for hd in range(HEADS):
            lg_f = rc_ref[hd]
            lg_b = rc_ref[4 + hd]
            dec_scr[hd] = (jnp.where(rel >= 0, jnp.exp(lg_f * jnp.maximum(rel, 0.0)), 0.0)
                           + jnp.where(rel <= 0, jnp.exp(lg_b * jnp.maximum(-rel, 0.0)), 0.0))
            qdec_scr[hd] = jnp.where(first_row, jnp.exp(lg_f * (lpos + 1.0)), jnp.exp(lg_b * (L - lpos)))

    d_model = x_ref.shape[-1]
    bg_w = d_model // HEADS
    masks = (key <= qry, key >= qry)
    c_refs, m_refs = (cf_ref, cb_ref), (mf_ref, mb_ref)
    ones = jnp.ones((ONES_ROWS, L), BF16)
    neg_inf = jnp.float32(-jnp.inf)
    n_chunks = x_ref.shape[1] // L
    a_cols = [(gr_ref[0, ci, 0:8, :] * LOG2E).T for ci in range(n_chunks)]
    scanned, y_ret, y_ml, bg_ret, bg_ml = [], [], [], [], []
    normed = []

    def gate_head(hd):
        if not normed:
            normed.append(_norm_modulate(x_ref[0], gain_ref[...], sh_ref[0], sc_ref[0]))
        hb = normed[0]
        gates = _dot_nt(wgt_ref[2 * hd * HEAD_DIM:2 * (hd + 1) * HEAD_DIM, :], hb)
        bg_ret.append(_dot(hb, wbr_ref[:, hd * bg_w:(hd + 1) * bg_w]))
        bg_ml.append(_dot(hb, wbm_ref[:, hd * bg_w:(hd + 1) * bg_w]))
        ret_t, ml_t = scanned[hd]
        y_ret.append(ret_t * _silu(gates[:HEAD_DIM, :]))
        y_ml.append(_rms(_sigmoid(gates[HEAD_DIM:, :]) * ml_t, axis=0))

    chunks = range(n_chunks)
    toks = [slice(ci * L, (ci + 1) * L) for ci in chunks]

    def first_wave(hd):
        sl = slice(hd * HEAD_DIM, (hd + 1) * HEAD_DIM)
        ml = slice(MIX_W + hd * HEAD_DIM, MIX_W + (hd + 1) * HEAD_DIM)
        qt_r = [qt_ref[0, ci, sl, :] for ci in chunks]
        qt_m = [qt_ref[0, ci, ml, :] for ci in chunks]
        s_ret = [_dot(k_ref[0, toks[ci], sl], qt_r[ci]) for ci in chunks]
        s_ml = [_dot(k_ref[0, toks[ci], ml], qt_m[ci]) for ci in chunks]
        inter_ret, inter_ml = [], []
        for ci in chunks:
            qw = jnp.concatenate([qt_r[ci] * qdec_scr[hd, 0:1, :].astype(BF16),
                                  qt_r[ci] * qdec_scr[hd, 1:2, :].astype(BF16)], axis=0)
            st = jnp.concatenate([sf_ref[0, ci, hd], sb_ref[0, ci, hd]], axis=1)
            inter_ret.append(_dot(st, qw))
            inter_ml.append([_dot(c_refs[d][0, ci, hd], qt_m[ci]) for d in range(2)])
        return s_ret, s_ml, inter_ret, inter_ml

    wave = first_wave(0)
    for hd in range(HEADS):
        sl = slice(hd * HEAD_DIM, (hd + 1) * HEAD_DIM)
        ml = slice(MIX_W + hd * HEAD_DIM, MIX_W + (hd + 1) * HEAD_DIM)
        s_ret, s_ml, inter_ret, inter_ml = wave
        if hd + 1 < HEADS:
            wave = first_wave(hd + 1)
        p_ret = [(s_ret[ci] * dec_scr[hd]).astype(BF16) for ci in chunks]
        p_ml, m_locs, m_ins = [], [], []
        for ci in chunks:
            for d in range(2):
                a_col = a_cols[ci][:, 4 * d + hd:4 * d + hd + 1]
                m_in = jnp.concatenate([m_refs[d][0, ci, hd:hd + 1, :] * LOG2E] * (L // HEAD_DIM), axis=1)
                a_masked = jnp.where(masks[d], a_col, neg_inf)
                m_loc = jnp.maximum(jnp.max(a_masked, axis=0, keepdims=True), m_in)
                p_ml.append((s_ml[ci] * jnp.exp2(a_masked - m_loc)).astype(BF16))
                m_locs.append(m_loc)
                m_ins.append(m_in)
        ret_cols, ml_cols = [], []
        for ci in chunks:
            ret_cols.append(_rms(_dot(vt_ref[0, ci, sl, :], p_ret[ci]) + inter_ret[ci], axis=0))
            v_ext = jnp.concatenate([vt_ref[0, ci, ml, :], ones], axis=0)
            hsum = None
            for d in range(2):
                m_loc, m_in = m_locs[2 * ci + d], m_ins[2 * ci + d]
                cum_row = gr_ref[0, ci, 8 + 4 * d + hd:9 + 4 * d + hd, :] * LOG2E
                tot = _dot(v_ext, p_ml[2 * ci + d]) + jnp.exp2(m_in - m_loc) * inter_ml[ci][d]
                den = jnp.maximum(jnp.abs(tot[HEAD_DIM:HEAD_DIM + 1, :]), jnp.exp2(-(cum_row + m_loc)))
                hd_out = tot[0:HEAD_DIM, :] * (1.0 / den)
                hsum = hd_out if hsum is None else hsum + hd_out
            ml_cols.append(hsum)
        scanned.append((jnp.concatenate(ret_cols, axis=1), jnp.concatenate(ml_cols, axis=1)))
        if hd >= 1:
            gate_head(hd - 1)
    gate_head(HEADS - 1)

    y_ret = jnp.concatenate(y_ret, axis=0).astype(BF16)
    y_ml = jnp.concatenate(y_ml, axis=0).astype(BF16)
    bg_ret = jnp.concatenate(bg_ret, axis=1)
    bg_ml = jnp.concatenate(bg_ml, axis=1)
    merged = _sigmoid(bg_ret) * _dot_tn(y_ret, wru_ref[...]) + _sigmoid(bg_ml) * _dot_tn(y_ml, wmu_ref[...])
    o_ref[0] = x_ref[0] + g1_ref[0] * _dot(merged.astype(BF16), wout_ref[...])


def _mixer(ret_consts, x, gain, shift, scale, gate1, qt, k, vt, gr, chunk_states, wgt, wbr, wbm, wru, wmu, wout):
    b, n, d = x.shape
    L = SCAN_CHUNK
    t = MIX_TOKENS
    ch = t // L
    sf, sb, cf, cb, mf, mb = chunk_states
    tok3 = lambda i, j: (i, j, 0)
    mod3 = lambda i, j: (i, 0, 0)
    st5 = lambda i, j: (i, j, 0, 0, 0)
    st4 = lambda i, j: (i, j, 0, 0)
    in_specs = [pl.BlockSpec(memory_space=pltpu.SMEM),
                pl.BlockSpec((1, t, d), tok3),
                _resident((1, d)),
                pl.BlockSpec((1, 1, d), mod3), pl.BlockSpec((1, 1, d), mod3), pl.BlockSpec((1, 1, d), mod3),
                pl.BlockSpec((1, ch, 2 * MIX_W, L), st4),
                pl.BlockSpec((1, t, 2 * MIX_W), tok3),
                pl.BlockSpec((1, ch, 2 * MIX_W, L), st4),
                pl.BlockSpec((1, ch, 16, L), st4),
                pl.BlockSpec((1, ch) + S_SHAPE[1:], st5),
                pl.BlockSpec((1, ch) + S_SHAPE[1:], st5),
                pl.BlockSpec((1, ch) + C_SHAPE[1:], st5),
                pl.BlockSpec((1, ch) + C_SHAPE[1:], st5),
                pl.BlockSpec((1, ch, HEADS, HEAD_DIM), st4),
                pl.BlockSpec((1, ch, HEADS, HEAD_DIM), st4),
                _resident(wgt.shape),
                _resident(wbr.shape),
                _resident(wbm.shape),
                _resident(wru.shape),
                _resident(wmu.shape),
                _resident(wout.shape)]
    return pl.pallas_call(
        _mixer_kernel,
        grid=(b, n // t),
        in_specs=in_specs,
        out_specs=pl.BlockSpec((1, t, d), tok3),
        out_shape=jax.ShapeDtypeStruct((b, n, d), F32),
        scratch_shapes=[pltpu.VMEM((HEADS, L, L), F32), pltpu.VMEM((HEADS, 8, L), F32)],
        compiler_params=pltpu.CompilerParams(dimension_semantics=("arbitrary", "arbitrary"),
                                             vmem_limit_bytes=VMEM_LIMIT),
        name="mixer",
    )(ret_consts, x, gain, shift, scale, gate1, qt, k, vt, gr, sf, sb, cf, cb, mf, mb,
      wgt, wbr, wbm, wru, wmu, wout)


def _ffn_kernel(x_ref, gain_ref, sh_ref, sc_ref, g2_ref, fgain_ref, w1_ref, w2_ref, o_ref, *, hidden):
    xf = x_ref[0]
    hb = _norm_modulate(xf, gain_ref[...], sh_ref[0], sc_ref[0])
    acc = None
    start = 0
    for width in FFN_SPLITS:
        gate = _dot(hb, w1_ref[:, start:start + width])
        up = _dot(hb, w1_ref[:, hidden + start:hidden + start + width])
        act = (_silu(gate) * up).astype(BF16)
        part = _dot(act, w2_ref[start:start + width, :])
        acc = part if acc is None else acc + part
        start += width
    o_ref[0] = _rms(xf + g2_ref[0] * acc) * fgain_ref[...]


def _ffn(x, gain, shift, scale, gate2, final_gain, w1, w2):
    b, n, d = x.shape
    t = FFN_TOKENS
    hidden = w2.shape[0]
    assert sum(FFN_SPLITS) == hidden
    tok3 = lambda i, j: (i, j, 0)
    mod3 = lambda i, j: (i, 0, 0)
    return pl.pallas_call(
        functools.partial(_ffn_kernel, hidden=hidden),
        grid=(b, n // t),
        in_specs=[pl.BlockSpec((1, t, d), tok3),
                  _resident((1, d)),
                  pl.BlockSpec((1, 1, d), mod3), pl.BlockSpec((1, 1, d), mod3), pl.BlockSpec((1, 1, d), mod3),
                  _resident((1, d)),
                  _resident(w1.shape),
                  _resident(w2.shape)],
        out_specs=pl.BlockSpec((1, t, d), tok3),
        out_shape=jax.ShapeDtypeStruct((b, n, d), F32),
        compiler_params=pltpu.CompilerParams(dimension_semantics=("parallel", "parallel"),
                                             vmem_limit_bytes=VMEM_LIMIT),
        name="ffn",
    )(x, gain, shift, scale, gate2, final_gain, w1, w2)


def _rope_tables(n):
    n_rows = n // GRID_W
    n_freq = HEAD_DIM // 4
    inv = ROPE_BASE ** (-jnp.arange(n_freq, dtype=F32) / n_freq)
    row_ang = inv[:, None] * jnp.arange(n_rows, dtype=F32)[None, :]
    col_ang = inv[:, None] * jnp.arange(GRID_W, dtype=F32)[None, :]
    per_row = lambda t: jnp.repeat(t, GRID_W, axis=1)
    per_col = lambda t: jnp.tile(t, (1, n_rows))
    cos_t = jnp.concatenate([per_row(jnp.cos(row_ang)), per_col(jnp.cos(col_ang))], axis=0)
    sin_t = jnp.concatenate([per_row(jnp.sin(row_ang)), per_col(jnp.sin(col_ang))], axis=0)
    cos2_t = jnp.concatenate([cos_t, cos_t], axis=0)
    sin2_t = jnp.concatenate([-sin_t, sin_t], axis=0)
    return cos2_t.T, sin2_t.T, cos2_t, sin2_t


def kernel(x, c, ctx, c_ctx, w_ada, b_ada, norm1_gain, norm2_gain, w_in, mlstm_gate_bias, ret_decay_logit,
           w_ret_up, w_ml_up, w_out, w_ffn_in, w_ffn_out, final_gain):
    assert w_ada.shape[0] == 1, "single-layer block"
    b, n, d = x.shape
    assert n % PROJ_TOKENS == 0 and n % FFN_TOKENS == 0 and n % MIX_TOKENS == 0
    assert PROJ_TOKENS % SCAN_CHUNK == 0 and MIX_TOKENS % SCAN_CHUNK == 0 and ctx.shape[1] % SCAN_CHUNK == 0
    assert (n // SCAN_CHUNK) % BWD_CHUNKS == 0

    rows = -(-(b + 1) // 16) * 16
    cc = jnp.concatenate([c, c_ctx[None, :], jnp.zeros((rows - b - 1, d), F32)], axis=0)
    mod = _adaln(cc, w_ada[0], b_ada[0])
    sh1, sc1, g1, sh2, sc2, g2 = (mod[:b, i * d:(i + 1) * d].reshape(b, 1, d) for i in range(6))
    csh1 = mod[b, 0:d].reshape(1, 1, d)
    csc1 = mod[b, d:2 * d].reshape(1, 1, d)

    w = w_in[0]
    o = [0]
    for width in (MIX_W, MIX_W, MIX_W, MIX_W, MIX_W, MIX_W, MIX_W, MIX_W, 4 * HEADS, d, d):
        o.append(o[-1] + width)
    col = lambda i: w[:, o[i]:o[i + 1]]
    gates = col(8).reshape(d, 4, HEADS)
    gates = jnp.concatenate([gates[:, 0], gates[:, 2], gates[:, 1], gates[:, 3]], axis=1)
    gb = mlstm_gate_bias[0]
    gbias = jnp.concatenate([gb[0], gb[2], gb[1], gb[3]]).reshape(4 * HEADS, 1).astype(F32)
    wqt = jnp.concatenate([col(0), col(4)], axis=1).T.astype(BF16)
    wk = jnp.concatenate([col(1), col(5)], axis=1).astype(BF16)
    wvt = jnp.concatenate([col(2), col(6), gates], axis=1).T.astype(BF16)
    bg_w = d // HEADS
    wgt = jnp.stack([col(3).T.reshape(HEADS, HEAD_DIM, d), col(7).T.reshape(HEADS, HEAD_DIM, d)],
                    axis=1).reshape(2 * MIX_W, d).astype(BF16)
    wbr = col(9).astype(BF16)
    wbm = col(10).astype(BF16)

    log_gamma = jax.nn.log_sigmoid(ret_decay_logit[0].astype(F32)).reshape(2 * HEADS)
    ret_consts = jnp.concatenate([log_gamma, jnp.exp(log_gamma * SCAN_CHUNK)])

    gain1 = norm1_gain[0].reshape(1, d)
    ctx_final = _ctx_states(ret_consts, ctx, gain1, csh1, csc1, wk, wvt, gbias)
    qt_x, k_x, vt_x, gr_x, sf, cf, mf, sb, cb, mb = _proj_scan(
        ret_consts, x, gain1, sh1, sc1, wqt, wk, wvt, gbias, _rope_tables(n), ctx_final)
    x1 = _mixer(ret_consts, x, gain1, sh1, sc1, g1, qt_x, k_x, vt_x, gr_x, (sf, sb, cf, cb, mf, mb),
                wgt, wbr, wbm, w_ret_up[0].astype(BF16), w_ml_up[0].astype(BF16), w_out[0].astype(BF16))
    return _ffn(x1, norm2_gain[0].reshape(1, d), sh2, sc2, g2, final_gain.reshape(1, d),
                w_ffn_in[0].astype(BF16), w_ffn_out[0].astype(BF16))
```

```python
import functools

import jax
import jax.numpy as jnp
from jax import lax
from jax.experimental import pallas as pl
from jax.experimental.pallas import tpu as pltpu

HEADS = 4
HEAD_DIM = 128
MIX_W = HEADS * HEAD_DIM
KEY_SCALE = HEAD_DIM ** -0.5
HEAD_DIRS = 2 * HEADS
GATE_ROWS = 2 * HEAD_DIRS
ONES_ROWS = 16
GRID_W = 64
ROPE_BASE = 10000.0
EPS = 1e-6
LOG2E = 1.4426950408889634
SCAN_CHUNK = 256
CTX_GROUP = 4
PROJ_TOKENS = 1024
BWD_CHUNKS = 16
MIX_TOKENS = 1024
FFN_TOKENS = 1024
FFN_SPLITS = (1536, 1280)
VMEM_LIMIT = 60 * 1024 * 1024

F32 = jnp.float32
BF16 = jnp.bfloat16


def _resident(shape):
    return pl.BlockSpec(shape, lambda *_: (0,) * len(shape), pipeline_mode=pl.Buffered(1))


def _dot(a, b):
    return jnp.dot(a, b, preferred_element_type=F32)


def _dot_nt(a, b):
    return lax.dot_general(a, b, (((1,), (1,)), ((), ())), preferred_element_type=F32)


def _dot_tn(a, b):
    return lax.dot_general(a, b, (((0,), (0,)), ((), ())), preferred_element_type=F32)


def _sigmoid(t):
    return 0.5 * jnp.tanh(0.5 * t) + 0.5


def _silu(t):
    return t * _sigmoid(t)


def _log_sigmoid(t):
    return jnp.minimum(t, 0.0) - jnp.log1p(jnp.exp(-jnp.abs(t)))


def _rms(t, axis=-1):
    return t * lax.rsqrt(jnp.mean(t * t, axis=axis, keepdims=True) + EPS)


def _norm_modulate(x, gain, shift, scale):
    return (_rms(x) * (gain * (1.0 + scale)) + shift).astype(BF16)


def _split3(t):
    hi = t.astype(BF16)
    r1 = t - hi.astype(F32)
    mid = r1.astype(BF16)
    lo = (r1 - mid.astype(F32)).astype(BF16)
    return hi, mid, lo


def _adaln_kernel(c_ref, w_ref, b_ref, o_ref):
    s = _silu(c_ref[...]).astype(BF16)
    o_ref[...] = _dot(s, w_ref[...].astype(BF16)) + b_ref[...]


def _adaln(cc, w, b):
    rows, d = cc.shape
    cols = w.shape[1]
    blk = 1536
    return pl.pallas_call(
        _adaln_kernel,
        grid=(cols // blk,),
        in_specs=[pl.BlockSpec((rows, d), lambda j: (0, 0)),
                  pl.BlockSpec((d, blk), lambda j: (0, j)),
                  pl.BlockSpec((1, blk), lambda j: (0, j))],
        out_specs=pl.BlockSpec((rows, blk), lambda j: (0, j)),
        out_shape=jax.ShapeDtypeStruct((rows, cols), F32),
        compiler_params=pltpu.CompilerParams(dimension_semantics=("parallel",),
                                             vmem_limit_bytes=VMEM_LIMIT),
        name="adaln",
    )(cc, w, b.reshape(1, cols))


def _project_q(hb, wqt_ref, rope_refs, qt_ref):
    L = SCAN_CHUNK
    n_chunks = hb.shape[0] // L
    qt = _dot_nt(wqt_ref[...], hb)
    for hd in range(HEADS):
        sl = slice(hd * HEAD_DIM, (hd + 1) * HEAD_DIM)
        t = qt[sl, :]
        if rope_refs is not None:
            cost_ref, sint_ref = rope_refs[2], rope_refs[3]
            half = HEAD_DIM // 2
            rot = jnp.concatenate([t[half:, :], t[:half, :]], axis=0)
            t = t * cost_ref[...] + rot * sint_ref[...]
        for ci in range(n_chunks):
            qt_ref[0, ci, sl, :] = t[:, ci * L:(ci + 1) * L].astype(BF16)
    for ci in range(n_chunks):
        qt_ref[0, ci, MIX_W:, :] = qt[MIX_W:, ci * L:(ci + 1) * L].astype(BF16)


def _project_kv(x_ref, gain_ref, sh_ref, sc_ref, wk_ref, wvt_ref, gb_ref, rope_refs, k_ref, vt_ref, gr_ref):
    L = SCAN_CHUNK
    tokens = x_ref.shape[1]
    n_chunks = tokens // L
    hb = _norm_modulate(x_ref[0], gain_ref[...], sh_ref[0], sc_ref[0])

    vg =_dot_nt(wvt_ref[...], hb)
    for ci in range(n_chunks):
        vt_ref[0, ci] = vg[0:2 * MIX_W, ci * L:(ci + 1) * L].astype(BF16)

    g = vg[2 * MIX_W:, :] + gb_ref[...]
    i_pre = g[0:HEAD_DIRS, :]
    log_f = _log_sigmoid(g[HEAD_DIRS:GATE_ROWS, :])
    src = lax.broadcasted_iota(jnp.int32, (L, L), 0)
    dst = lax.broadcasted_iota(jnp.int32, (L, L), 1)
    prefix_m = (src <= dst).astype(BF16)
    suffix_m = (src >= dst).astype(BF16)
    is_fwd = lax.broadcasted_iota(jnp.int32, (HEAD_DIRS, 1), 0) < HEADS
    parts = _split3(jnp.concatenate([log_f, log_f], axis=0))
    stacked = jnp.concatenate([p[:, ci * L:(ci + 1) * L] for ci in range(n_chunks) for p in parts], axis=0)
    pre_all = _dot(stacked, prefix_m)
    suf_all = _dot(stacked, suffix_m)
    for ci in range(n_chunks):
        cs = slice(ci * L, (ci + 1) * L)
        pieces = [slice((3 * ci + i) * GATE_ROWS, (3 * ci + i) * GATE_ROWS + HEAD_DIRS) for i in range(3)]
        pre = sum(pre_all[rows] for rows in pieces)
        suf = sum(suf_all[rows] for rows in pieces)
        cum = jnp.where(is_fwd, pre, suf)
        gr_ref[0, ci] = jnp.concatenate([i_pre[:, cs] - cum, cum], axis=0)

    k = _dot(hb, wk_ref[...])
    for hd in range(HEADS):
        sl = slice(hd * HEAD_DIM, (hd + 1) * HEAD_DIM)
        t = k[:, sl]
        if rope_refs is not None:
            cos_ref, sin_ref = rope_refs[0], rope_refs[1]
            t = t * cos_ref[...] + pltpu.roll(t, HEAD_DIM // 2, 1) * sin_ref[...]
        else:
            t = t * KEY_SCALE
        k_ref[0, :, sl] = t.astype(BF16)
    k_ref[0, :, MIX_W:] = (k[:, MIX_W:] * KEY_SCALE).astype(BF16)
    return hb


S_SHAPE = (2, HEADS, HEAD_DIM, HEAD_DIM)
C_SHAPE = (2, HEADS, HEAD_DIM + ONES_ROWS, HEAD_DIM)
M_SHAPE = (2 * HEADS, HEAD_DIM)


def _chunk_sources(rc_ref, k_ref, vt_ref, gr_ref, ci, d):
    L = SCAN_CHUNK
    cs = slice(ci * L, (ci + 1) * L)
    pos = lax.broadcasted_iota(jnp.int32, (1, L), 1).astype(F32)
    ones = jnp.ones((ONES_ROWS, L), BF16)
    a = gr_ref[0, ci, HEADS * d:HEADS * (d + 1), :]
    cum = gr_ref[0, ci, HEAD_DIRS + HEADS * d:HEAD_DIRS + HEADS * (d + 1), :]
    edge = cum[:, L - 1:L] if d == 0 else cum[:, 0:1]
    b_last = jnp.broadcast_to(edge, (HEADS, HEAD_DIM))
    amax = jnp.broadcast_to(jnp.max(a, axis=1, keepdims=True), (HEADS, HEAD_DIM))
    w_loc = jnp.exp(a - jnp.concatenate([amax] * (L // HEAD_DIM), axis=1))
    u_ret, u_ml = [], []
    for hd in range(HEADS):
        sl = slice(hd * HEAD_DIM, (hd + 1) * HEAD_DIM)
        ml = slice(MIX_W + hd * HEAD_DIM, MIX_W + (hd + 1) * HEAD_DIM)
        lg = rc_ref[HEADS * d + hd]
        to_end = jnp.exp(lg * ((L - 1.0) - pos)) if d == 0 else jnp.exp(lg * pos)
        u_ret.append(_dot(vt_ref[0, ci, sl, :] * to_end.astype(BF16), k_ref[0, cs, sl]))
        v_ext = jnp.concatenate([vt_ref[0, ci, ml, :], ones], axis=0)
        u_ml.append(_dot(v_ext * w_loc[hd:hd + 1, :].astype(BF16), k_ref[0, cs, ml]))
    return amax, b_last, u_ret, u_ml


def _advance_state(rc_ref, s_scr, c_scr, m_scr, d, amax, b_last, u_ret, u_ml, emit):
    rows = slice(HEADS * d, HEADS * (d + 1))
    m_old = m_scr[rows, :]
    m_mid = jnp.maximum(m_old, amax)
    w_old = jnp.exp(m_old - m_mid)
    w_new = jnp.exp(amax - m_mid)
    if emit is not None:
        s_out, c_out, m_out, slot = emit
        m_out[0, slot] = m_old
    m_scr[rows, :] = b_last + m_mid
    for hd in range(HEADS):
        s_prev = s_scr[d, hd]
        c_prev = c_scr[d, hd]
        if emit is not None:
            s_out[0, slot, hd] = s_prev.astype(BF16)
            c_out[0, slot, hd] = c_prev.astype(BF16)
        s_scr[d, hd] = rc_ref[2 * HEADS + HEADS * d + hd] * s_prev + u_ret[hd]
        c_scr[d, hd] = w_old[hd:hd + 1, :] * c_prev + w_new[hd:hd + 1, :] * u_ml[hd]


def _ctx_kernel(rc_ref, x_ref, gain_ref, sh_ref, sc_ref, wk_ref, wvt_ref, gb_ref,
                s_fin, c_fin, m_fin, k_scr, vt_scr, gr_scr, s_scr, c_scr, m_scr, *, group):
    L = SCAN_CHUNK
    _project_kv(x_ref, gain_ref, sh_ref, sc_ref, wk_ref, wvt_ref, gb_ref, None, k_scr, vt_scr, gr_scr)
    per_element = x_ref.shape[1] // L // group
    for g in range(group):
        s_scr[...] = jnp.zeros_like(s_scr)
        c_scr[...] = jnp.zeros_like(c_scr)
        m_scr[...] = jnp.zeros_like(m_scr)
        own = range(g * per_element, (g + 1) * per_element)
        for d in range(2):
            for ci in (own if d == 0 else reversed(own)):
                _advance_state(rc_ref, s_scr, c_scr, m_scr, d,
                               *_chunk_sources(rc_ref, k_scr, vt_scr, gr_scr, ci, d), None)
        s_fin[g] = s_scr[...]
        c_fin[g] = c_scr[...]
        m_fin[g] = m_scr[...]


def _ctx_states(ret_consts, ctx, gain, shift, scale, wk, wvt, gbias):
    b, n, d = ctx.shape
    group = CTX_GROUP if b % CTX_GROUP == 0 else 1
    tokens = group * n
    nck = tokens // SCAN_CHUNK
    blk = lambda shape: pl.BlockSpec((group,) + shape, lambda i: (i,) + (0,) * len(shape))
    return pl.pallas_call(
        functools.partial(_ctx_kernel, group=group),
        grid=(b // group,),
        in_specs=[pl.BlockSpec(memory_space=pltpu.SMEM),
                  pl.BlockSpec((1, tokens, d), lambda i: (i, 0, 0)),
                  _resident((1, d)), _resident((1, 1, d)), _resident((1, 1, d)),
                  _resident(wk.shape), _resident(wvt.shape), _resident(gbias.shape)],
        out_specs=(blk(S_SHAPE), blk(C_SHAPE), blk(M_SHAPE)),
        out_shape=tuple(jax.ShapeDtypeStruct((b,) + shape, F32) for shape in (S_SHAPE, C_SHAPE, M_SHAPE)),
        scratch_shapes=[pltpu.VMEM((1, tokens, 2 * MIX_W), BF16),
                        pltpu.VMEM((1, nck, 2 * MIX_W, SCAN_CHUNK), BF16),
                        pltpu.VMEM((1, nck, GATE_ROWS, SCAN_CHUNK), F32),
                        pltpu.VMEM(S_SHAPE, F32), pltpu.VMEM(C_SHAPE, F32), pltpu.VMEM(M_SHAPE, F32)],
        compiler_params=pltpu.CompilerParams(dimension_semantics=("arbitrary",),
                                             vmem_limit_bytes=VMEM_LIMIT),
        name="ctx_states",
    )(ret_consts, ctx.reshape(b // group, tokens, d), gain, shift, scale, wk, wvt, gbias)


def _proj_scan_kernel(rc_ref, x_ref, gain_ref, sh_ref, sc_ref, wqt_ref, wk_ref, wvt_ref, gb_ref,
                      cos_ref, sin_ref, cost_ref, sint_ref, s0_ref, c0_ref, m0_ref,
                      qt_ref, k_ref, vt_ref, gr_ref, sf_ref, cf_ref, mf_ref, sb_ref, cb_ref, mb_ref,
                      s_scr, c_scr, m_scr, ub_ret, ub_ml, ub_stat, *, n_tiles, bwd_chunks):
    L = SCAN_CHUNK
    j = pl.program_id(1)
    per_tile = x_ref.shape[1] // L
    n_chunks = n_tiles * per_tile

    @pl.when(j == 0)
    def _():
        s_scr[...] = s0_ref[0]
        c_scr[...] = c0_ref[0]
        m_scr[...] = m0_ref[0]

    @pl.when(j < n_tiles)
    def _():
        rope_refs = (cos_ref, sin_ref, cost_ref, sint_ref)
        hb = _project_kv(x_ref, gain_ref, sh_ref, sc_ref, wk_ref, wvt_ref, gb_ref, rope_refs,
                         k_ref, vt_ref, gr_ref)
        _project_q(hb, wqt_ref, rope_refs, qt_ref)
        for ci in range(per_tile):
            _advance_state(rc_ref, s_scr, c_scr, m_scr, 0, *_chunk_sources(rc_ref, k_ref, vt_ref, gr_ref, ci, 0),
                           (sf_ref, cf_ref, mf_ref, ci))
            amax, b_last, u_ret, u_ml = _chunk_sources(rc_ref, k_ref, vt_ref, gr_ref, ci, 1)
            chunk = j * per_tile + ci
            ub_stat[chunk] = jnp.concatenate([amax, b_last], axis=0)
            for hd in range(HEADS):
                ub_ret[chunk, hd] = u_ret[hd].astype(BF16)
                ub_ml[chunk, hd] = u_ml[hd].astype(BF16)

    @pl.when(j >= n_tiles)
    def _():
        first = n_chunks - 1 - (j - n_tiles) * bwd_chunks
        for i in range(bwd_chunks):
            chunk = first - i
            stat = ub_stat[chunk]
            _advance_state(rc_ref, s_scr, c_scr, m_scr, 1, stat[0:HEADS], stat[HEADS:2 * HEADS],
                           [ub_ret[chunk, hd].astype(F32) for hd in range(HEADS)],
                           [ub_ml[chunk, hd].astype(F32) for hd in range(HEADS)],
                           (sb_ref, cb_ref, mb_ref, bwd_chunks - 1 - i))


def _proj_scan(ret_consts, x, gain, shift, scale, wqt, wk, wvt, gbias, rope, init):
    b, n, d = x.shape
    L = SCAN_CHUNK
    t = PROJ_TOKENS
    nt = n // t
    nc = n // L
    per_tile = t // L
    nbb = nc // BWD_CHUNKS
    tile = lambda j: jnp.minimum(j, nt - 1)
    bblk = lambda j: nbb - 1 - jnp.maximum(j - nt, 0)
    tok3 = lambda i, j: (i, tile(j), 0)
    chunk4 = lambda i, j: (i, tile(j), 0, 0)
    mod3 = lambda i, j: (i, 0, 0)
    per_b = lambda shape: pl.BlockSpec((1,) + shape, lambda i, j: (i,) + (0,) * len(shape))
    cos, sin, cos_t, sin_t = rope
    in_specs = [pl.BlockSpec(memory_space=pltpu.SMEM),
                pl.BlockSpec((1, t, d), tok3),
                _resident((1, d)),
                pl.BlockSpec((1, 1, d), mod3), pl.BlockSpec((1, 1, d), mod3),
                _resident(wqt.shape), _resident(wk.shape), _resident(wvt.shape), _resident(gbias.shape),
                pl.BlockSpec((t, HEAD_DIM), lambda i, j: (tile(j), 0)),
                pl.BlockSpec((t, HEAD_DIM), lambda i, j: (tile(j), 0)),
                pl.BlockSpec((HEAD_DIM, t), lambda i, j: (0, tile(j))),
                pl.BlockSpec((HEAD_DIM, t), lambda i, j: (0, tile(j))),
                per_b(S_SHAPE), per_b(C_SHAPE), per_b(M_SHAPE)]
    out_shape = [jax.ShapeDtypeStruct((b, nc, 2 * MIX_W, L), BF16),
                 jax.ShapeDtypeStruct((b, n, 2 * MIX_W), BF16),
                 jax.ShapeDtypeStruct((b, nc, 2 * MIX_W, L), BF16),
                 jax.ShapeDtypeStruct((b, nc, GATE_ROWS, L), F32)]
    out_specs = [pl.BlockSpec((1, per_tile, 2 * MIX_W, L), chunk4),
                 pl.BlockSpec((1, t, 2 * MIX_W), tok3),
                 pl.BlockSpec((1, per_tile, 2 * MIX_W, L), chunk4),
                 pl.BlockSpec((1, per_tile, GATE_ROWS, L), chunk4)]
    for blk, step in ((per_tile, tile), (BWD_CHUNKS, bblk)):
        out_shape += [jax.ShapeDtypeStruct((b, nc) + S_SHAPE[1:], BF16),
                      jax.ShapeDtypeStruct((b, nc) + C_SHAPE[1:], BF16),
                      jax.ShapeDtypeStruct((b, nc, HEADS, HEAD_DIM), F32)]
        out_specs += [pl.BlockSpec((1, blk) + S_SHAPE[1:], lambda i, j, step=step: (i, step(j), 0, 0, 0)),
                      pl.BlockSpec((1, blk) + C_SHAPE[1:], lambda i, j, step=step: (i, step(j), 0, 0, 0)),
                      pl.BlockSpec((1, blk, HEADS, HEAD_DIM), lambda i, j, step=step: (i, step(j), 0, 0))]
    return pl.pallas_call(
        functools.partial(_proj_scan_kernel, n_tiles=nt, bwd_chunks=BWD_CHUNKS),
        grid=(b, nt + nbb),
        in_specs=in_specs,
        out_specs=tuple(out_specs),
        out_shape=tuple(out_shape),
        scratch_shapes=[pltpu.VMEM(S_SHAPE, F32), pltpu.VMEM(C_SHAPE, F32), pltpu.VMEM(M_SHAPE, F32),
                        pltpu.VMEM((nc,) + S_SHAPE[1:], BF16), pltpu.VMEM((nc,) + C_SHAPE[1:], BF16),
                        pltpu.VMEM((nc,) + M_SHAPE, F32)],
        compiler_params=pltpu.CompilerParams(dimension_semantics=("arbitrary", "arbitrary"),
                                             vmem_limit_bytes=VMEM_LIMIT),
        name="proj_scan",
    )(ret_consts, x, gain, shift, scale, wqt, wk, wvt, gbias, cos, sin, cos_t, sin_t, *init)


def _mixer_kernel(rc_ref, x_ref, gain_ref, sh_ref, sc_ref, g1_ref, qt_ref, k_ref, vt_ref, gr_ref,
                  sf_ref, sb_ref, cf_ref, cb_ref, mf_ref, mb_ref,
                  wgt_ref, wbr_ref, wbm_ref, wru_ref, wmu_ref, wout_ref, o_ref, dec_scr, qdec_scr):
    L = SCAN_CHUNK
    key = lax.broadcasted_iota(jnp.int32, (L, L), 0)
    qry = lax.broadcasted_iota(jnp.int32, (L, L), 1)

    @pl.when((pl.program_id(0) == 0) & (pl.program_id(1) == 0))
    def _():
        rel = (qry - key).astype(F32)
        lpos = lax.broadcasted_iota(jnp.int32, (8, L), 1).astype(F32)
        first_row = lax.broadcasted_iota(jnp.int32, (8, L), 0) == 0
        for hd in range(HEADS):
            lg_f = rc_ref[hd]
            lg_b = rc_ref[HEADS + hd]
            dec_scr[hd] = (jnp.where(rel >= 0, jnp.exp(lg_f * jnp.maximum(rel, 0.0)), 0.0)
                           + jnp.where(rel <= 0, jnp.exp(lg_b * jnp.maximum(-rel, 0.0)), 0.0))
            qdec_scr[hd] = jnp.where(first_row, jnp.exp(lg_f * (lpos + 1.0)), jnp.exp(lg_b * (L - lpos)))

    d_model = x_ref.shape[-1]
    bg_w = d_model // HEADS
    masks = (key <= qry, key >= qry)
    c_refs, m_refs = (cf_ref, cb_ref), (mf_ref, mb_ref)
    ones = jnp.ones((ONES_ROWS, L), BF16)
    neg_inf = jnp.float32(-jnp.inf)
    n_chunks = x_ref.shape[1] // L
    a_cols = [(gr_ref[0, ci, 0:HEAD_DIRS, :] * LOG2E).T for ci in range(n_chunks)]
    scanned, y_ret, y_ml, bg_ret, bg_ml = [], [], [], [], []
    normed = []

    def gate_head(hd):
        if not normed:
            normed.append(_norm_modulate(x_ref[0], gain_ref[...], sh_ref[0], sc_ref[0]))
        hb = normed[0]
        gates = _dot_nt(wgt_ref[2 * hd * HEAD_DIM:2 * (hd + 1) * HEAD_DIM, :], hb)
        bg_ret.append(_dot(hb, wbr_ref[:, hd * bg_w:(hd + 1) * bg_w]))
        bg_ml.append(_dot(hb, wbm_ref[:, hd * bg_w:(hd + 1) * bg_w]))
        ret_t, ml_t = scanned[hd]
        y_ret.append(ret_t * _silu(gates[:HEAD_DIM, :]))
        y_ml.append(_rms(_sigmoid(gates[HEAD_DIM:, :]) * ml_t, axis=0))

    chunks = range(n_chunks)
    toks = [slice(ci * L, (ci + 1) * L) for ci in chunks]

    def first_wave(hd):
        sl = slice(hd * HEAD_DIM, (hd + 1) * HEAD_DIM)
        ml = slice(MIX_W + hd * HEAD_DIM, MIX_W + (hd + 1) * HEAD_DIM)
        qt_r = [qt_ref[0, ci, sl, :] for ci in chunks]
        qt_m = [qt_ref[0, ci, ml, :] for ci in chunks]
        s_ret = [_dot(k_ref[0, toks[ci], sl], qt_r[ci]) for ci in chunks]
        s_ml = [_dot(k_ref[0, toks[ci], ml], qt_m[ci]) for ci in chunks]
        inter_ret, inter_ml = [], []
        for ci in chunks:
            qw = jnp.concatenate([qt_r[ci] * qdec_scr[hd, 0:1, :].astype(BF16),
                                  qt_r[ci] * qdec_scr[hd, 1:2, :].astype(BF16)], axis=0)
            st = jnp.concatenate([sf_ref[0, ci, hd], sb_ref[0, ci, hd]], axis=1)
            inter_ret.append(_dot(st, qw))
            inter_ml.append([_dot(c_refs[d][0, ci, hd], qt_m[ci]) for d in range(2)])
        return s_ret, s_ml, inter_ret, inter_ml

    wave = first_wave(0)
    for hd in range(HEADS):
        sl = slice(hd * HEAD_DIM, (hd + 1) * HEAD_DIM)
        ml = slice(MIX_W + hd * HEAD_DIM, MIX_W + (hd + 1) * HEAD_DIM)
        s_ret, s_ml, inter_ret, inter_ml = wave
        if hd + 1 < HEADS:
            wave = first_wave(hd + 1)
        p_ret = [(s_ret[ci] * dec_scr[hd]).astype(BF16) for ci in chunks]
        p_ml, m_locs, m_ins = [], [], []
        for ci in chunks:
            for d in range(2):
                pair = HEADS * d + hd
                a_col = a_cols[ci][:, pair:pair + 1]
                m_in = jnp.concatenate([m_refs[d][0, ci, hd:hd + 1, :] * LOG2E] * (L // HEAD_DIM), axis=1)
                a_masked = jnp.where(masks[d], a_col, neg_inf)
                m_loc = jnp.maximum(jnp.max(a_masked, axis=0, keepdims=True), m_in)
                p_ml.append((s_ml[ci] * jnp.exp2(a_masked - m_loc)).astype(BF16))
                m_locs.append(m_loc)
                m_ins.append(m_in)
        ret_cols, ml_cols = [], []
        for ci in chunks:
            ret_cols.append(_rms(_dot(vt_ref[0, ci, sl, :], p_ret[ci]) + inter_ret[ci], axis=0))
            v_ext = jnp.concatenate([vt_ref[0, ci, ml, :], ones], axis=0)
            hsum = None
            for d in range(2):
                m_loc, m_in = m_locs[2 * ci + d], m_ins[2 * ci + d]
                pair = HEAD_DIRS + HEADS * d + hd
                cum_row = gr_ref[0, ci, pair:pair + 1, :] * LOG2E
                tot = _dot(v_ext, p_ml[2 * ci + d]) + jnp.exp2(m_in - m_loc) * inter_ml[ci][d]
                den = jnp.maximum(jnp.abs(tot[HEAD_DIM:HEAD_DIM + 1, :]), jnp.exp2(-(cum_row + m_loc)))
                hd_out = tot[0:HEAD_DIM, :] * (1.0 / den)
                hsum = hd_out if hsum is None else hsum + hd_out
            ml_cols.append(hsum)
        scanned.append((jnp.concatenate(ret_cols, axis=1), jnp.concatenate(ml_cols, axis=1)))
        if hd >= 1:
            gate_head(hd - 1)
    gate_head(HEADS - 1)

    y_ret = jnp.concatenate(y_ret, axis=0).astype(BF16)
    y_ml = jnp.concatenate(y_ml, axis=0).astype(BF16)
    bg_ret = jnp.concatenate(bg_ret, axis=1)
    bg_ml = jnp.concatenate(bg_ml, axis=1)
    merged = _sigmoid(bg_ret) * _dot_tn(y_ret, wru_ref[...]) + _sigmoid(bg_ml) * _dot_tn(y_ml, wmu_ref[...])
    o_ref[0] = x_ref[0] + g1_ref[0] * _dot(merged.astype(BF16), wout_ref[...])


def _mixer(ret_consts, x, gain, shift, scale, gate1, qt, k, vt, gr, chunk_states, wgt, wbr, wbm, wru, wmu, wout):
    b, n, d = x.shape
    L = SCAN_CHUNK
    t = MIX_TOKENS
    ch = t // L
    sf, sb, cf, cb, mf, mb = chunk_states
    tok3 = lambda i, j: (i, j, 0)
    mod3 = lambda i, j: (i, 0, 0)
    st5 = lambda i, j: (i, j, 0, 0, 0)
    st4 = lambda i, j: (i, j, 0, 0)
    in_specs = [pl.BlockSpec(memory_space=pltpu.SMEM),
                pl.BlockSpec((1, t, d), tok3),
                _resident((1, d)),
                pl.BlockSpec((1, 1, d), mod3), pl.BlockSpec((1, 1, d), mod3), pl.BlockSpec((1, 1, d), mod3),
                pl.BlockSpec((1, ch, 2 * MIX_W, L), st4),
                pl.BlockSpec((1, t, 2 * MIX_W), tok3),
                pl.BlockSpec((1, ch, 2 * MIX_W, L), st4),
                pl.BlockSpec((1, ch, GATE_ROWS, L), st4),
                pl.BlockSpec((1, ch) + S_SHAPE[1:], st5),
                pl.BlockSpec((1, ch) + S_SHAPE[1:], st5),
                pl.BlockSpec((1, ch) + C_SHAPE[1:], st5),
                pl.BlockSpec((1, ch) + C_SHAPE[1:], st5),
                pl.BlockSpec((1, ch, HEADS, HEAD_DIM), st4),
                pl.BlockSpec((1, ch, HEADS, HEAD_DIM), st4),
                _resident(wgt.shape),
                _resident(wbr.shape),
                _resident(wbm.shape),
                _resident(wru.shape),
                _resident(wmu.shape),
                _resident(wout.shape)]
    return pl.pallas_call(
        _mixer_kernel,
        grid=(b, n // t),
        in_specs=in_specs,
        out_specs=pl.BlockSpec((1, t, d), tok3),
        out_shape=jax.ShapeDtypeStruct((b, n, d), F32),
        scratch_shapes=[pltpu.VMEM((HEADS, L, L), F32), pltpu.VMEM((HEADS, 8, L), F32)],
        compiler_params=pltpu.CompilerParams(dimension_semantics=("arbitrary", "arbitrary"),
                                             vmem_limit_bytes=VMEM_LIMIT),
        name="mixer",
    )(ret_consts, x, gain, shift, scale, gate1, qt, k, vt, gr, sf, sb, cf, cb, mf, mb,
      wgt, wbr, wbm, wru, wmu, wout)


def _ffn_kernel(x_ref, gain_ref, sh_ref, sc_ref, g2_ref, fgain_ref, w1_ref, w2_ref, o_ref, *, hidden):
    xf = x_ref[0]
    hb = _norm_modulate(xf, gain_ref[...], sh_ref[0], sc_ref[0])
    acc = None
    start = 0
    for width in FFN_SPLITS:
        gate = _dot(hb, w1_ref[:, start:start + width])
        up = _dot(hb, w1_ref[:, hidden + start:hidden + start + width])
        act = (_silu(gate) * up).astype(BF16)
        part = _dot(act, w2_ref[start:start + width, :])
        acc = part if acc is None else acc + part
        start += width
    o_ref[0] = _rms(xf + g2_ref[0] * acc) * fgain_ref[...]


def _ffn(x, gain, shift, scale, gate2, final_gain, w1, w2):
    b, n, d = x.shape
    t = FFN_TOKENS
    hidden = w2.shape[0]
    assert sum(FFN_SPLITS) == hidden
    tok3 = lambda i, j: (i, j, 0)
    mod3 = lambda i, j: (i, 0, 0)
    return pl.pallas_call(
        functools.partial(_ffn_kernel, hidden=hidden),
        grid=(b, n // t),
        in_specs=[pl.BlockSpec((1, t, d), tok3),
                  _resident((1, d)),
                  pl.BlockSpec((1, 1, d), mod3), pl.BlockSpec((1, 1, d), mod3), pl.BlockSpec((1, 1, d), mod3),
                  _resident((1, d)),
                  _resident(w1.shape),
                  _resident(w2.shape)],
        out_specs=pl.BlockSpec((1, t, d), tok3),
        out_shape=jax.ShapeDtypeStruct((b, n, d), F32),
        compiler_params=pltpu.CompilerParams(dimension_semantics=("parallel", "parallel"),
                                             vmem_limit_bytes=VMEM_LIMIT),
        name="ffn",
    )(x, gain, shift, scale, gate2, final_gain, w1, w2)


def _rope_tables(n):
    n_rows = n // GRID_W
    n_freq = HEAD_DIM // 4
    inv = ROPE_BASE ** (-jnp.arange(n_freq, dtype=F32) / n_freq)
    row_ang = inv[:, None] * jnp.arange(n_rows, dtype=F32)[None, :]
    col_ang = inv[:, None] * jnp.arange(GRID_W, dtype=F32)[None, :]
    per_row = lambda t: jnp.repeat(t, GRID_W, axis=1)
    per_col = lambda t: jnp.tile(t, (1, n_rows))
    cos_t = jnp.concatenate([per_row(jnp.cos(row_ang)), per_col(jnp.cos(col_ang))], axis=0)
    sin_t = jnp.concatenate([per_row(jnp.sin(row_ang)), per_col(jnp.sin(col_ang))], axis=0)
    cos2_t = jnp.concatenate([cos_t, cos_t], axis=0)
    sin2_t = jnp.concatenate([-sin_t, sin_t], axis=0)
    return cos2_t.T * KEY_SCALE, sin2_t.T * KEY_SCALE, cos2_t, sin2_t


def kernel(x, c, ctx, c_ctx, w_ada, b_ada, norm1_gain, norm2_gain, w_in, mlstm_gate_bias, ret_decay_logit,
           w_ret_up, w_ml_up, w_out, w_ffn_in, w_ffn_out, final_gain):
    assert w_ada.shape[0] == 1, "single-layer block"
    b, n, d = x.shape
    assert n % PROJ_TOKENS == 0 and n % FFN_TOKENS == 0 and n % MIX_TOKENS == 0
    assert PROJ_TOKENS % SCAN_CHUNK == 0 and MIX_TOKENS % SCAN_CHUNK == 0 and ctx.shape[1] % SCAN_CHUNK == 0
    assert (n // SCAN_CHUNK) % BWD_CHUNKS == 0

    rows = -(-(b + 1) // 16) * 16
    cc = jnp.concatenate([c, c_ctx[None, :], jnp.zeros((rows - b - 1, d), F32)], axis=0)
    mod = _adaln(cc, w_ada[0], b_ada[0])
    sh1, sc1, g1, sh2, sc2, g2 = (mod[:b, i * d:(i + 1) * d].reshape(b, 1, d) for i in range(6))
    csh1 = mod[b, 0:d].reshape(1, 1, d)
    csc1 = mod[b, d:2 * d].reshape(1, 1, d)

    w = w_in[0]
    o = [0]
    for width in (MIX_W, MIX_W, MIX_W, MIX_W, MIX_W, MIX_W, MIX_W, MIX_W, GATE_ROWS, d, d):
        o.append(o[-1] + width)
    col = lambda i: w[:, o[i]:o[i + 1]]
    gates = col(8).reshape(d, 4, HEADS)
    gates = jnp.concatenate([gates[:, 0], gates[:, 2], gates[:, 1], gates[:, 3]], axis=1)
    gb = mlstm_gate_bias[0]
    gbias = jnp.concatenate([gb[0], gb[2], gb[1], gb[3]]).reshape(GATE_ROWS, 1).astype(F32)
    wqt = jnp.concatenate([col(0), col(4)], axis=1).T.astype(BF16)
    wk = jnp.concatenate([col(1), col(5)], axis=1).astype(BF16)
    wvt = jnp.concatenate([col(2), col(6), gates], axis=1).T.astype(BF16)
    bg_w = d // HEADS
    wgt = jnp.stack([col(3).T.reshape(HEADS, HEAD_DIM, d), col(7).T.reshape(HEADS, HEAD_DIM, d)],
                    axis=1).reshape(2 * MIX_W, d).astype(BF16)
    wbr = col(9).astype(BF16)
    wbm = col(10).astype(BF16)

    log_gamma = jax.nn.log_sigmoid(ret_decay_logit[0].astype(F32)).reshape(2 * HEADS)
    ret_consts = jnp.concatenate([log_gamma, jnp.exp(log_gamma * SCAN_CHUNK)])

    gain1 = norm1_gain[0].reshape(1, d)
    ctx_final = _ctx_states(ret_consts, ctx, gain1, csh1, csc1, wk, wvt, gbias)
    qt_x, k_x, vt_x, gr_x, sf, cf, mf, sb, cb, mb = _proj_scan(
        ret_consts, x, gain1, sh1, sc1, wqt, wk, wvt, gbias, _rope_tables(n), ctx_final)
    x1 = _mixer(ret_consts, x, gain1, sh1, sc1, g1, qt_x, k_x, vt_x, gr_x, (sf, sb, cf, cb, mf, mb),
                wgt, wbr, wbm, w_ret_up[0].astype(BF16), w_ml_up[0].astype(BF16), w_out[0].astype(BF16))
    return _ffn(x1, norm2_gain[0].reshape(1, d), sh2, sc2, g2, final_gain.reshape(1, d),
                w_ffn_in[0].astype(BF16), w_ffn_out[0].astype(BF16))
```

```python
import functools

import jax
import jax.numpy as jnp
from jax import lax
from jax.experimental import pallas as pl
from jax.experimental.pallas import tpu as pltpu

HEADS = 4
HEAD_DIM = 128
MIX_W = HEADS * HEAD_DIM
KEY_SCALE = HEAD_DIM ** -0.5
HEAD_DIRS = 2 * HEADS
GATE_ROWS = 2 * HEAD_DIRS
ONES_ROWS = 16
GRID_W = 64
ROPE_BASE = 10000.0
EPS = 1e-6
LOG2E = 1.4426950408889634
SCAN_CHUNK = 256
CTX_GROUP = 4
PROJ_TOKENS = 1024
BWD_CHUNKS = 16
MIX_TOKENS = 1024
FFN_TOKENS = 1024
FFN_SPLITS = (1536, 1280)
VMEM_LIMIT = 60 * 1024 * 1024

F32 = jnp.float32
BF16 = jnp.bfloat16


def _resident(shape):
    return pl.BlockSpec(shape, lambda *_: (0,) * len(shape), pipeline_mode=pl.Buffered(1))


def _dot(a, b):
    return jnp.dot(a, b, preferred_element_type=F32)


def _dot_nt(a, b):
    return lax.dot_general(a, b, (((1,), (1,)), ((), ())), preferred_element_type=F32)


def _dot_tn(a, b):
    return lax.dot_general(a, b, (((0,), (0,)), ((), ())), preferred_element_type=F32)


def _sigmoid(t):
    return 0.5 * jnp.tanh(0.5 * t) + 0.5


def _silu(t):
    return t * _sigmoid(t)


def _log_sigmoid(t):
    return jnp.minimum(t, 0.0) - jnp.log1p(jnp.exp(-jnp.abs(t)))


def _rms(t, axis=-1):
    return t * lax.rsqrt(jnp.mean(t * t, axis=axis, keepdims=True) + EPS)


def _norm_modulate(x, gain, shift, scale):
    return (_rms(x) * (gain * (1.0 + scale)) + shift).astype(BF16)


def _split3(t):
    hi = t.astype(BF16)
    r1 = t - hi.astype(F32)
    mid = r1.astype(BF16)
    lo = (r1 - mid.astype(F32)).astype(BF16)
    return hi, mid, lo


def _adaln_kernel(c_ref, w_ref, b_ref, o_ref):
    s = _silu(c_ref[...]).astype(BF16)
    o_ref[...] = _dot(s, w_ref[...].astype(BF16)) + b_ref[...]


def _adaln(cc, w, b):
    rows, d = cc.shape
    cols = w.shape[1]
    blk = 1536
    return pl.pallas_call(
        _adaln_kernel,
        grid=(cols // blk,),
        in_specs=[pl.BlockSpec((rows, d), lambda j: (0, 0)),
                  pl.BlockSpec((d, blk), lambda j: (0, j)),
                  pl.BlockSpec((1, blk), lambda j: (0, j))],
        out_specs=pl.BlockSpec((rows, blk), lambda j: (0, j)),
        out_shape=jax.ShapeDtypeStruct((rows, cols), F32),
        compiler_params=pltpu.CompilerParams(dimension_semantics=("parallel",),
                                             vmem_limit_bytes=VMEM_LIMIT),
        name="adaln",
    )(cc, w, b.reshape(1, cols))


def _project_q(hb, wqt_ref, rope_refs, qt_ref):
    L = SCAN_CHUNK
    n_chunks = hb.shape[0] // L
    qt = _dot_nt(wqt_ref[...], hb)
    for hd in range(HEADS):
        sl = slice(hd * HEAD_DIM, (hd + 1) * HEAD_DIM)
        t = qt[sl, :]
        if rope_refs is not None:
            cost_ref, sint_ref = rope_refs[2], rope_refs[3]
            half = HEAD_DIM // 2
            rot = jnp.concatenate([t[half:, :], t[:half, :]], axis=0)
            t = t * cost_ref[...] + rot * sint_ref[...]
        for ci in range(n_chunks):
            qt_ref[0, ci, sl, :] = t[:, ci * L:(ci + 1) * L].astype(BF16)
    for ci in range(n_chunks):
        qt_ref[0, ci, MIX_W:, :] = qt[MIX_W:, ci * L:(ci + 1) * L].astype(BF16)


def _project_kv(x_ref, gain_ref, sh_ref, sc_ref, wk_ref, wvt_ref, gb_ref, rope_refs, k_ref, vt_ref, gr_ref):
    L = SCAN_CHUNK
    tokens = x_ref.shape[1]
    n_chunks = tokens // L
    hb = _norm_modulate(x_ref[0], gain_ref[...], sh_ref[0], sc_ref[0])

    vg =_dot_nt(wvt_ref[...], hb)
    for ci in range(n_chunks):
        vt_ref[0, ci] = vg[0:2 * MIX_W, ci * L:(ci + 1) * L].astype(BF16)

    g = vg[2 * MIX_W:, :] + gb_ref[...]
    i_pre = g[0:HEAD_DIRS, :]
    log_f = _log_sigmoid(g[HEAD_DIRS:GATE_ROWS, :])
    src = lax.broadcasted_iota(jnp.int32, (L, L), 0)
    dst = lax.broadcasted_iota(jnp.int32, (L, L), 1)
    prefix_m = (src <= dst).astype(BF16)
    suffix_m = (src >= dst).astype(BF16)
    is_fwd = lax.broadcasted_iota(jnp.int32, (HEAD_DIRS, 1), 0) < HEADS
    parts = _split3(jnp.concatenate([log_f, log_f], axis=0))
    stacked = jnp.concatenate([p[:, ci * L:(ci + 1) * L] for ci in range(n_chunks) for p in parts], axis=0)
    pre_all = _dot(stacked, prefix_m)
    suf_all = _dot(stacked, suffix_m)
    for ci in range(n_chunks):
        cs = slice(ci * L, (ci + 1) * L)
        pieces = [slice((3 * ci + i) * GATE_ROWS, (3 * ci + i) * GATE_ROWS + HEAD_DIRS) for i in range(3)]
        pre = sum(pre_all[rows] for rows in pieces)
        suf = sum(suf_all[rows] for rows in pieces)
        cum = jnp.where(is_fwd, pre, suf)
        gr_ref[0, ci] = jnp.concatenate([i_pre[:, cs] - cum, cum], axis=0)

    k = _dot(hb, wk_ref[...]) * KEY_SCALE
    for hd in range(HEADS):
        sl = slice(hd * HEAD_DIM, (hd + 1) * HEAD_DIM)
        t = k[:, sl]
        if rope_refs is not None:
            cos_ref, sin_ref = rope_refs[0], rope_refs[1]
            t = t * cos_ref[...] + pltpu.roll(t, HEAD_DIM // 2, 1) * sin_ref[...]
        k_ref[0, :, sl] = t.astype(BF16)
    k_ref[0, :, MIX_W:] = k[:, MIX_W:].astype(BF16)
    return hb


S_SHAPE = (2, HEADS, HEAD_DIM, HEAD_DIM)
C_SHAPE = (2, HEADS, HEAD_DIM + ONES_ROWS, HEAD_DIM)
M_SHAPE = (2 * HEADS, HEAD_DIM)


def _chunk_sources(rc_ref, k_ref, vt_ref, gr_ref, ci, d):
    L = SCAN_CHUNK
    cs = slice(ci * L, (ci + 1) * L)
    pos = lax.broadcasted_iota(jnp.int32, (1, L), 1).astype(F32)
    ones = jnp.ones((ONES_ROWS, L), BF16)
    a = gr_ref[0, ci, HEADS * d:HEADS * (d + 1), :]
    cum = gr_ref[0, ci, HEAD_DIRS + HEADS * d:HEAD_DIRS + HEADS * (d + 1), :]
    edge = cum[:, L - 1:L] if d == 0 else cum[:, 0:1]
    b_last = jnp.broadcast_to(edge, (HEADS, HEAD_DIM))
    amax = jnp.broadcast_to(jnp.max(a, axis=1, keepdims=True), (HEADS, HEAD_DIM))
    w_loc = jnp.exp(a - jnp.concatenate([amax] * (L // HEAD_DIM), axis=1))
    u_ret, u_ml = [], []
    for hd in range(HEADS):
        sl = slice(hd * HEAD_DIM, (hd + 1) * HEAD_DIM)
        ml = slice(MIX_W + hd * HEAD_DIM, MIX_W + (hd + 1) * HEAD_DIM)
        lg = rc_ref[HEADS * d + hd]
        to_end = jnp.exp(lg * ((L - 1.0) - pos)) if d == 0 else jnp.exp(lg * pos)
        u_ret.append(_dot(vt_ref[0, ci, sl, :] * to_end.astype(BF16), k_ref[0, cs, sl]))
        v_ext = jnp.concatenate([vt_ref[0, ci, ml, :], ones], axis=0)
        u_ml.append(_dot(v_ext * w_loc[hd:hd + 1, :].astype(BF16), k_ref[0, cs, ml]))
    return amax, b_last, u_ret, u_ml


def _advance_state(rc_ref, s_scr, c_scr, m_scr, d, amax, b_last, u_ret, u_ml, emit):
    rows = slice(HEADS * d, HEADS * (d + 1))
    m_old = m_scr[rows, :]
    m_mid = jnp.maximum(m_old, amax)
    w_old = jnp.exp(m_old - m_mid)
    w_new = jnp.exp(amax - m_mid)
    if emit is not None:
        s_out, c_out, m_out, slot = emit
        m_out[0, slot] = m_old
    m_scr[rows, :] = b_last + m_mid
    for hd in range(HEADS):
        s_prev = s_scr[d, hd]
        c_prev = c_scr[d, hd]
        if emit is not None:
            s_out[0, slot, hd] = s_prev.astype(BF16)
            c_out[0, slot, hd] = c_prev.astype(BF16)
        s_scr[d, hd] = rc_ref[2 * HEADS + HEADS * d + hd] * s_prev + u_ret[hd]
        c_scr[d, hd] = w_old[hd:hd + 1, :] * c_prev + w_new[hd:hd + 1, :] * u_ml[hd]


def _ctx_kernel(rc_ref, x_ref, gain_ref, sh_ref, sc_ref, wk_ref, wvt_ref, gb_ref,
                s_fin, c_fin, m_fin, k_scr, vt_scr, gr_scr, s_scr, c_scr, m_scr, *, group):
    L = SCAN_CHUNK
    _project_kv(x_ref, gain_ref, sh_ref, sc_ref, wk_ref, wvt_ref, gb_ref, None, k_scr, vt_scr, gr_scr)
    per_element = x_ref.shape[1] // L // group
    for g in range(group):
        s_scr[...] = jnp.zeros_like(s_scr)
        c_scr[...] = jnp.zeros_like(c_scr)
        m_scr[...] = jnp.zeros_like(m_scr)
        own = range(g * per_element, (g + 1) * per_element)
        for d in range(2):
            for ci in (own if d == 0 else reversed(own)):
                _advance_state(rc_ref, s_scr, c_scr, m_scr, d,
                               *_chunk_sources(rc_ref, k_scr, vt_scr, gr_scr, ci, d), None)
        s_fin[g] = s_scr[...]
        c_fin[g] = c_scr[...]
        m_fin[g] = m_scr[...]


def _ctx_states(ret_consts, ctx, gain, shift, scale, wk, wvt, gbias):
    b, n, d = ctx.shape
    group = CTX_GROUP if b % CTX_GROUP == 0 else 1
    tokens = group * n
    nck = tokens // SCAN_CHUNK
    blk = lambda shape: pl.BlockSpec((group,) + shape, lambda i: (i,) + (0,) * len(shape))
    return pl.pallas_call(
        functools.partial(_ctx_kernel, group=group),
        grid=(b // group,),
        in_specs=[pl.BlockSpec(memory_space=pltpu.SMEM),
                  pl.BlockSpec((1, tokens, d), lambda i: (i, 0, 0)),
                  _resident((1, d)), _resident((1, 1, d)), _resident((1, 1, d)),
                  _resident(wk.shape), _resident(wvt.shape), _resident(gbias.shape)],
        out_specs=(blk(S_SHAPE), blk(C_SHAPE), blk(M_SHAPE)),
        out_shape=tuple(jax.ShapeDtypeStruct((b,) + shape, F32) for shape in (S_SHAPE, C_SHAPE, M_SHAPE)),
        scratch_shapes=[pltpu.VMEM((1, tokens, 2 * MIX_W), BF16),
                        pltpu.VMEM((1, nck, 2 * MIX_W, SCAN_CHUNK), BF16),
                        pltpu.VMEM((1, nck, GATE_ROWS, SCAN_CHUNK), F32),
                        pltpu.VMEM(S_SHAPE, F32), pltpu.VMEM(C_SHAPE, F32), pltpu.VMEM(M_SHAPE, F32)],
        compiler_params=pltpu.CompilerParams(dimension_semantics=("arbitrary",),
                                             vmem_limit_bytes=VMEM_LIMIT),
        name="ctx_states",
    )(ret_consts, ctx.reshape(b // group, tokens, d), gain, shift, scale, wk, wvt, gbias)


def _proj_scan_kernel(rc_ref, x_ref, gain_ref, sh_ref, sc_ref, wqt_ref, wk_ref, wvt_ref, gb_ref,
                      cos_ref, sin_ref, cost_ref, sint_ref, s0_ref, c0_ref, m0_ref,
                      qt_ref, k_ref, vt_ref, gr_ref, sf_ref, cf_ref, mf_ref, sb_ref, cb_ref, mb_ref,
                      s_scr, c_scr, m_scr, ub_ret, ub_ml, ub_stat, *, n_tiles, bwd_chunks):
    L = SCAN_CHUNK
    j = pl.program_id(1)
    per_tile = x_ref.shape[1] // L
    n_chunks = n_tiles * per_tile

    @pl.when(j == 0)
    def _():
        s_scr[...] = s0_ref[0]
        c_scr[...] = c0_ref[0]
        m_scr[...] = m0_ref[0]

    @pl.when(j < n_tiles)
    def _():
        rope_refs = (cos_ref, sin_ref, cost_ref, sint_ref)
        hb = _project_kv(x_ref, gain_ref, sh_ref, sc_ref, wk_ref, wvt_ref, gb_ref, rope_refs,
                         k_ref, vt_ref, gr_ref)
        _project_q(hb, wqt_ref, rope_refs, qt_ref)
        for ci in range(per_tile):
            _advance_state(rc_ref, s_scr, c_scr, m_scr, 0, *_chunk_sources(rc_ref, k_ref, vt_ref, gr_ref, ci, 0),
                           (sf_ref, cf_ref, mf_ref, ci))
            amax, b_last, u_ret, u_ml = _chunk_sources(rc_ref, k_ref, vt_ref, gr_ref, ci, 1)
            chunk = j * per_tile + ci
            ub_stat[chunk] = jnp.concatenate([amax, b_last], axis=0)
            for hd in range(HEADS):
                ub_ret[chunk, hd] = u_ret[hd].astype(BF16)
                ub_ml[chunk, hd] = u_ml[hd].astype(BF16)

    @pl.when(j >= n_tiles)
    def _():
        first = n_chunks - 1 - (j - n_tiles) * bwd_chunks
        for i in range(bwd_chunks):
            chunk = first - i
            stat = ub_stat[chunk]
            _advance_state(rc_ref, s_scr, c_scr, m_scr, 1, stat[0:HEADS], stat[HEADS:2 * HEADS],
                           [ub_ret[chunk, hd].astype(F32) for hd in range(HEADS)],
                           [ub_ml[chunk, hd].astype(F32) for hd in range(HEADS)],
                           (sb_ref, cb_ref, mb_ref, bwd_chunks - 1 - i))


def _proj_scan(ret_consts, x, gain, shift, scale, wqt, wk, wvt, gbias, rope, init):
    b, n, d = x.shape
    L = SCAN_CHUNK
    t = PROJ_TOKENS
    nt = n // t
    nc = n // L
    per_tile = t // L
    nbb = nc // BWD_CHUNKS
    tile = lambda j: jnp.minimum(j, nt - 1)
    bblk = lambda j: nbb - 1 - jnp.maximum(j - nt, 0)
    tok3 = lambda i, j: (i, tile(j), 0)
    chunk4 = lambda i, j: (i, tile(j), 0, 0)
    mod3 = lambda i, j: (i, 0, 0)
    per_b = lambda shape: pl.BlockSpec((1,) + shape, lambda i, j: (i,) + (0,) * len(shape))
    cos, sin, cos_t, sin_t = rope
    in_specs = [pl.BlockSpec(memory_space=pltpu.SMEM),
                pl.BlockSpec((1, t, d), tok3),
                _resident((1, d)),
                pl.BlockSpec((1, 1, d), mod3), pl.BlockSpec((1, 1, d), mod3),
                _resident(wqt.shape), _resident(wk.shape), _resident(wvt.shape), _resident(gbias.shape),
                pl.BlockSpec((t, HEAD_DIM), lambda i, j: (tile(j), 0)),
                pl.BlockSpec((t, HEAD_DIM), lambda i, j: (tile(j), 0)),
                pl.BlockSpec((HEAD_DIM, t), lambda i, j: (0, tile(j))),
                pl.BlockSpec((HEAD_DIM, t), lambda i, j: (0, tile(j))),
                per_b(S_SHAPE), per_b(C_SHAPE), per_b(M_SHAPE)]
    out_shape = [jax.ShapeDtypeStruct((b, nc, 2 * MIX_W, L), BF16),
                 jax.ShapeDtypeStruct((b, n, 2 * MIX_W), BF16),
                 jax.ShapeDtypeStruct((b, nc, 2 * MIX_W, L), BF16),
                 jax.ShapeDtypeStruct((b, nc, GATE_ROWS, L), F32)]
    out_specs = [pl.BlockSpec((1, per_tile, 2 * MIX_W, L), chunk4),
                 pl.BlockSpec((1, t, 2 * MIX_W), tok3),
                 pl.BlockSpec((1, per_tile, 2 * MIX_W, L), chunk4),
                 pl.BlockSpec((1, per_tile, GATE_ROWS, L), chunk4)]
    for blk, step in ((per_tile, tile), (BWD_CHUNKS, bblk)):
        out_shape += [jax.ShapeDtypeStruct((b, nc) + S_SHAPE[1:], BF16),
                      jax.ShapeDtypeStruct((b, nc) + C_SHAPE[1:], BF16),
                      jax.ShapeDtypeStruct((b, nc, HEADS, HEAD_DIM), F32)]
        out_specs += [pl.BlockSpec((1, blk) + S_SHAPE[1:], lambda i, j, step=step: (i, step(j), 0, 0, 0)),
                      pl.BlockSpec((1, blk) + C_SHAPE[1:], lambda i, j, step=step: (i, step(j), 0, 0, 0)),
                      pl.BlockSpec((1, blk, HEADS, HEAD_DIM), lambda i, j, step=step: (i, step(j), 0, 0))]
    return pl.pallas_call(
        functools.partial(_proj_scan_kernel, n_tiles=nt, bwd_chunks=BWD_CHUNKS),
        grid=(b, nt + nbb),
        in_specs=in_specs,
        out_specs=tuple(out_specs),
        out_shape=tuple(out_shape),
        scratch_shapes=[pltpu.VMEM(S_SHAPE, F32), pltpu.VMEM(C_SHAPE, F32), pltpu.VMEM(M_SHAPE, F32),
                        pltpu.VMEM((nc,) + S_SHAPE[1:], BF16), pltpu.VMEM((nc,) + C_SHAPE[1:], BF16),
                        pltpu.VMEM((nc,) + M_SHAPE, F32)],
        compiler_params=pltpu.CompilerParams(dimension_semantics=("arbitrary", "arbitrary"),
                                             vmem_limit_bytes=VMEM_LIMIT),
        name="proj_scan",
    )(ret_consts, x, gain, shift, scale, wqt, wk, wvt, gbias, cos, sin, cos_t, sin_t, *init)


def _mixer_kernel(rc_ref, x_ref, gain_ref, sh_ref, sc_ref, g1_ref, qt_ref, k_ref, vt_ref, gr_ref,
                  sf_ref, sb_ref, cf_ref, cb_ref, mf_ref, mb_ref,
                  wgt_ref, wbr_ref, wbm_ref, wru_ref, wmu_ref, wout_ref, o_ref, dec_scr, qdec_scr):
    L = SCAN_CHUNK
    key = lax.broadcasted_iota(jnp.int32, (L, L), 0)
    qry = lax.broadcasted_iota(jnp.int32, (L, L), 1)

    @pl.when((pl.program_id(0) == 0) & (pl.program_id(1) == 0))
    def _():
        rel = (qry - key).astype(F32)
        lpos = lax.broadcasted_iota(jnp.int32, (8, L), 1).astype(F32)
        first_row = lax.broadcasted_iota(jnp.int32, (8, L), 0) == 0
        for hd in range(HEADS):
            lg_f = rc_ref[hd]
            lg_b = rc_ref[HEADS + hd]
            dec_scr[hd] = (jnp.where(rel >= 0, jnp.exp(lg_f * jnp.maximum(rel, 0.0)), 0.0)
                           + jnp.where(rel <= 0, jnp.exp(lg_b * jnp.maximum(-rel, 0.0)), 0.0))
            qdec_scr[hd] = jnp.where(first_row, jnp.exp(lg_f * (lpos + 1.0)), jnp.exp(lg_b * (L - lpos)))

    d_model = x_ref.shape[-1]
    bg_w = d_model // HEADS
    masks = (key <= qry, key >= qry)
    c_refs, m_refs = (cf_ref, cb_ref), (mf_ref, mb_ref)
    ones = jnp.ones((ONES_ROWS, L), BF16)
    neg_inf = jnp.float32(-jnp.inf)
    n_chunks = x_ref.shape[1] // L
    a_cols = [(gr_ref[0, ci, 0:HEAD_DIRS, :] * LOG2E).T for ci in range(n_chunks)]
    scanned, y_ret, y_ml, bg_ret, bg_ml = [], [], [], [], []
    normed = []

    def gate_head(hd):
        if not normed:
            normed.append(_norm_modulate(x_ref[0], gain_ref[...], sh_ref[0], sc_ref[0]))
        hb = normed[0]
        gates = _dot_nt(wgt_ref[2 * hd * HEAD_DIM:2 * (hd + 1) * HEAD_DIM, :], hb)
        bg_ret.append(_dot(hb, wbr_ref[:, hd * bg_w:(hd + 1) * bg_w]))
        bg_ml.append(_dot(hb, wbm_ref[:, hd * bg_w:(hd + 1) * bg_w]))
        ret_t, ml_t = scanned[hd]
        y_ret.append(ret_t * _silu(gates[:HEAD_DIM, :]))
        y_ml.append(_rms(_sigmoid(gates[HEAD_DIM:, :]) * ml_t, axis=0))

    chunks = range(n_chunks)
    toks = [slice(ci * L, (ci + 1) * L) for ci in chunks]

    def first_wave(hd):
        sl = slice(hd * HEAD_DIM, (hd + 1) * HEAD_DIM)
        ml = slice(MIX_W + hd * HEAD_DIM, MIX_W + (hd + 1) * HEAD_DIM)
        qt_r = [qt_ref[0, ci, sl, :] for ci in chunks]
        qt_m = [qt_ref[0, ci, ml, :] for ci in chunks]
        s_ret = [_dot(k_ref[0, toks[ci], sl], qt_r[ci]) for ci in chunks]
        s_ml = [_dot(k_ref[0, toks[ci], ml], qt_m[ci]) for ci in chunks]
        inter_ret, inter_ml = [], []
        for ci in chunks:
            qw = jnp.concatenate([qt_r[ci] * qdec_scr[hd, 0:1, :].astype(BF16),
                                  qt_r[ci] * qdec_scr[hd, 1:2, :].astype(BF16)], axis=0)
            st = jnp.concatenate([sf_ref[0, ci, hd], sb_ref[0, ci, hd]], axis=1)
            inter_ret.append(_dot(st, qw))
            inter_ml.append([_dot(c_refs[d][0, ci, hd], qt_m[ci]) for d in range(2)])
        return s_ret, s_ml, inter_ret, inter_ml

    wave = first_wave(0)
    for hd in range(HEADS):
        sl = slice(hd * HEAD_DIM, (hd + 1) * HEAD_DIM)
        ml = slice(MIX_W + hd * HEAD_DIM, MIX_W + (hd + 1) * HEAD_DIM)
        s_ret, s_ml, inter_ret, inter_ml = wave
        if hd + 1 < HEADS:
            wave = first_wave(hd + 1)
        p_ret = [(s_ret[ci] * dec_scr[hd]).astype(BF16) for ci in chunks]
        p_ml, m_locs, m_ins = [], [], []
        for ci in chunks:
            for d in range(2):
                pair = HEADS * d + hd
                a_col = a_cols[ci][:, pair:pair + 1]
                m_in = jnp.concatenate([m_refs[d][0, ci, hd:hd + 1, :] * LOG2E] * (L // HEAD_DIM), axis=1)
                a_masked = jnp.where(masks[d], a_col, neg_inf)
                m_loc = jnp.maximum(jnp.max(a_masked, axis=0, keepdims=True), m_in)
                p_ml.append((s_ml[ci] * jnp.exp2(a_masked - m_loc)).astype(BF16))
                m_locs.append(m_loc)
                m_ins.append(m_in)
        ret_cols, ml_cols = [], []
        for ci in chunks:
            ret_cols.append(_rms(_dot(vt_ref[0, ci, sl, :], p_ret[ci]) + inter_ret[ci], axis=0))
            v_ext = jnp.concatenate([vt_ref[0, ci, ml, :], ones], axis=0)
            hsum = None
            for d in range(2):
                m_loc, m_in = m_locs[2 * ci + d], m_ins[2 * ci + d]
                pair = HEAD_DIRS + HEADS * d + hd
                cum_row = gr_ref[0, ci, pair:pair + 1, :] * LOG2E
                tot = _dot(v_ext, p_ml[2 * ci + d]) + jnp.exp2(m_in - m_loc) * inter_ml[ci][d]
                den = jnp.maximum(jnp.abs(tot[HEAD_DIM:HEAD_DIM + 1, :]), jnp.exp2(-(cum_row + m_loc)))
                hd_out = tot[0:HEAD_DIM, :] * (1.0 / den)
                hsum = hd_out if hsum is None else hsum + hd_out
            ml_cols.append(hsum)
        scanned.append((jnp.concatenate(ret_cols, axis=1), jnp.concatenate(ml_cols, axis=1)))
        if hd >= 1:
            gate_head(hd - 1)
    gate_head(HEADS - 1)

    y_ret = jnp.concatenate(y_ret, axis=0).astype(BF16)
    y_ml = jnp.concatenate(y_ml, axis=0).astype(BF16)
    bg_ret = jnp.concatenate(bg_ret, axis=1)
    bg_ml = jnp.concatenate(bg_ml, axis=1)
    merged = _sigmoid(bg_ret) * _dot_tn(y_ret, wru_ref[...]) + _sigmoid(bg_ml) * _dot_tn(y_ml, wmu_ref[...])
    o_ref[0] = x_ref[0] + g1_ref[0] * _dot(merged.astype(BF16), wout_ref[...])


def _mixer(ret_consts, x, gain, shift, scale, gate1, qt, k, vt, gr, chunk_states, wgt, wbr, wbm, wru, wmu, wout):
    b, n, d = x.shape
    L = SCAN_CHUNK
    t = MIX_TOKENS
    ch = t // L
    sf, sb, cf, cb, mf, mb = chunk_states
    tok3 = lambda i, j: (i, j, 0)
    mod3 = lambda i, j: (i, 0, 0)
    st5 = lambda i, j: (i, j, 0, 0, 0)
    st4 = lambda i, j: (i, j, 0, 0)
    in_specs = [pl.BlockSpec(memory_space=pltpu.SMEM),
                pl.BlockSpec((1, t, d), tok3),
                _resident((1, d)),
                pl.BlockSpec((1, 1, d), mod3), pl.BlockSpec((1, 1, d), mod3), pl.BlockSpec((1, 1, d), mod3),
                pl.BlockSpec((1, ch, 2 * MIX_W, L), st4),
                pl.BlockSpec((1, t, 2 * MIX_W), tok3),
                pl.BlockSpec((1, ch, 2 * MIX_W, L), st4),
                pl.BlockSpec((1, ch, GATE_ROWS, L), st4),
                pl.BlockSpec((1, ch) + S_SHAPE[1:], st5),
                pl.BlockSpec((1, ch) + S_SHAPE[1:], st5),
                pl.BlockSpec((1, ch) + C_SHAPE[1:], st5),
                pl.BlockSpec((1, ch) + C_SHAPE[1:], st5),
                pl.BlockSpec((1, ch, HEADS, HEAD_DIM), st4),
                pl.BlockSpec((1, ch, HEADS, HEAD_DIM), st4),
                _resident(wgt.shape),
                _resident(wbr.shape),
                _resident(wbm.shape),
                _resident(wru.shape),
                _resident(wmu.shape),
                _resident(wout.shape)]
    return pl.pallas_call(
        _mixer_kernel,
        grid=(b, n // t),
        in_specs=in_specs,
        out_specs=pl.BlockSpec((1, t, d), tok3),
        out_shape=jax.ShapeDtypeStruct((b, n, d), F32),
        scratch_shapes=[pltpu.VMEM((HEADS, L, L), F32), pltpu.VMEM((HEADS, 8, L), F32)],
        compiler_params=pltpu.CompilerParams(dimension_semantics=("arbitrary", "arbitrary"),
                                             vmem_limit_bytes=VMEM_LIMIT),
        name="mixer",
    )(ret_consts, x, gain, shift, scale, gate1, qt, k, vt, gr, sf, sb, cf, cb, mf, mb,
      wgt, wbr, wbm, wru, wmu, wout)


def _ffn_kernel(x_ref, gain_ref, sh_ref, sc_ref, g2_ref, fgain_ref, w1_ref, w2_ref, o_ref, *, hidden):
    xf = x_ref[0]
    hb = _norm_modulate(xf, gain_ref[...], sh_ref[0], sc_ref[0])
    acc = None
    start = 0
    for width in FFN_SPLITS:
        gate = _dot(hb, w1_ref[:, start:start + width])
        up = _dot(hb, w1_ref[:, hidden + start:hidden + start + width])
        act = (_silu(gate) * up).astype(BF16)
        part = _dot(act, w2_ref[start:start + width, :])
        acc = part if acc is None else acc + part
        start += width
    o_ref[0] = _rms(xf + g2_ref[0] * acc) * fgain_ref[...]


def _ffn(x, gain, shift, scale, gate2, final_gain, w1, w2):
    b, n, d = x.shape
    t = FFN_TOKENS
    hidden = w2.shape[0]
    assert sum(FFN_SPLITS) == hidden
    tok3 = lambda i, j: (i, j, 0)
    mod3 = lambda i, j: (i, 0, 0)
    return pl.pallas_call(
        functools.partial(_ffn_kernel, hidden=hidden),
        grid=(b, n // t),
        in_specs=[pl.BlockSpec((1, t, d), tok3),
                  _resident((1, d)),
                  pl.BlockSpec((1, 1, d), mod3), pl.BlockSpec((1, 1, d), mod3), pl.BlockSpec((1, 1, d), mod3),
                  _resident((1, d)),
                  _resident(w1.shape),
                  _resident(w2.shape)],
        out_specs=pl.BlockSpec((1, t, d), tok3),
        out_shape=jax.ShapeDtypeStruct((b, n, d), F32),
        compiler_params=pltpu.CompilerParams(dimension_semantics=("parallel", "parallel"),
                                             vmem_limit_bytes=VMEM_LIMIT),
        name="ffn",
    )(x, gain, shift, scale, gate2, final_gain, w1, w2)


def _rope_tables(n):
    n_rows = n // GRID_W
    n_freq = HEAD_DIM // 4
    inv = ROPE_BASE ** (-jnp.arange(n_freq, dtype=F32) / n_freq)
    row_ang = inv[:, None] * jnp.arange(n_rows, dtype=F32)[None, :]
    col_ang = inv[:, None] * jnp.arange(GRID_W, dtype=F32)[None, :]
    per_row = lambda t: jnp.repeat(t, GRID_W, axis=1)
    per_col = lambda t: jnp.tile(t, (1, n_rows))
    cos_t = jnp.concatenate([per_row(jnp.cos(row_ang)), per_col(jnp.cos(col_ang))], axis=0)
    sin_t = jnp.concatenate([per_row(jnp.sin(row_ang)), per_col(jnp.sin(col_ang))], axis=0)
    cos2_t = jnp.concatenate([cos_t, cos_t], axis=0)
    sin2_t = jnp.concatenate([-sin_t, sin_t], axis=0)
    return cos2_t.T, sin2_t.T, cos2_t, sin2_t


def kernel(x, c, ctx, c_ctx, w_ada, b_ada, norm1_gain, norm2_gain, w_in, mlstm_gate_bias, ret_decay_logit,
           w_ret_up, w_ml_up, w_out, w_ffn_in, w_ffn_out, final_gain):
    assert w_ada.shape[0] == 1, "single-layer block"
    b, n, d = x.shape
    assert n % PROJ_TOKENS == 0 and n % FFN_TOKENS == 0 and n % MIX_TOKENS == 0
    assert PROJ_TOKENS % SCAN_CHUNK == 0 and MIX_TOKENS % SCAN_CHUNK == 0 and ctx.shape[1] % SCAN_CHUNK == 0
    assert (n // SCAN_CHUNK) % BWD_CHUNKS == 0

    rows = -(-(b + 1) // 16) * 16
    cc = jnp.concatenate([c, c_ctx[None, :], jnp.zeros((rows - b - 1, d), F32)], axis=0)
    mod = _adaln(cc, w_ada[0], b_ada[0])
    sh1, sc1, g1, sh2, sc2, g2 = (mod[:b, i * d:(i + 1) * d].reshape(b, 1, d) for i in range(6))
    csh1 = mod[b, 0:d].reshape(1, 1, d)
    csc1 = mod[b, d:2 * d].reshape(1, 1, d)

    w = w_in[0]
    o = [0]
    for width in (MIX_W, MIX_W, MIX_W, MIX_W, MIX_W, MIX_W, MIX_W, MIX_W, GATE_ROWS, d, d):
        o.append(o[-1] + width)
    col = lambda i: w[:, o[i]:o[i + 1]]
    gates = col(8).reshape(d, 4, HEADS)
    gates = jnp.concatenate([gates[:, 0], gates[:, 2], gates[:, 1], gates[:, 3]], axis=1)
    gb = mlstm_gate_bias[0]
    gbias = jnp.concatenate([gb[0], gb[2], gb[1], gb[3]]).reshape(GATE_ROWS, 1).astype(F32)
    wqt = jnp.concatenate([col(0), col(4)], axis=1).T.astype(BF16)
    wk = jnp.concatenate([col(1), col(5)], axis=1).astype(BF16)
    wvt = jnp.concatenate([col(2), col(6), gates], axis=1).T.astype(BF16)
    bg_w = d // HEADS
    wgt = jnp.stack([col(3).T.reshape(HEADS, HEAD_DIM, d), col(7).T.reshape(HEADS, HEAD_DIM, d)],
                    axis=1).reshape(2 * MIX_W, d).astype(BF16)
    wbr = col(9).astype(BF16)
    wbm = col(10).astype(BF16)

    log_gamma = jax.nn.log_sigmoid(ret_decay_logit[0].astype(F32)).reshape(2 * HEADS)
    ret_consts = jnp.concatenate([log_gamma, jnp.exp(log_gamma * SCAN_CHUNK)])

    gain1 = norm1_gain[0].reshape(1, d)
    ctx_final = _ctx_states(ret_consts, ctx, gain1, csh1, csc1, wk, wvt, gbias)
    qt_x, k_x, vt_x, gr_x, sf, cf, mf, sb, cb, mb = _proj_scan(
        ret_consts, x, gain1, sh1, sc1, wqt, wk, wvt, gbias, _rope_tables(n), ctx_final)
    x1 = _mixer(ret_consts, x, gain1, sh1, sc1, g1, qt_x, k_x, vt_x, gr_x, (sf, sb, cf, cb, mf, mb),
                wgt, wbr, wbm, w_ret_up[0].astype(BF16), w_ml_up[0].astype(BF16), w_out[0].astype(BF16))
    return _ffn(x1, norm2_gain[0].reshape(1, d), sh2, sc2, g2, final_gain.reshape(1, d),
                w_ffn_in[0].astype(BF16), w_ffn_out[0].astype(BF16))
```
